```python
import jax, jax.numpy as jnp
from jax import lax
import numpy as np

D_MODEL = 1024
BATCH = 32
SEQ = 256
DEPTH = 4
DEC_BATCH = 4
DEC_SEQ = 4096
PAST_LEN = 512

GRID_W = 64
N_MIXERS = 3
POOL_GROUPS = 4
POOL_WINDOWS = (2, 4, 8, 16)
POOL_CH = D_MODEL // POOL_GROUPS
CONV_W = 3
MLA_HEADS = 8
QK_NOPE = 128
QK_ROPE = 64
V_DIM = 128
Q_RANK = 512
KV_RANK = 256
AXIS_FREQS = QK_ROPE // 4
ROPE_BASE = 10000.0
SM_SCALE = (QK_NOPE + QK_ROPE) ** -0.5
Q_BLOCK = 128
N_GROUPS = 4
EXP_PER_GROUP = 8
N_EXPERTS = N_GROUPS * EXP_PER_GROUP
TOP_K = 2
D_EXPERT = 512
MOE_BLOCK = 128
EPS = 1e-6
N_POOL = len(range(0, DEPTH, N_MIXERS))
N_CONV = len(range(1, DEPTH, N_MIXERS))
N_MLA = len(range(2, DEPTH, N_MIXERS))

kernel_name = "hybrid_flow_pool_conv_mla_hmoe_step"


def rmsnorm(x, g):
    xf = x.astype(jnp.float32)
    y = xf * lax.rsqrt(jnp.mean(xf * xf, axis=-1, keepdims=True) + EPS)
    return (y * g.astype(jnp.float32)).astype(x.dtype)


def modulation(cvec, w, b):
    m = jax.nn.silu(cvec) @ w + b
    return tuple(s[:, None, :] for s in jnp.split(m, 6, axis=-1))


def modulate(h, shift, scale):
    return h * (1 + scale) + shift


def axial_rope_tables(n_tokens):
    rows_n = n_tokens // GRID_W
    rows = jnp.repeat(jnp.arange(rows_n), GRID_W).astype(jnp.float32)
    cols = jnp.tile(jnp.arange(GRID_W), rows_n).astype(jnp.float32)
    inv = ROPE_BASE ** (-(jnp.arange(AXIS_FREQS, dtype=jnp.float32) / AXIS_FREQS))
    ang = jnp.stack([rows[:, None] * inv, cols[:, None] * inv], axis=1)
    return jnp.cos(ang), jnp.sin(ang)


def apply_axial_rope(x, cos, sin):
    xr = x.reshape(x.shape[:-1] + (2, 2, AXIS_FREQS)).astype(jnp.float32)
    a, b = xr[..., 0, :], xr[..., 1, :]
    out = jnp.stack([a * cos - b * sin, a * sin + b * cos], axis=-2)
    return out.reshape(x.shape).astype(x.dtype)


def centred_mean(x, w):
    L = x.shape[1]
    lo = w // 2
    hi = w - 1 - lo
    xp = jnp.pad(x.astype(jnp.float32), ((0, 0), (lo + 1, hi), (0, 0)))
    cs = jnp.cumsum(xp, axis=1)
    s = cs[:, w:w + L] - cs[:, :L]
    t = jnp.arange(L)
    cnt = (jnp.minimum(t + hi, L - 1) - jnp.maximum(t - lo, 0) + 1).astype(jnp.float32)
    return (s / cnt[None, :, None]).astype(x.dtype)


def pool_mixer(h, w_pool, scale):
    outs = []
    for g, win in enumerate(POOL_WINDOWS):
        xg = h[..., g * POOL_CH:(g + 1) * POOL_CH]
        outs.append((centred_mean(xg, win) - xg) @ w_pool[g])
    return jnp.concatenate(outs, axis=-1) * scale


def conv_mixer(h, w_in, conv_w, w_out):
    b_gate, c_gate, v = jnp.split(h @ w_in, 3, axis=-1)
    u = c_gate * v
    L = u.shape[1]
    up = jnp.pad(u, ((0, 0), (1, 1), (0, 0)))
    conv = up[:, :L] * conv_w[0] + up[:, 1:L + 1] * conv_w[1] + up[:, 2:] * conv_w[2]
    return (b_gate * conv) @ w_out


def mla_query(h, w_dq, q_norm, w_uq):
    B, L, _ = h.shape
    q = (rmsnorm(h @ w_dq, q_norm) @ w_uq).reshape(B, L, MLA_HEADS, QK_NOPE + QK_ROPE)
    return q[..., :QK_NOPE], q[..., QK_NOPE:]


def mla_compress(h, w_dkv, kv_norm):
    kv = h @ w_dkv
    return rmsnorm(kv[..., :KV_RANK], kv_norm), kv[..., KV_RANK:]


def mla_expand(ckv, w_ukv):
    B, Lk, _ = ckv.shape
    kv = (ckv @ w_ukv).reshape(B, Lk, MLA_HEADS, QK_NOPE + V_DIM)
    return kv[..., :QK_NOPE], kv[..., QK_NOPE:]


def mla_attend(q_nope, q_pe, k_nope, k_pe, v, w_o):
    B, L = q_nope.shape[:2]
    nb = L // Q_BLOCK
    qn = q_nope.reshape(B, nb, Q_BLOCK, MLA_HEADS, QK_NOPE).transpose(1, 0, 2, 3, 4)
    qp = q_pe.reshape(B, nb, Q_BLOCK, MLA_HEADS, QK_ROPE).transpose(1, 0, 2, 3, 4)

    def block(args):
        qn_b, qp_b = args
        s = jnp.einsum('bqhd,bkhd->bhqk', qn_b, k_nope) + jnp.einsum('bqhr,bkr->bhqk', qp_b, k_pe)
        p = jax.nn.softmax(s.astype(jnp.float32) * SM_SCALE, axis=-1).astype(v.dtype)
        return jnp.einsum('bhqk,bkhd->bqhd', p, v)

    o = lax.map(block, (qn, qp))
    return o.transpose(1, 0, 2, 3, 4).reshape(B, L, MLA_HEADS * V_DIM) @ w_o


def hier_route(x, w_rg, b_rg, w_re, b_re):
    N = x.shape[0]
    g_prob = jax.nn.softmax((x @ w_rg + b_rg).astype(jnp.float32), axis=-1)
    g_p, g_idx = lax.top_k(g_prob, 1)
    e_logits = (x @ w_re + b_re).astype(jnp.float32).reshape(N, N_GROUPS, EXP_PER_GROUP)
    e_sel = jnp.take_along_axis(e_logits, g_idx[:, :, None], axis=1)[:, 0]
    e_p, e_idx = lax.top_k(jax.nn.softmax(e_sel, axis=-1), TOP_K)
    wts = g_p * e_p / jnp.sum(e_p, axis=-1, keepdims=True)
    return g_idx * EXP_PER_GROUP + e_idx, wts


def moe_ffn(h, w_rg, b_rg, w_re, b_re, w1, w3, w2):
    B, L, D = h.shape
    N = B * L
    x = h.reshape(N, D)
    ids, wts = hier_route(x, w_rg, b_rg, w_re, b_re)
    A = N * TOP_K
    flat_e = ids.reshape(A)
    flat_w = wts.reshape(A)
    flat_tok = jnp.arange(A, dtype=jnp.int32) // TOP_K
    order = jnp.argsort(flat_e)
    sorted_e = flat_e[order]
    sizes = jnp.bincount(flat_e, length=N_EXPERTS)
    starts = jnp.cumsum(sizes) - sizes
    padded = (sizes + MOE_BLOCK - 1) // MOE_BLOCK * MOE_BLOCK
    pad_ends = jnp.cumsum(padded)
    pad_starts = pad_ends - padded
    dest = pad_starts[sorted_e] + jnp.arange(A) - starts[sorted_e]
    cap = A + N_EXPERTS * MOE_BLOCK
    n_blk = cap // MOE_BLOCK
    slot_tok = jnp.full((cap,), N, jnp.int32).at[dest].set(flat_tok[order])
    slot_w = jnp.zeros((cap,), jnp.float32).at[dest].set(flat_w[order])
    blk_e = jnp.minimum(jnp.searchsorted(pad_ends, jnp.arange(n_blk) * MOE_BLOCK, side='right'), N_EXPERTS - 1)
    x_pad = jnp.concatenate([x, jnp.zeros((1, D), x.dtype)], axis=0)
    xb = x_pad[slot_tok].reshape(n_blk, MOE_BLOCK, D)

    def expert_block(args):
        xe, e = args
        return (jax.nn.silu(xe @ w1[e]) * (xe @ w3[e])) @ w2[e]

    ob = lax.map(expert_block, (xb, blk_e)).reshape(cap, D)
    y = jnp.zeros((N + 1, D), x.dtype).at[slot_tok].add(ob * slot_w[:, None].astype(x.dtype))
    return y[:N].reshape(B, L, D)


def setup_inputs(seed: int = 0) -> dict:
    key = jax.random.key(seed)
    ks = jax.random.split(key, 32)
    f = jnp.float32
    D = D_MODEL

    def nrm(k, shape, fan_in, gain=1.0):
        return jax.random.normal(k, shape, f) * (gain * fan_in ** -0.5)

    def ones_noise(k, shape):
        return 1.0 + 0.05 * jax.random.normal(k, shape, f)

    return {
        "x_prompt": jax.random.normal(ks[0], (BATCH, SEQ, D), f),
        "x_sample": jax.random.normal(ks[1], (DEC_BATCH, DEC_SEQ, D), f),
        "cache_ckv": jax.random.normal(ks[2], (DEC_BATCH, N_MLA, PAST_LEN, KV_RANK), f),
        "cache_kpe": jax.random.normal(ks[3], (DEC_BATCH, N_MLA, PAST_LEN, QK_ROPE), f),
        "c": jax.random.normal(ks[4], (DEC_BATCH, D), f),
        "c_ctx": jax.random.normal(ks[5], (D,), f),
        "norm_mix": ones_noise(ks[6], (DEPTH, D)),
        "norm_ffn": ones_noise(ks[7], (DEPTH, D)),
        "norm_final": ones_noise(ks[8], (D,)),
        "w_ada": nrm(ks[9], (DEPTH, D, 6 * D), D, 0.5),
        "b_ada": 0.02 * jax.random.normal(ks[10], (DEPTH, 6 * D), f),
        "w_pool": nrm(ks[11], (N_POOL, POOL_GROUPS, POOL_CH, POOL_CH), POOL_CH),
        "pool_scale": ones_noise(ks[12], (N_POOL, D)),
        "w_conv_in": nrm(ks[13], (N_CONV, D, 3 * D), D),
        "conv_w": nrm(ks[14], (N_CONV, CONV_W, D), CONV_W),
        "w_conv_out": nrm(ks[15], (N_CONV, D, D), D),
        "w_dq": nrm(ks[16], (N_MLA, D, Q_RANK), D),
        "q_norm": ones_noise(ks[17], (N_MLA, Q_RANK)),
        "w_uq": nrm(ks[18], (N_MLA, Q_RANK, MLA_HEADS * (QK_NOPE + QK_ROPE)), Q_RANK),
        "w_dkv": nrm(ks[19], (N_MLA, D, KV_RANK + QK_ROPE), D),
        "kv_norm": ones_noise(ks[20], (N_MLA, KV_RANK)),
        "w_ukv": nrm(ks[21], (N_MLA, KV_RANK, MLA_HEADS * (QK_NOPE + V_DIM)), KV_RANK),
        "w_o": nrm(ks[22], (N_MLA, MLA_HEADS * V_DIM, D), MLA_HEADS * V_DIM),
        "w_route_g": nrm(ks[23], (DEPTH, D, N_GROUPS), D),
        "b_route_g": 0.01 * jax.random.normal(ks[24], (DEPTH, N_GROUPS), f),
        "w_route_e": nrm(ks[25], (DEPTH, D, N_EXPERTS), D),
        "b_route_e": 0.01 * jax.random.normal(ks[26], (DEPTH, N_EXPERTS), f),
        "w1": nrm(ks[27], (DEPTH, N_EXPERTS, D, D_EXPERT), D),
        "w3": nrm(ks[28], (DEPTH, N_EXPERTS, D, D_EXPERT), D),
        "w2": nrm(ks[29], (DEPTH, N_EXPERTS, D_EXPERT, D), D_EXPERT),
    }


def reference(x_prompt, x_sample, cache_ckv, cache_kpe, c, c_ctx, norm_mix, norm_ffn, norm_final,
              w_ada, b_ada, w_pool, pool_scale, w_conv_in, conv_w, w_conv_out,
              w_dq, q_norm, w_uq, w_dkv, kv_norm, w_ukv, w_o,
              w_route_g, b_route_g, w_route_e, b_route_e, w1, w3, w2):
    ctx = x_prompt
    lat = x_sample
    cos, sin = axial_rope_tables(lat.shape[1])
    new_ckv, new_kpe = [], []
    for l in range(DEPTH):
        kind = l % N_MIXERS
        j = l // N_MIXERS
        sh_c, sc_c, g_c, sh2_c, sc2_c, g2_c = modulation(c_ctx[None, :], w_ada[l], b_ada[l])
        sh_x, sc_x, g_x, sh2_x, sc2_x, g2_x = modulation(c, w_ada[l], b_ada[l])
        hc = modulate(rmsnorm(ctx, norm_mix[l]), sh_c, sc_c)
        hx = modulate(rmsnorm(lat, norm_mix[l]), sh_x, sc_x)
        if kind == 0:
            oc = pool_mixer(hc, w_pool[j], pool_scale[j])
            ox = pool_mixer(hx, w_pool[j], pool_scale[j])
        elif kind == 1:
            oc = conv_mixer(hc, w_conv_in[j], conv_w[j], w_conv_out[j])
            ox = conv_mixer(hx, w_conv_in[j], conv_w[j], w_conv_out[j])
        else:
            qn_c, qp_c = mla_query(hc, w_dq[j], q_norm[j], w_uq[j])
            ckv_c, kpe_c = mla_compress(hc, w_dkv[j], kv_norm[j])
            kn_c, v_c = mla_expand(ckv_c, w_ukv[j])
            oc = mla_attend(qn_c, qp_c, kn_c, kpe_c, v_c, w_o[j])
            new_ckv.append(ckv_c)
            new_kpe.append(kpe_c)
            qn_x, qp_x = mla_query(hx, w_dq[j], q_norm[j], w_uq[j])
            qp_x = apply_axial_rope(qp_x, cos[:, None], sin[:, None])
            ckv_x, kpe_x = mla_compress(hx, w_dkv[j], kv_norm[j])
            kpe_x = apply_axial_rope(kpe_x, cos, sin)
            ckv_all = jnp.concatenate([cache_ckv[:, j], ckv_x], axis=1)
            kpe_all = jnp.concatenate([cache_kpe[:, j], kpe_x], axis=1)
            kn_x, v_x = mla_expand(ckv_all, w_ukv[j])
            ox = mla_attend(qn_x, qp_x, kn_x, kpe_all, v_x, w_o[j])
        ctx = ctx + g_c * oc
        lat = lat + g_x * ox
        ctx = ctx + g2_c * moe_ffn(modulate(rmsnorm(ctx, norm_ffn[l]), sh2_c, sc2_c),
                                   w_route_g[l], b_route_g[l], w_route_e[l], b_route_e[l], w1[l], w3[l], w2[l])
        lat = lat + g2_x * moe_ffn(modulate(rmsnorm(lat, norm_ffn[l]), sh2_x, sc2_x),
                                   w_route_g[l], b_route_g[l], w_route_e[l], b_route_e[l], w1[l], w3[l], w2[l])
    y_prompt = rmsnorm(ctx, norm_final)
    y_sample = rmsnorm(lat, norm_final)
    new_ckv_arr = jnp.stack(new_ckv, axis=1)
    new_kpe_arr = jnp.stack(new_kpe, axis=1)
    return (y_prompt, y_sample, new_ckv_arr, new_kpe_arr)
```

```python
import functools

import numpy as np
import jax
import jax.numpy as jnp
from jax import lax
from jax.experimental import pallas as pl
from jax.experimental.pallas import tpu as pltpu

F32 = jnp.float32
BF16 = jnp.bfloat16
I32 = jnp.int32

D = 1024
N_CTX_SEQ, CTX_LEN = 32, 256
N_LAT_SEQ, LAT_LEN = 4, 4096
PAST = 512
DEPTH = 4
GRID_W = 64
POOL_WINDOWS = (2, 4, 8, 16)
POOL_CH = D // 4
HEADS = 8
QK_NOPE, QK_ROPE, V_DIM = 128, 64, 128
Q_RANK, KV_RANK = 512, 256
AXIS_FREQS = QK_ROPE // 4
ROPE_BASE = 10000.0
SM_SCALE = (QK_NOPE + QK_ROPE) ** -0.5
N_GROUPS, EXP_PER_GROUP = 4, 8
N_EXPERTS = N_GROUPS * EXP_PER_GROUP
D_EXPERT = 512
EPS = 1e-6

T_CTX = N_CTX_SEQ * CTX_LEN
T_LAT = N_LAT_SEQ * LAT_LEN
T = T_CTX + T_LAT
TM = 256
NT = T // TM
NCT = T_CTX // TM
LT = LAT_LEN // TM
LT_SHIFT = LT.bit_length() - 1
HALO = 8
KB = TM + 128
NR = 48
ER0 = 16
TS = 256
CAP = 2 * T + N_EXPERTS * TS
NST = CAP // TS
PAD_SLOT = 1 << 30
NK_LAT = PAST + LAT_LEN
TQ = 256
MIB = 1024 * 1024

_HI = lax.Precision.HIGHEST


def _cparams(n_axes, vmem_mib):
    return pltpu.CompilerParams(
        dimension_semantics=("arbitrary",) * n_axes,
        vmem_limit_bytes=vmem_mib * MIB)


def _rms(x, g):
    return x * lax.rsqrt(jnp.mean(x * x, axis=-1, keepdims=True) + EPS) * g


def _tile_info(i):
    is_ctx = i < NCT
    jl = jnp.maximum(i - NCT, 0)
    j = jnp.where(is_ctx, 0, jl & (LT - 1))
    mrow = jnp.where(is_ctx, 0, 1 + lax.shift_right_logical(jl, LT_SHIFT))
    return is_ctx, j, mrow


def _mod(mod_ref, mrow, k):
    return mod_ref[pl.ds(mrow, 1), k * D:(k + 1) * D]


def _dot_nt(a, b):
    return lax.dot_general(a, b, (((1,), (1,)), ((), ())), preferred_element_type=F32)


def _ffn_pre(x1, i, mrow, mod_ref, nf_ref, wrh_ref, wrl_ref, rb_ref, tri_ref, ones_ref,
             carry_ref, h2u_ref, ri_ref, rw_ref, cnt_ref):
    h2 = _rms(x1, nf_ref[...]) * (1.0 + _mod(mod_ref, mrow, 4)) + _mod(mod_ref, mrow, 3)
    h_hi = h2.astype(BF16)
    h_lo = (h2 - h_hi.astype(F32)).astype(BF16)
    h2u_ref[...] = h2

    lt = (_dot_nt(wrh_ref[...], h_hi) + _dot_nt(wrh_ref[...], h_lo)
          + _dot_nt(wrl_ref[...], h_hi) + rb_ref[...])
    row8 = lax.broadcasted_iota(I32, (8, TM), 0)
    gl = jnp.where(row8 < N_GROUPS, lt[0:8], -jnp.inf)
    ge = jnp.exp(gl - jnp.max(gl, axis=0, keepdims=True))
    gprob = ge / jnp.sum(ge, axis=0, keepdims=True)
    g_p = jnp.max(gprob, axis=0, keepdims=True)
    g_idx = jnp.min(jnp.where(gprob == g_p, row8, 8), axis=0, keepdims=True)
    e_sel = lt[ER0:ER0 + 8]
    for g in range(1, N_GROUPS):
        e_sel = jnp.where(g_idx == g, lt[ER0 + 8 * g:ER0 + 8 * g + 8], e_sel)
    ee = jnp.exp(e_sel - jnp.max(e_sel, axis=0, keepdims=True))
    eprob = ee / jnp.sum(ee, axis=0, keepdims=True)
    p0 = jnp.max(eprob, axis=0, keepdims=True)
    i0 = jnp.min(jnp.where(eprob == p0, row8, 8), axis=0, keepdims=True)
    rest = jnp.where(row8 == i0, -1.0, eprob)
    p1 = jnp.max(rest, axis=0, keepdims=True)
    i1 = jnp.min(jnp.where(rest == p1, row8, 8), axis=0, keepdims=True)
    psum = p0 + p1
    w0 = g_p * p0 / psum
    w1 = g_p * p1 / psum
    id0 = g_idx * EXP_PER_GROUP + i0
    id1 = g_idx * EXP_PER_GROUP + i1

    @pl.when(i == 0)
    def _():
        carry_ref[...] = jnp.zeros_like(carry_ref)

    rowe = lax.broadcasted_iota(I32, (N_EXPERTS, TM), 0)
    oh0 = rowe == id0
    oh1 = rowe == id1
    oh0b = jnp.where(oh0, 1.0, 0.0).astype(BF16)
    oh1b = jnp.where(oh1, 1.0, 0.0).astype(BF16)
    tri = tri_ref[...]
    ones = ones_ref[...]
    carry = carry_ref[...]
    pre0 = jnp.dot(oh0b, tri, preferred_element_type=F32)
    pre1 = jnp.dot(oh1b, tri, preferred_element_type=F32)
    tot0 = jnp.dot(oh0b, ones, preferred_element_type=F32)
    tot1 = jnp.dot(oh1b, ones, preferred_element_type=F32)
    rank0 = jnp.sum(jnp.where(oh0, carry + pre0, 0.0), axis=0, keepdims=True)
    rank1 = jnp.sum(jnp.where(oh1, carry + tot0 + pre1, 0.0), axis=0, keepdims=True)
    carry = carry + tot0 + tot1
    carry_ref[...] = carry
    cnt_ref[...] = carry

    ri_ref[...] = jnp.where(row8 == 0, id0,
                  jnp.where(row8 == 1, id1,
                  jnp.where(row8 == 2, rank0.astype(I32),
                  jnp.where(row8 == 3, rank1.astype(I32), 0))))
    rw_ref[...] = jnp.where(row8 == 0, w0, jnp.where(row8 == 1, w1, 0.0))


def _ffn_in_specs():
    const2 = lambda i: (0, 0)
    return [
        pl.BlockSpec((1, D), const2),
        pl.BlockSpec((NR, D), const2),
        pl.BlockSpec((NR, D), const2),
        pl.BlockSpec((NR, TM), const2),
        pl.BlockSpec((TM, TM), const2),
        pl.BlockSpec((TM, TM), const2),
    ]


def _ffn_out_shapes():
    return [
        jax.ShapeDtypeStruct((T, D), F32),
        jax.ShapeDtypeStruct((T, D), F32),
        jax.ShapeDtypeStruct((NT, 8, TM), I32),
        jax.ShapeDtypeStruct((NT, 8, TM), F32),
        jax.ShapeDtypeStruct((N_EXPERTS, TM), F32),
    ]


def _ffn_out_specs():
    return [
        pl.BlockSpec((TM, D), lambda i: (i, 0)),
        pl.BlockSpec((TM, D), lambda i: (i, 0)),
        pl.BlockSpec((None, 8, TM), lambda i: (i, 0, 0)),
        pl.BlockSpec((None, 8, TM), lambda i: (i, 0, 0)),
        pl.BlockSpec((N_EXPERTS, TM), lambda i: (0, 0)),
    ]


_CARRY = pltpu.VMEM((N_EXPERTS, TM), F32)


def _mod_kernel(cs_ref, w_ref, b_ref, o_ref):
    s = cs_ref[...]
    a = s * jax.nn.sigmoid(s)
    o_ref[...] = jnp.dot(a, w_ref[...], precision=_HI, preferred_element_type=F32) + b_ref[...]


def _modulation(cs, w_ada, b_ada):
    nb = 6
    return pl.pallas_call(
        _mod_kernel,
        out_shape=jax.ShapeDtypeStruct((DEPTH, 8, 6 * D), F32),
        grid=(DEPTH, nb),
        in_specs=[
            pl.BlockSpec((8, D), lambda l, n: (0, 0)),
            pl.BlockSpec((None, D, D), lambda l, n: (l, 0, n)),
            pl.BlockSpec((None, 1, D), lambda l, n: (l, 0, n)),
        ],
        out_specs=pl.BlockSpec((None, 8, D), lambda l, n: (l, 0, n)),
        compiler_params=_cparams(2, 32),
        name="modulation",
    )(cs, w_ada, b_ada.reshape(DEPTH, 1, 6 * D))


def _halo_specs(width):
    nb = T // HALO
    per = TM // HALO
    return [
        pl.BlockSpec((TM, width), lambda i: (i, 0)),
        pl.BlockSpec((HALO, width), lambda i: (jnp.maximum(i * per - 1, 0), 0)),
        pl.BlockSpec((HALO, width), lambda i: (jnp.minimum((i + 1) * per, nb - 1), 0)),
    ]


def _normed_halo(xt, xp, xn, i, mod_ref, nm_ref):
    is_ctx, j, mrow = _tile_info(i)
    g = nm_ref[...]
    sc = 1.0 + _mod(mod_ref, mrow, 1)
    sh = _mod(mod_ref, mrow, 0)
    pv = jnp.where(jnp.logical_and(jnp.logical_not(is_ctx), j > 0), 1.0, 0.0)
    nv = jnp.where(jnp.logical_and(jnp.logical_not(is_ctx), j < LT - 1), 1.0, 0.0)
    ht = _rms(xt, g) * sc + sh
    hp = (_rms(xp, g) * sc + sh) * pv
    hn = (_rms(xn, g) * sc + sh) * nv
    return ht, hp, hn, is_ctx, j, mrow


def _pool_kernel(x_ref, xp_ref, xn_ref, mod_ref, nm_ref, band_ref, wp_ref, ps_ref,
                 nf_ref, wrh_ref, wrl_ref, rb_ref, tri_ref, ones_ref,
                 x1_ref, h2u_ref, ri_ref, rw_ref, cnt_ref, carry_ref):
    i = pl.program_id(0)
    x = x_ref[...]
    ht, hp, hn, is_ctx, j, mrow = _normed_halo(x, xp_ref[...], xn_ref[...], i, mod_ref, nm_ref)
    hext = jnp.concatenate([ht, hp, hn, jnp.zeros((KB - TM - 2 * HALO, D), F32)], axis=0)
    e_hi = hext.astype(BF16)
    e_lo = (hext - e_hi.astype(F32)).astype(BF16)
    seq_len = jnp.where(is_ctx, CTX_LEN, LAT_LEN)
    t = j * TM + lax.broadcasted_iota(I32, (TM, POOL_CH), 0)
    outs = []
    for g, win in enumerate(POOL_WINDOWS):
        lo = win // 2
        hi = win - 1 - lo
        cols = slice(g * POOL_CH, (g + 1) * POOL_CH)
        band = band_ref[g]
        s = (jnp.dot(band, e_hi[:, cols], preferred_element_type=F32)
             + jnp.dot(band, e_lo[:, cols], preferred_element_type=F32))
        cnt = (jnp.minimum(t + hi, seq_len - 1) - jnp.maximum(t - lo, 0) + 1).astype(F32)
        d = s / cnt - ht[:, cols]
        outs.append(jnp.dot(d.astype(BF16), wp_ref[g], preferred_element_type=F32))
    o = jnp.concatenate(outs, axis=1) * ps_ref[...]
    x1 = x + _mod(mod_ref, mrow, 2) * o
    x1_ref[...] = x1
    _ffn_pre(x1, i, mrow, mod_ref, nf_ref, wrh_ref, wrl_ref, rb_ref, tri_ref, ones_ref,
             carry_ref, h2u_ref, ri_ref, rw_ref, cnt_ref)


def _pool_band():
    band = np.zeros((len(POOL_WINDOWS), TM, KB), np.float32)
    r = np.arange(TM)[:, None]
    for g, win in enumerate(POOL_WINDOWS):
        lo = win // 2
        hi = win - 1 - lo
        pos = np.concatenate([np.arange(TM), np.arange(-HALO, 0), np.arange(TM, TM + HALO)])[None, :]
        band[g, :, :TM + 2 * HALO] = (pos >= r - lo) & (pos <= r + hi)
    return jnp.asarray(band, BF16)


def _pool_layer(x, mod_l, nm, wp, ps, ffn_args):
    const2 = lambda i: (0, 0)
    const3 = lambda i: (0, 0, 0)
    return pl.pallas_call(
        _pool_kernel,
        out_shape=_ffn_out_shapes(),
        grid=(NT,),
        in_specs=_halo_specs(D) + [
            pl.BlockSpec((8, 6 * D), const2),
            pl.BlockSpec((1, D), const2),
            pl.BlockSpec((len(POOL_WINDOWS), TM, KB), const3),
            pl.BlockSpec((len(POOL_WINDOWS), POOL_CH, POOL_CH), const3),
            pl.BlockSpec((1, D), const2),
        ] + _ffn_in_specs(),
        out_specs=_ffn_out_specs(),
        scratch_shapes=[_CARRY],
        compiler_params=_cparams(1, 48),
        name="pool_mixer",
    )(x, x, x, mod_l, nm, _pool_band(), wp.astype(BF16), ps, *ffn_args)


def _conv_kernel(x_ref, xp_ref, xn_ref, mod_ref, nm_ref, win_ref, cw_ref, wout_ref,
                 nf_ref, wrh_ref, wrl_ref, rb_ref, tri_ref, ones_ref,
                 x1_ref, h2u_ref, ri_ref, rw_ref, cnt_ref, carry_ref):
    i = pl.program_id(0)
    x = x_ref[...]
    ht, hp, hn, is_ctx, j, mrow = _normed_halo(x, xp_ref[...], xn_ref[...], i, mod_ref, nm_ref)
    hext = jnp.concatenate([ht, hp, hn], axis=0).astype(BF16)
    bcv = jnp.dot(hext, win_ref[...], preferred_element_type=F32)
    b = bcv[:TM, :D]
    u = bcv[:, D:2 * D] * bcv[:, 2 * D:]
    um = u[:TM]
    u_before = u[TM + HALO - 1:TM + HALO]
    u_after = u[TM + HALO:TM + HALO + 1]
    row = lax.broadcasted_iota(I32, (TM, D), 0)
    up = jnp.where(row == 0, u_before, pltpu.roll(um, 1, 0))
    un = jnp.where(row == TM - 1, u_after, pltpu.roll(um, TM - 1, 0))
    cw = cw_ref[...]
    conv = up * cw[0:1] + um * cw[1:2] + un * cw[2:3]
    o = jnp.dot((b * conv).astype(BF16), wout_ref[...], preferred_element_type=F32)
    x1 = x + _mod(mod_ref, mrow, 2) * o
    x1_ref[...] = x1
    _ffn_pre(x1, i, mrow, mod_ref, nf_ref, wrh_ref, wrl_ref, rb_ref, tri_ref, ones_ref,
             carry_ref, h2u_ref, ri_ref, rw_ref, cnt_ref)


def _conv_layer(x, mod_l, nm, w_in, cw, w_out, ffn_args):
    const2 = lambda i: (0, 0)
    cw8 = jnp.concatenate([cw, jnp.zeros((8 - cw.shape[0], D), F32)], axis=0)
    return pl.pallas_call(
        _conv_kernel,
        out_shape=_ffn_out_shapes(),
        grid=(NT,),
        in_specs=_halo_specs(D) + [
            pl.BlockSpec((8, 6 * D), const2),
            pl.BlockSpec((1, D), const2),
            pl.BlockSpec((D, 3 * D), const2),
            pl.BlockSpec((8, D), const2),
            pl.BlockSpec((D, D), const2),
        ] + _ffn_in_specs(),
        out_specs=_ffn_out_specs(),
        scratch_shapes=[_CARRY],
        compiler_params=_cparams(1, 56),
        name="conv_mixer",
    )(x, x, x, mod_l, nm, w_in.astype(BF16), cw8, w_out.astype(BF16), *ffn_args)


W_CAT = Q_RANK + KV_RANK + 2 * QK_ROPE
QH = 2 * QK_NOPE


def _mla_proj_kernel(x_ref, mod_ref, nm_ref, wcat_ref, qn_ref, kvn_ref, wq_ref, wqs_ref,
                     c_ref, s_ref, q_ref, ckv_ref, kpe_ref):
    i = pl.program_id(0)
    _, _, mrow = _tile_info(i)
    h = _rms(x_ref[...], nm_ref[...]) * (1.0 + _mod(mod_ref, mrow, 1)) + _mod(mod_ref, mrow, 0)
    y = jnp.dot(h.astype(BF16), wcat_ref[...], preferred_element_type=F32)
    cqn = _rms(y[:, :Q_RANK], qn_ref[...]).astype(BF16)
    q = jnp.dot(cqn, wq_ref[...], preferred_element_type=F32)
    qs = jnp.dot(cqn, wqs_ref[...], preferred_element_type=F32)
    cos = c_ref[...]
    sin = s_ref[...]
    parts = []
    for hh in range(HEADS):
        parts.append(q[:, hh * QH:hh * QH + QK_NOPE])
        parts.append(q[:, hh * QH + QK_NOPE:(hh + 1) * QH] * cos + qs[:, hh * 128:(hh + 1) * 128] * sin)
    q_ref[...] = (jnp.concatenate(parts, axis=1) * SM_SCALE).astype(BF16)
    ckv_ref[...] = _rms(y[:, Q_RANK:Q_RANK + KV_RANK], kvn_ref[...])
    k2 = y[:, Q_RANK + KV_RANK:]
    kr = k2 * cos + pltpu.roll(k2, QK_ROPE, 1) * sin
    lane = lax.broadcasted_iota(I32, (TM, 2 * QK_ROPE), 1)
    kpe_ref[...] = jnp.where(lane < QK_ROPE, kr, 0.0)


def _rope_tables():
    rows_n = LAT_LEN // GRID_W
    rows = jnp.repeat(jnp.arange(rows_n), GRID_W).astype(F32)
    cols = jnp.tile(jnp.arange(GRID_W), rows_n).astype(F32)
    inv = ROPE_BASE ** (-(jnp.arange(AXIS_FREQS, dtype=F32) / AXIS_FREQS))
    ang = jnp.stack([rows[:, None] * inv, cols[:, None] * inv], axis=1)
    cos, sin = jnp.cos(ang), jnp.sin(ang)
    c64 = jnp.concatenate([cos[:, 0], cos[:, 0], cos[:, 1], cos[:, 1]], axis=1)
    s64 = jnp.concatenate([-sin[:, 0], sin[:, 0], -sin[:, 1], sin[:, 1]], axis=1)
    c = jnp.concatenate([c64, jnp.ones((LAT_LEN, QK_ROPE), F32)], axis=1)
    s = jnp.concatenate([s64, jnp.zeros((LAT_LEN, QK_ROPE), F32)], axis=1)
    c = jnp.concatenate([c, jnp.ones((TM, 2 * QK_ROPE), F32)], axis=0)
    s = jnp.concatenate([s, jnp.zeros((TM, 2 * QK_ROPE), F32)], axis=0)
    return c, s


def _swap_rope_cols(w):
    f = AXIS_FREQS
    return jnp.concatenate([w[..., f:2 * f], w[..., :f], w[..., 3 * f:], w[..., 2 * f:3 * f]], axis=-1)


def _mla_proj(x, mod_l, nm, w_dq, q_norm, w_uq, w_dkv, kv_norm):
    const2 = lambda i: (0, 0)
    w_kpe = w_dkv[:, KV_RANK:]
    wcat = jnp.concatenate([w_dq, w_dkv[:, :KV_RANK], w_kpe, _swap_rope_cols(w_kpe)], axis=1).astype(BF16)
    wq = w_uq.reshape(Q_RANK, HEADS, QK_NOPE + QK_ROPE)
    zpad = jnp.zeros((Q_RANK, HEADS, QK_ROPE), F32)
    wq_a = jnp.concatenate([wq, zpad], axis=2).reshape(Q_RANK, HEADS * QH).astype(BF16)
    wq_s = jnp.concatenate([_swap_rope_cols(wq[:, :, QK_NOPE:]), zpad], axis=2)
    wq_s = wq_s.reshape(Q_RANK, HEADS * 128).astype(BF16)
    cos, sin = _rope_tables()
    tab_idx = lambda i: (jnp.where(i < NCT, LT, jnp.maximum(i - NCT, 0) & (LT - 1)), 0)
    return pl.pallas_call(
        _mla_proj_kernel,
        out_shape=[jax.ShapeDtypeStruct((T, HEADS * QH), BF16),
                   jax.ShapeDtypeStruct((T, KV_RANK), F32),
                   jax.ShapeDtypeStruct((T, 2 * QK_ROPE), F32)],
        grid=(NT,),
        in_specs=[
            pl.BlockSpec((TM, D), lambda i: (i, 0)),
            pl.BlockSpec((8, 6 * D), const2),
            pl.BlockSpec((1, D), const2),
            pl.BlockSpec((D, W_CAT), const2),
            pl.BlockSpec((1, Q_RANK), const2),
            pl.BlockSpec((1, KV_RANK), const2),
            pl.BlockSpec((Q_RANK, HEADS * QH), const2),
            pl.BlockSpec((Q_RANK, HEADS * 128), const2),
            pl.BlockSpec((TM, 2 * QK_ROPE), tab_idx),
            pl.BlockSpec((TM, 2 * QK_ROPE), tab_idx),
        ],
        out_specs=[pl.BlockSpec((TM, HEADS * QH), lambda i: (i, 0)),
                   pl.BlockSpec((TM, KV_RANK), lambda i: (i, 0)),
                   pl.BlockSpec((TM, 2 * QK_ROPE), lambda i: (i, 0))],
        compiler_params=_cparams(1, 48),
        name="mla_proj",
    )(x, mod_l, nm, wcat, q_norm, kv_norm, wq_a, wq_s, cos, sin)


def _kv_expand_kernel(ckv_ref, kpe_ref, wk_ref, wv_ref, k_ref, v_ref):
    c = ckv_ref[...].astype(BF16)
    kn = jnp.dot(c, wk_ref[...], preferred_element_type=F32).astype(BF16)
    kp = kpe_ref[...].astype(BF16)
    parts = []
    for hh in range(HEADS):
        parts.append(kn[:, hh * QK_NOPE:(hh + 1) * QK_NOPE])
        parts.append(kp)
    k_ref[...] = jnp.concatenate(parts, axis=1)
    v_ref[...] = jnp.dot(c, wv_ref[...], preferred_element_type=F32).astype(BF16)


def _kv_expand(ckv, kpe, wk, wv, name):
    n = ckv.shape[0]
    const2 = lambda i: (0, 0)
    return pl.pallas_call(
        _kv_expand_kernel,
        out_shape=[jax.ShapeDtypeStruct((n, HEADS * QH), BF16),
                   jax.ShapeDtypeStruct((n, HEADS * V_DIM), BF16)],
        grid=(n // TM,),
        in_specs=[
            pl.BlockSpec((TM, KV_RANK), lambda i: (i, 0)),
            pl.BlockSpec((TM, 2 * QK_ROPE), lambda i: (i, 0)),
            pl.BlockSpec((KV_RANK, HEADS * QK_NOPE), const2),
            pl.BlockSpec((KV_RANK, HEADS * V_DIM), const2),
        ],
        out_specs=[pl.BlockSpec((TM, HEADS * QH), lambda i: (i, 0)),
                   pl.BlockSpec((TM, HEADS * V_DIM), lambda i: (i, 0))],
        compiler_params=_cparams(1, 32),
        name=name,
    )(ckv, kpe, wk, wv)


def _attend(q, k, v):
    s = _dot_nt(q, k)
    p = jnp.exp(s - jnp.max(s, axis=1, keepdims=True))
    l = jnp.sum(p, axis=1, keepdims=True)
    return jnp.dot(p.astype(BF16), v, preferred_element_type=F32) / l


def _attn_lat_kernel(q_ref, k_ref, v_ref, o_ref):
    o_ref[...] = _attend(q_ref[...], k_ref[...], v_ref[...]).astype(BF16)


def _attn_lat(q, k, v):
    nq = LAT_LEN // TQ
    q0 = T_CTX // TQ
    return pl.pallas_call(
        _attn_lat_kernel,
        out_shape=jax.ShapeDtypeStruct((T_LAT, HEADS * V_DIM), BF16),
        grid=(N_LAT_SEQ, HEADS, nq),
        in_specs=[
            pl.BlockSpec((TQ, QH), lambda b, h, t: (q0 + b * nq + t, h)),
            pl.BlockSpec((NK_LAT, QH), lambda b, h, t: (b, h)),
            pl.BlockSpec((NK_LAT, V_DIM), lambda b, h, t: (b, h)),
        ],
        out_specs=pl.BlockSpec((TQ, V_DIM), lambda b, h, t: (b * nq + t, h)),
        compiler_params=_cparams(3, 48),
        name="attn_latent",
    )(q, k, v)


def _attn_ctx_kernel(q_ref, k_ref, v_ref, o_ref):
    outs = []
    for hh in range(HEADS):
        outs.append(_attend(q_ref[:, hh * QH:(hh + 1) * QH], k_ref[:, hh * QH:(hh + 1) * QH],
                            v_ref[:, hh * V_DIM:(hh + 1) * V_DIM]))
    o_ref[...] = jnp.concatenate(outs, axis=1).astype(BF16)


def _attn_ctx(q, k, v):
    return pl.pallas_call(
        _attn_ctx_kernel,
        out_shape=jax.ShapeDtypeStruct((T_CTX, HEADS * V_DIM), BF16),
        grid=(N_CTX_SEQ,),
        in_specs=[
            pl.BlockSpec((CTX_LEN, HEADS * QH), lambda b: (b, 0)),
            pl.BlockSpec((CTX_LEN, HEADS * QH), lambda b: (b, 0)),
            pl.BlockSpec((CTX_LEN, HEADS * V_DIM), lambda b: (b, 0)),
        ],
        out_specs=pl.BlockSpec((CTX_LEN, HEADS * V_DIM), lambda b: (b, 0)),
        compiler_params=_cparams(1, 32),
        name="attn_context",
    )(q, k, v)


def _attn_out_kernel(x_ref, oc_ref, ol_ref, mod_ref, wo_ref,
                     nf_ref, wrh_ref, wrl_ref, rb_ref, tri_ref, ones_ref,
                     x1_ref, h2u_ref, ri_ref, rw_ref, cnt_ref, carry_ref):
    i = pl.program_id(0)
    is_ctx, _, mrow = _tile_info(i)
    att = jnp.where(is_ctx, oc_ref[...], ol_ref[...])
    o = jnp.dot(att, wo_ref[...], preferred_element_type=F32)
    x1 = x_ref[...] + _mod(mod_ref, mrow, 2) * o
    x1_ref[...] = x1
    _ffn_pre(x1, i, mrow, mod_ref, nf_ref, wrh_ref, wrl_ref, rb_ref, tri_ref, ones_ref,
             carry_ref, h2u_ref, ri_ref, rw_ref, cnt_ref)


def _attn_out(x, o_ctx, o_lat, mod_l, w_o, ffn_args):
    const2 = lambda i: (0, 0)
    return pl.pallas_call(
        _attn_out_kernel,
        out_shape=_ffn_out_shapes(),
        grid=(NT,),
        in_specs=[
            pl.BlockSpec((TM, D), lambda i: (i, 0)),
            pl.BlockSpec((TM, D), lambda i: (jnp.minimum(i, NCT - 1), 0)),
            pl.BlockSpec((TM, D), lambda i: (jnp.maximum(i - NCT, 0), 0)),
            pl.BlockSpec((8, 6 * D), const2),
            pl.BlockSpec((D, D), const2),
        ] + _ffn_in_specs(),
        out_specs=_ffn_out_specs(),
        scratch_shapes=[_CARRY],
        compiler_params=_cparams(1, 48),
        name="attn_out",
    )(x, o_ctx, o_lat, mod_l, w_o.astype(BF16), *ffn_args)


def _mla_layer(x, mod_l, nm, cache_ckv, cache_kpe, w_dq, q_norm, w_uq, w_dkv, kv_norm, w_ukv, w_o, ffn_args):
    q, ckv, kpe = _mla_proj(x, mod_l, nm, w_dq, q_norm, w_uq, w_dkv, kv_norm)
    wkv = w_ukv.reshape(KV_RANK, HEADS, QK_NOPE + V_DIM)
    wk = wkv[:, :, :QK_NOPE].reshape(KV_RANK, HEADS * QK_NOPE).astype(BF16)
    wv = wkv[:, :, QK_NOPE:].reshape(KV_RANK, HEADS * V_DIM).astype(BF16)
    k_c, v_c = _kv_expand(ckv[:T_CTX], kpe[:T_CTX], wk, wv, "kv_expand_context")
    kpe_cache = jnp.concatenate([cache_kpe, jnp.zeros_like(cache_kpe)], axis=-1)
    ckv_l = jnp.concatenate([cache_ckv, ckv[T_CTX:].reshape(N_LAT_SEQ, LAT_LEN, KV_RANK)], axis=1)
    kpe_l = jnp.concatenate([kpe_cache, kpe[T_CTX:].reshape(N_LAT_SEQ, LAT_LEN, 2 * QK_ROPE)], axis=1)
    k_l, v_l = _kv_expand(ckv_l.reshape(N_LAT_SEQ * NK_LAT, KV_RANK),
                          kpe_l.reshape(N_LAT_SEQ * NK_LAT, 2 * QK_ROPE), wk, wv, "kv_expand_latent")
    o_c = _attn_ctx(q, k_c, v_c)
    o_l = _attn_lat(q, k_l, v_l)
    outs = _attn_out(x, o_c, o_l, mod_l, w_o, ffn_args)
    new_ckv = ckv[:T_CTX].reshape(N_CTX_SEQ, 1, CTX_LEN, KV_RANK)
    new_kpe = kpe[:T_CTX, :QK_ROPE].reshape(N_CTX_SEQ, 1, CTX_LEN, QK_ROPE)
    return outs, new_ckv, new_kpe


def _slot_kernel(meta_ref, ri_ref, sv_ref, buf_ref, sem):
    ti = pl.program_id(0)

    @pl.when(ti == 0)
    def _():
        def fill(p, c):
            sv_ref[p] = PAD_SLOT
            return c

        def fill_expert(e, c):
            return lax.fori_loop(meta_ref[1, e], meta_ref[2, e], fill, c)
        lax.fori_loop(0, N_EXPERTS, fill_expert, 0)
        lax.fori_loop(meta_ref[2, N_EXPERTS - 1], CAP, fill, 0)

    cp = pltpu.make_async_copy(ri_ref.at[ti], buf_ref, sem)
    cp.start()
    cp.wait()

    def tok(t, c):
        for k in range(2):
            p = meta_ref[0, buf_ref[k, t]] + buf_ref[2 + k, t]
            sv_ref[p] = 2 * (ti * TM + t) + k
        return c
    lax.fori_loop(0, TM, tok, 0, unroll=8)


def _slot_map(meta, route_i):
    return pl.pallas_call(
        _slot_kernel,
        out_shape=jax.ShapeDtypeStruct((CAP,), I32),
        grid=(NT,),
        in_specs=[pl.BlockSpec(memory_space=pltpu.SMEM), pl.BlockSpec(memory_space=pl.ANY)],
        out_specs=pl.BlockSpec(memory_space=pltpu.SMEM),
        scratch_shapes=[pltpu.SMEM((8, TM), I32), pltpu.SemaphoreType.DMA],
        compiler_params=_cparams(1, 16),
        name="slot_map",
    )(meta, route_i)


def _moe_kernel(blk_ref, nu_ref, sv_ref, h2u_ref, w1_ref, w3_ref, w2_ref, o_ref,
                idx_ref, xbuf, obuf, wb1, wb3, wb2, pe_ref, isem, gsem, ssem):
    i = pl.program_id(0)
    n_used = nu_ref[0]
    slot = i % 2

    def idx_copy(tile):
        return pltpu.make_async_copy(sv_ref.at[tile], idx_ref.at[tile % 3], isem.at[tile % 3])

    def start_gather(tile):
        b = tile % 3
        sl = tile % 2
        for r in range(TS):
            a = idx_ref[b, 0, r]
            src = jnp.where(a >= PAD_SLOT, 0, lax.shift_right_logical(a, 1))
            pltpu.make_async_copy(h2u_ref.at[pl.ds(src, 1)], xbuf.at[sl, pl.ds(r, 1)], gsem.at[sl]).start()

    def wait_gather(sl):
        pltpu.make_async_copy(h2u_ref.at[pl.ds(0, TS)], xbuf.at[sl], gsem.at[sl]).wait()

    def wait_scatter(sl):
        pltpu.make_async_copy(obuf.at[sl], o_ref.at[0, pl.ds(0, TS)], ssem.at[sl]).wait()

    @pl.when(i == 0)
    def _():
        pe_ref[0] = -1
        obuf[0] = jnp.zeros((TS, D), F32)
        fills = [pltpu.make_async_copy(obuf.at[0], o_ref.at[k, pl.ds(T + h * TS, TS)], ssem.at[0])
                 for k in range(2) for h in range(2)]
        for f in fills:
            f.start()
        for f in fills:
            f.wait()
        c0 = idx_copy(0)
        c0.start()
        c0.wait()
        start_gather(0)

        @pl.when(n_used > 1)
        def _():
            idx_copy(1).start()

    @pl.when(i < n_used)
    def _():
        @pl.when(i + 1 < n_used)
        def _():
            idx_copy(i + 1).wait()
            start_gather(i + 1)

        @pl.when(i + 2 < n_used)
        def _():
            idx_copy(i + 2).start()

        e = blk_ref[i]

        @pl.when(e != pe_ref[0])
        def _():
            wb1[...] = w1_ref[...].astype(BF16)
            wb3[...] = w3_ref[...].astype(BF16)
            wb2[...] = w2_ref[...].astype(BF16)
            pe_ref[0] = e

        wait_gather(slot)
        xb = xbuf[slot].astype(BF16)
        a = jnp.dot(xb, wb1[...], preferred_element_type=F32)
        b = jnp.dot(xb, wb3[...], preferred_element_type=F32)
        hm = (a * jax.nn.sigmoid(a) * b).astype(BF16)
        out = jnp.dot(hm, wb2[...], preferred_element_type=F32)

        @pl.when(i >= 2)
        def _():
            wait_scatter(slot)

        obuf[slot] = out
        bi = i % 3
        for r in range(TS):
            a_r = idx_ref[bi, 0, r]
            pad = a_r >= PAD_SLOT
            dk = jnp.where(pad, 0, a_r & 1)
            dt = jnp.where(pad, T + slot * TS + r, lax.shift_right_logical(a_r, 1))
            pltpu.make_async_copy(obuf.at[slot, pl.ds(r, 1)], o_ref.at[dk, pl.ds(dt, 1)], ssem.at[slot]).start()

        @pl.when(i == n_used - 1)
        def _():
            wait_scatter(slot)

            @pl.when(i >= 1)
            def _():
                wait_scatter(1 - slot)


def _moe_experts(l, blk_e, n_used, slot_val, h2u, w1, w3, w2):
    grid_spec = pltpu.PrefetchScalarGridSpec(
        num_scalar_prefetch=2,
        grid=(NST,),
        in_specs=[
            pl.BlockSpec(memory_space=pl.ANY),
            pl.BlockSpec(memory_space=pl.ANY),
            pl.BlockSpec((None, None, D, D_EXPERT), lambda i, blk, nu: (l, blk[i], 0, 0)),
            pl.BlockSpec((None, None, D, D_EXPERT), lambda i, blk, nu: (l, blk[i], 0, 0)),
            pl.BlockSpec((None, None, D_EXPERT, D), lambda i, blk, nu: (l, blk[i], 0, 0)),
        ],
        out_specs=pl.BlockSpec(memory_space=pl.ANY),
        scratch_shapes=[
            pltpu.SMEM((3, 1, TS), I32),
            pltpu.VMEM((2, TS, D), F32),
            pltpu.VMEM((2, TS, D), F32),
            pltpu.VMEM((D, D_EXPERT), BF16),
            pltpu.VMEM((D, D_EXPERT), BF16),
            pltpu.VMEM((D_EXPERT, D), BF16),
            pltpu.SMEM((1,), I32),
            pltpu.SemaphoreType.DMA((3,)),
            pltpu.SemaphoreType.DMA((2,)),
            pltpu.SemaphoreType.DMA((2,)),
        ],
    )
    return pl.pallas_call(
        _moe_kernel,
        out_shape=jax.ShapeDtypeStruct((2, T + 2 * TS, D), F32),
        grid_spec=grid_spec,
        compiler_params=_cparams(1, 48),
        name="moe_experts",
    )(blk_e, n_used, slot_val.reshape(NST, 1, TS), h2u, w1, w3, w2)


def _lane_to_col(row):
    r = lax.broadcasted_iota(I32, (TM, TM), 0)
    c = lax.broadcasted_iota(I32, (TM, TM), 1)
    return jnp.sum(jnp.where(r == c, row, 0.0), axis=1, keepdims=True)


def _combine_kernel(final, x1_ref, o0_ref, o1_ref, rw_ref, mod_ref, nfin_ref, y_ref):
    i = pl.program_id(0)
    _, _, mrow = _tile_info(i)
    rw = rw_ref[...]
    w0 = _lane_to_col(rw[0:1])
    w1 = _lane_to_col(rw[1:2])
    g2 = _mod(mod_ref, mrow, 5)
    x2 = x1_ref[...] + g2 * (o0_ref[...] * w0 + o1_ref[...] * w1)
    if final:
        x2 = _rms(x2, nfin_ref[...])
    y_ref[...] = x2


def _combine(x1, o_rows, route_w, mod_l, norm_final, final):
    const2 = lambda i: (0, 0)
    return pl.pallas_call(
        functools.partial(_combine_kernel, final),
        out_shape=jax.ShapeDtypeStruct((T, D), F32),
        grid=(NT,),
        in_specs=[
            pl.BlockSpec((TM, D), lambda i: (i, 0)),
            pl.BlockSpec((None, TM, D), lambda i: (0, i, 0)),
            pl.BlockSpec((None, TM, D), lambda i: (1, i, 0)),
            pl.BlockSpec((None, 8, TM), lambda i: (i, 0, 0)),
            pl.BlockSpec((8, 6 * D), const2),
            pl.BlockSpec((1, D), const2),
        ],
        out_specs=pl.BlockSpec((TM, D), lambda i: (i, 0)),
        compiler_params=_cparams(1, 32),
        name="moe_combine",
    )(x1, o_rows, o_rows, route_w, mod_l, norm_final)


def _moe_layer(l, x1, h2u, route_i, route_w, counts, mod_l, w1, w3, w2, norm_final, final):
    sizes = counts[:, 0].astype(I32)
    padded = (sizes + TS - 1) // TS * TS
    pad_ends = jnp.cumsum(padded)
    pad_starts = pad_ends - padded
    n_used = (pad_ends[-1:] // TS).astype(I32)
    tile_start = jnp.arange(NST, dtype=I32) * TS
    blk_e = jnp.minimum(jnp.sum((pad_ends[None, :] <= tile_start[:, None]).astype(I32), axis=1),
                        N_EXPERTS - 1).astype(I32)
    meta = jnp.stack([pad_starts, pad_starts + sizes, pad_ends]).astype(I32)
    slot_val = _slot_map(meta, route_i)
    o_rows = _moe_experts(l, blk_e, n_used, slot_val, h2u, w1, w3, w2)
    return _combine(x1, o_rows, route_w, mod_l, norm_final, final)


def _routing_params(w_rg, b_rg, w_re, b_re):
    wt = jnp.zeros((NR, D), F32).at[:N_GROUPS].set(w_rg.T).at[ER0:ER0 + N_EXPERTS].set(w_re.T)
    hi = wt.astype(BF16)
    lo = (wt - hi.astype(F32)).astype(BF16)
    bias = jnp.zeros((NR,), F32).at[:N_GROUPS].set(b_rg).at[ER0:ER0 + N_EXPERTS].set(b_re)
    return hi, lo, jnp.broadcast_to(bias[:, None], (NR, TM))


def kernel(x_prompt, x_sample, cache_ckv, cache_kpe, c, c_ctx, norm_mix, norm_ffn, norm_final, w_ada, b_ada, w_pool, pool_scale, w_conv_in, conv_w, w_conv_out, w_dq, q_norm, w_uq, w_dkv, kv_norm, w_ukv, w_o, w_route_g, b_route_g, w_route_e, b_route_e, w1, w3, w2):
    x = jnp.concatenate([x_prompt.reshape(T_CTX, D), x_sample.reshape(T_LAT, D)], axis=0)
    cs = jnp.concatenate([c_ctx[None, :], c, jnp.zeros((8 - 1 - N_LAT_SEQ, D), F32)], axis=0)
    mod_all = _modulation(cs, w_ada, b_ada)
    tri = jnp.asarray(np.triu(np.ones((TM, TM), np.float32), 1), BF16)
    ones = jnp.ones((TM, TM), BF16)
    nfin = norm_final[None, :]
    new_ckv = new_kpe = None
    for l in range(DEPTH):
        kind, j = l % 3, l // 3
        mod_l = mod_all[l]
        nm = norm_mix[l][None, :]
        wrh, wrl, rb = _routing_params(w_route_g[l], b_route_g[l], w_route_e[l], b_route_e[l])
        ffn_args = (norm_ffn[l][None, :], wrh, wrl, rb, tri, ones)
        if kind == 0:
            outs = _pool_layer(x, mod_l, nm, w_pool[j], pool_scale[j][None, :], ffn_args)
        elif kind == 1:
            outs = _conv_layer(x, mod_l, nm, w_conv_in[j], conv_w[j], w_conv_out[j], ffn_args)
        else:
            outs, new_ckv, new_kpe = _mla_layer(
                x, mod_l, nm, cache_ckv[:, j], cache_kpe[:, j], w_dq[j], q_norm[j][None, :], w_uq[j],
                w_dkv[j], kv_norm[j][None, :], w_ukv[j], w_o[j], ffn_args)
        x1, h2u, route_i, route_w, counts = outs
        x = _moe_layer(l, x1, h2u, route_i, route_w, counts, mod_l, w1, w3, w2, nfin, l == DEPTH - 1)
    y_prompt = x[:T_CTX].reshape(N_CTX_SEQ, CTX_LEN, D)
    y_sample = x[T_CTX:].reshape(N_LAT_SEQ, LAT_LEN, D)
    return (y_prompt, y_sample, new_ckv, new_kpe)
```

```python
import functools

import numpy as np
import jax
import jax.numpy as jnp
from jax import lax
from jax.experimental import pallas as pl
from jax.experimental.pallas import tpu as pltpu

F32 = jnp.float32
BF16 = jnp.bfloat16
I32 = jnp.int32

D = 1024
N_CTX_SEQ, CTX_LEN = 32, 256
N_LAT_SEQ, LAT_LEN = 4, 4096
PAST = 512
DEPTH = 4
GRID_W = 64
POOL_WINDOWS = (2, 4, 8, 16)
POOL_CH = D // 4
HEADS = 8
QK_NOPE, QK_ROPE, V_DIM = 128, 64, 128
Q_RANK, KV_RANK = 512, 256
AXIS_FREQS = QK_ROPE // 4
ROPE_BASE = 10000.0
SM_SCALE = (QK_NOPE + QK_ROPE) ** -0.5
LOG2E = 1.4426950408889634
N_GROUPS, EXP_PER_GROUP = 4, 8
N_EXPERTS = N_GROUPS * EXP_PER_GROUP
D_EXPERT = 512
EPS = 1e-6

T_CTX = N_CTX_SEQ * CTX_LEN
T_LAT = N_LAT_SEQ * LAT_LEN
T = T_CTX + T_LAT
TM = 256
NT = T // TM
NCT = T_CTX // TM
LT = LAT_LEN // TM
LT_SHIFT = LT.bit_length() - 1
HALO = 8
KB = TM + 128
NR = 48
ER0 = 16
TS = 256
CAP = 2 * T + N_EXPERTS * TS
NST = CAP // TS
NS = D // 128
TP = T + 2 * TS
NK_LAT = PAST + LAT_LEN
TQ = 512
MIB = 1024 * 1024

_HI = lax.Precision.HIGHEST


def _cparams(n_axes, vmem_mib):
    return pltpu.CompilerParams(
        dimension_semantics=("arbitrary",) * n_axes,
        vmem_limit_bytes=vmem_mib * MIB)


def _rms(x, g):
    return x * lax.rsqrt(jnp.mean(x * x, axis=-1, keepdims=True) + EPS) * g


def _tile_info(i):
    is_ctx = i < NCT
    jl = jnp.maximum(i - NCT, 0)
    j = jnp.where(is_ctx, 0, jl & (LT - 1))
    mrow = jnp.where(is_ctx, 0, 1 + lax.shift_right_logical(jl, LT_SHIFT))
    return is_ctx, j, mrow


def _mod(mod_ref, mrow, k):
    return mod_ref[pl.ds(mrow, 1), k * D:(k + 1) * D]


def _store_slabs(ref, x):
    for s in range(NS):
        ref[pl.ds(s, x.shape[0], stride=NS), :] = x[:, s * 128:(s + 1) * 128]


def _load_slabs(ref, rows):
    return jnp.concatenate([ref[pl.ds(s, rows, stride=NS), :] for s in range(NS)], axis=1)


def _dot_nt(a, b):
    return lax.dot_general(a, b, (((1,), (1,)), ((), ())), preferred_element_type=F32)


def _ffn_pre(x1, i, mrow, mod_ref, nf_ref, wrh_ref, wrl_ref, rb_ref, tri_ref, ones_ref,
             carry_ref, h2u_ref, ri_ref, rw_ref, cnt_ref):
    h2 = _rms(x1, nf_ref[...]) * (1.0 + _mod(mod_ref, mrow, 4)) + _mod(mod_ref, mrow, 3)
    h_hi = h2.astype(BF16)
    h_lo = (h2 - h_hi.astype(F32)).astype(BF16)
    _store_slabs(h2u_ref, h2)

    lt = (_dot_nt(wrh_ref[...], h_hi) + _dot_nt(wrh_ref[...], h_lo)
          + _dot_nt(wrl_ref[...], h_hi) + rb_ref[...])
    row8 = lax.broadcasted_iota(I32, (8, TM), 0)
    gl = jnp.where(row8 < N_GROUPS, lt[0:8], -jnp.inf)
    ge = jnp.exp(gl - jnp.max(gl, axis=0, keepdims=True))
    gprob = ge / jnp.sum(ge, axis=0, keepdims=True)
    g_p = jnp.max(gprob, axis=0, keepdims=True)
    g_idx = jnp.min(jnp.where(gprob == g_p, row8, 8), axis=0, keepdims=True)
    e_sel = lt[ER0:ER0 + 8]
    for g in range(1, N_GROUPS):
        e_sel = jnp.where(g_idx == g, lt[ER0 + 8 * g:ER0 + 8 * g + 8], e_sel)
    ee = jnp.exp(e_sel - jnp.max(e_sel, axis=0, keepdims=True))
    eprob = ee / jnp.sum(ee, axis=0, keepdims=True)
    p0 = jnp.max(eprob, axis=0, keepdims=True)
    i0 = jnp.min(jnp.where(eprob == p0, row8, 8), axis=0, keepdims=True)
    rest = jnp.where(row8 == i0, -1.0, eprob)
    p1 = jnp.max(rest, axis=0, keepdims=True)
    i1 = jnp.min(jnp.where(rest == p1, row8, 8), axis=0, keepdims=True)
    psum = p0 + p1
    w0 = g_p * p0 / psum
    w1 = g_p * p1 / psum
    id0 = g_idx * EXP_PER_GROUP + i0
    id1 = g_idx * EXP_PER_GROUP + i1

    @pl.when(i == 0)
    def _():
        carry_ref[...] = jnp.zeros_like(carry_ref)

    rowe = lax.broadcasted_iota(I32, (N_EXPERTS, TM), 0)
    oh0 = rowe == id0
    oh1 = rowe == id1
    oh0b = jnp.where(oh0, 1.0, 0.0).astype(BF16)
    oh1b = jnp.where(oh1, 1.0, 0.0).astype(BF16)
    tri = tri_ref[...]
    ones = ones_ref[...]
    carry = carry_ref[...]
    pre0 = jnp.dot(oh0b, tri, preferred_element_type=F32)
    pre1 = jnp.dot(oh1b, tri, preferred_element_type=F32)
    tot0 = jnp.dot(oh0b, ones, preferred_element_type=F32)
    tot1 = jnp.dot(oh1b, ones, preferred_element_type=F32)
    rank0 = jnp.sum(jnp.where(oh0, carry + pre0, 0.0), axis=0, keepdims=True)
    rank1 = jnp.sum(jnp.where(oh1, carry + tot0 + pre1, 0.0), axis=0, keepdims=True)
    carry = carry + tot0 + tot1
    carry_ref[...] = carry
    cnt_ref[...] = carry

    ri_ref[...] = jnp.where(row8 == 0, id0,
                  jnp.where(row8 == 1, id1,
                  jnp.where(row8 == 2, rank0.astype(I32),
                  jnp.where(row8 == 3, rank1.astype(I32), 0))))
    rw_ref[...] = jnp.where(row8 == 0, w0, jnp.where(row8 == 1, w1, 0.0))


def _ffn_in_specs():
    const2 = lambda i: (0, 0)
    return [
        pl.BlockSpec((1, D), const2),
        pl.BlockSpec((NR, D), const2),
        pl.BlockSpec((NR, D), const2),
        pl.BlockSpec((NR, TM), const2),
        pl.BlockSpec((TM, TM), const2),
        pl.BlockSpec((TM, TM), const2),
    ]


def _ffn_out_shapes():
    return [
        jax.ShapeDtypeStruct((T, D), F32),
        jax.ShapeDtypeStruct((T * NS, 128), F32),
        jax.ShapeDtypeStruct((NT, 8, TM), I32),
        jax.ShapeDtypeStruct((NT, 8, TM), F32),
        jax.ShapeDtypeStruct((N_EXPERTS, TM), F32),
    ]


def _ffn_out_specs():
    return [
        pl.BlockSpec((TM, D), lambda i: (i, 0)),
        pl.BlockSpec((TM * NS, 128), lambda i: (i, 0)),
        pl.BlockSpec((None, 8, TM), lambda i: (i, 0, 0)),
        pl.BlockSpec((None, 8, TM), lambda i: (i, 0, 0)),
        pl.BlockSpec((N_EXPERTS, TM), lambda i: (0, 0)),
    ]


_CARRY = pltpu.VMEM((N_EXPERTS, TM), F32)


def _mod_kernel(cs_ref, w_ref, b_ref, o_ref):
    s = cs_ref[...]
    a = s * jax.nn.sigmoid(s)
    o_ref[...] = jnp.dot(a, w_ref[...], precision=_HI, preferred_element_type=F32) + b_ref[...]


def _modulation(cs, w_ada, b_ada):
    nb = 6
    return pl.pallas_call(
        _mod_kernel,
        out_shape=jax.ShapeDtypeStruct((DEPTH, 8, 6 * D), F32),
        grid=(DEPTH, nb),
        in_specs=[
            pl.BlockSpec((8, D), lambda l, n: (0, 0)),
            pl.BlockSpec((None, D, D), lambda l, n: (l, 0, n)),
            pl.BlockSpec((None, 1, D), lambda l, n: (l, 0, n)),
        ],
        out_specs=pl.BlockSpec((None, 8, D), lambda l, n: (l, 0, n)),
        compiler_params=_cparams(2, 32),
        name="modulation",
    )(cs, w_ada, b_ada.reshape(DEPTH, 1, 6 * D))


def _halo_specs(width):
    nb = T // HALO
    per = TM // HALO
    return [
        pl.BlockSpec((TM, width), lambda i: (i, 0)),
        pl.BlockSpec((HALO, width), lambda i: (jnp.maximum(i * per - 1, 0), 0)),
        pl.BlockSpec((HALO, width), lambda i: (jnp.minimum((i + 1) * per, nb - 1), 0)),
    ]


def _normed_halo(xt, xp, xn, i, mod_ref, nm_ref):
    is_ctx, j, mrow = _tile_info(i)
    g = nm_ref[...]
    sc = 1.0 + _mod(mod_ref, mrow, 1)
    sh = _mod(mod_ref, mrow, 0)
    pv = jnp.where(jnp.logical_and(jnp.logical_not(is_ctx), j > 0), 1.0, 0.0)
    nv = jnp.where(jnp.logical_and(jnp.logical_not(is_ctx), j < LT - 1), 1.0, 0.0)
    ht = _rms(xt, g) * sc + sh
    hp = (_rms(xp, g) * sc + sh) * pv
    hn = (_rms(xn, g) * sc + sh) * nv
    return ht, hp, hn, is_ctx, j, mrow


def _pool_kernel(split, *refs):
    if split:
        xc_ref, x_ref, xp_ref, xn_ref = refs[:4]
        refs = refs[4:]
    else:
        x_ref, xp_ref, xn_ref = refs[:3]
        refs = refs[3:]
    (mod_ref, nm_ref, band_ref, wp_ref, ps_ref, nf_ref, wrh_ref, wrl_ref, rb_ref, tri_ref, ones_ref,
     x1_ref, h2u_ref, ri_ref, rw_ref, cnt_ref, carry_ref) = refs
    i = pl.program_id(0)
    x = x_ref[...]
    if split:
        x = jnp.where(i < NCT, xc_ref[...], x)
    ht, hp, hn, is_ctx, j, mrow = _normed_halo(x, xp_ref[...], xn_ref[...], i, mod_ref, nm_ref)
    hext = jnp.concatenate([ht, hp, hn, jnp.zeros((KB - TM - 2 * HALO, D), F32)], axis=0)
    e_hi = hext.astype(BF16)
    e_lo = (hext - e_hi.astype(F32)).astype(BF16)
    seq_len = jnp.where(is_ctx, CTX_LEN, LAT_LEN)
    t = j * TM + lax.broadcasted_iota(I32, (TM, POOL_CH), 0)
    outs = []
    for g, win in enumerate(POOL_WINDOWS):
        lo = win // 2
        hi = win - 1 - lo
        cols = slice(g * POOL_CH, (g + 1) * POOL_CH)
        band = band_ref[g]
        s = (jnp.dot(band, e_hi[:, cols], preferred_element_type=F32)
             + jnp.dot(band, e_lo[:, cols], preferred_element_type=F32))
        cnt = (jnp.minimum(t + hi, seq_len - 1) - jnp.maximum(t - lo, 0) + 1).astype(F32)
        d = s / cnt - ht[:, cols]
        outs.append(jnp.dot(d.astype(BF16), wp_ref[g], preferred_element_type=F32))
    o = jnp.concatenate(outs, axis=1) * ps_ref[...]
    x1 = x + _mod(mod_ref, mrow, 2) * o
    x1_ref[...] = x1
    _ffn_pre(x1, i, mrow, mod_ref, nf_ref, wrh_ref, wrl_ref, rb_ref, tri_ref, ones_ref,
             carry_ref, h2u_ref, ri_ref, rw_ref, cnt_ref)


def _pool_band():
    band = np.zeros((len(POOL_WINDOWS), TM, KB), np.float32)
    r = np.arange(TM)[:, None]
    for g, win in enumerate(POOL_WINDOWS):
        lo = win // 2
        hi = win - 1 - lo
        pos = np.concatenate([np.arange(TM), np.arange(-HALO, 0), np.arange(TM, TM + HALO)])[None, :]
        band[g, :, :TM + 2 * HALO] = (pos >= r - lo) & (pos <= r + hi)
    return jnp.asarray(band, BF16)


def _split_halo_specs():
    nb = T_LAT // HALO
    per = TM // HALO
    lat = lambda i: jnp.maximum(i - NCT, 0)
    return [
        pl.BlockSpec((TM, D), lambda i: (jnp.minimum(i, NCT - 1), 0)),
        pl.BlockSpec((TM, D), lambda i: (lat(i), 0)),
        pl.BlockSpec((HALO, D), lambda i: (jnp.maximum(lat(i) * per - 1, 0), 0)),
        pl.BlockSpec((HALO, D), lambda i: (jnp.minimum((lat(i) + 1) * per, nb - 1), 0)),
    ]


def _pool_layer(xs, mod_l, nm, wp, ps, ffn_args):
    const2 = lambda i: (0, 0)
    const3 = lambda i: (0, 0, 0)
    split = isinstance(xs, tuple)
    x_args = (xs[0], xs[1], xs[1], xs[1]) if split else (xs, xs, xs)
    return pl.pallas_call(
        functools.partial(_pool_kernel, split),
        out_shape=_ffn_out_shapes(),
        grid=(NT,),
        in_specs=(_split_halo_specs() if split else _halo_specs(D)) + [
            pl.BlockSpec((8, 6 * D), const2),
            pl.BlockSpec((1, D), const2),
            pl.BlockSpec((len(POOL_WINDOWS), TM, KB), const3),
            pl.BlockSpec((len(POOL_WINDOWS), POOL_CH, POOL_CH), const3),
            pl.BlockSpec((1, D), const2),
        ] + _ffn_in_specs(),
        out_specs=_ffn_out_specs(),
        scratch_shapes=[_CARRY],
        compiler_params=_cparams(1, 48),
        name="pool_mixer",
    )(*x_args, mod_l, nm, _pool_band(), wp.astype(BF16), ps, *ffn_args)


def _conv_kernel(x_ref, xp_ref, xn_ref, mod_ref, nm_ref, win_ref, cw_ref, wout_ref,
                 nf_ref, wrh_ref, wrl_ref, rb_ref, tri_ref, ones_ref,
                 x1_ref, h2u_ref, ri_ref, rw_ref, cnt_ref, carry_ref):
    i = pl.program_id(0)
    x = x_ref[...]
    ht, hp, hn, is_ctx, j, mrow = _normed_halo(x, xp_ref[...], xn_ref[...], i, mod_ref, nm_ref)
    hext = jnp.concatenate([ht, hp, hn], axis=0).astype(BF16)
    bcv = jnp.dot(hext, win_ref[...], preferred_element_type=F32)
    b = bcv[:TM, :D]
    u = bcv[:, D:2 * D] * bcv[:, 2 * D:]
    um = u[:TM]
    u_before = u[TM + HALO - 1:TM + HALO]
    u_after = u[TM + HALO:TM + HALO + 1]
    row = lax.broadcasted_iota(I32, (TM, D), 0)
    up = jnp.where(row == 0, u_before, pltpu.roll(um, 1, 0))
    un = jnp.where(row == TM - 1, u_after, pltpu.roll(um, TM - 1, 0))
    cw = cw_ref[...]
    conv = up * cw[0:1] + um * cw[1:2] + un * cw[2:3]
    o = jnp.dot((b * conv).astype(BF16), wout_ref[...], preferred_element_type=F32)
    x1 = x + _mod(mod_ref, mrow, 2) * o
    x1_ref[...] = x1
    _ffn_pre(x1, i, mrow, mod_ref, nf_ref, wrh_ref, wrl_ref, rb_ref, tri_ref, ones_ref,
             carry_ref, h2u_ref, ri_ref, rw_ref, cnt_ref)


def _conv_layer(x, mod_l, nm, w_in, cw, w_out, ffn_args):
    const2 = lambda i: (0, 0)
    cw8 = jnp.concatenate([cw, jnp.zeros((8 - cw.shape[0], D), F32)], axis=0)
    return pl.pallas_call(
        _conv_kernel,
        out_shape=_ffn_out_shapes(),
        grid=(NT,),
        in_specs=_halo_specs(D) + [
            pl.BlockSpec((8, 6 * D), const2),
            pl.BlockSpec((1, D), const2),
            pl.BlockSpec((D, 3 * D), const2),
            pl.BlockSpec((8, D), const2),
            pl.BlockSpec((D, D), const2),
        ] + _ffn_in_specs(),
        out_specs=_ffn_out_specs(),
        scratch_shapes=[_CARRY],
        compiler_params=_cparams(1, 56),
        name="conv_mixer",
    )(x, x, x, mod_l, nm, w_in.astype(BF16), cw8, w_out.astype(BF16), *ffn_args)


W_CAT = Q_RANK + KV_RANK + 2 * QK_ROPE
QH = 2 * QK_NOPE


def _mla_proj_kernel(x_ref, mod_ref, nm_ref, wcat_ref, qn_ref, kvn_ref, wq_ref, wqs_ref,
                     c_ref, s_ref, q_ref, ckv_ref, kpe_ref):
    i = pl.program_id(0)
    _, _, mrow = _tile_info(i)
    h = _rms(x_ref[...], nm_ref[...]) * (1.0 + _mod(mod_ref, mrow, 1)) + _mod(mod_ref, mrow, 0)
    y = jnp.dot(h.astype(BF16), wcat_ref[...], preferred_element_type=F32)
    cqn = _rms(y[:, :Q_RANK], qn_ref[...]).astype(BF16)
    q = jnp.dot(cqn, wq_ref[...], preferred_element_type=F32)
    qs = jnp.dot(cqn, wqs_ref[...], preferred_element_type=F32)
    cos = c_ref[...]
    sin = s_ref[...]
    parts = []
    for hh in range(HEADS):
        parts.append(q[:, hh * QH:hh * QH + QK_NOPE])
        parts.append(q[:, hh * QH + QK_NOPE:(hh + 1) * QH] * cos + qs[:, hh * 128:(hh + 1) * 128] * sin)
    q_ref[...] = (jnp.concatenate(parts, axis=1) * (SM_SCALE * LOG2E)).astype(BF16)
    ckv_ref[...] = _rms(y[:, Q_RANK:Q_RANK + KV_RANK], kvn_ref[...])
    k2 = y[:, Q_RANK + KV_RANK:]
    kr = k2 * cos + pltpu.roll(k2, QK_ROPE, 1) * sin
    lane = lax.broadcasted_iota(I32, (TM, 2 * QK_ROPE), 1)
    kpe_ref[...] = jnp.where(lane < QK_ROPE, kr, 0.0)


def _rope_tables():
    rows_n = LAT_LEN // GRID_W
    rows = jnp.repeat(jnp.arange(rows_n), GRID_W).astype(F32)
    cols = jnp.tile(jnp.arange(GRID_W), rows_n).astype(F32)
    inv = ROPE_BASE ** (-(jnp.arange(AXIS_FREQS, dtype=F32) / AXIS_FREQS))
    ang = jnp.stack([rows[:, None] * inv, cols[:, None] * inv], axis=1)
    cos, sin = jnp.cos(ang), jnp.sin(ang)
    c64 = jnp.concatenate([cos[:, 0], cos[:, 0], cos[:, 1], cos[:, 1]], axis=1)
    s64 = jnp.concatenate([-sin[:, 0], sin[:, 0], -sin[:, 1], sin[:, 1]], axis=1)
    c = jnp.concatenate([c64, jnp.ones((LAT_LEN, QK_ROPE), F32)], axis=1)
    s = jnp.concatenate([s64, jnp.zeros((LAT_LEN, QK_ROPE), F32)], axis=1)
    c = jnp.concatenate([c, jnp.ones((TM, 2 * QK_ROPE), F32)], axis=0)
    s = jnp.concatenate([s, jnp.zeros((TM, 2 * QK_ROPE), F32)], axis=0)
    return c, s


def _swap_rope_cols(w):
    f = AXIS_FREQS
    return jnp.concatenate([w[..., f:2 * f], w[..., :f], w[..., 3 * f:], w[..., 2 * f:3 * f]], axis=-1)


def _mla_proj(x, mod_l, nm, w_dq, q_norm, w_uq, w_dkv, kv_norm):
    const2 = lambda i: (0, 0)
    w_kpe = w_dkv[:, KV_RANK:]
    wcat = jnp.concatenate([w_dq, w_dkv[:, :KV_RANK], w_kpe, _swap_rope_cols(w_kpe)], axis=1).astype(BF16)
    wq = w_uq.reshape(Q_RANK, HEADS, QK_NOPE + QK_ROPE)
    zpad = jnp.zeros((Q_RANK, HEADS, QK_ROPE), F32)
    wq_a = jnp.concatenate([wq, zpad], axis=2).reshape(Q_RANK, HEADS * QH).astype(BF16)
    wq_s = jnp.concatenate([_swap_rope_cols(wq[:, :, QK_NOPE:]), zpad], axis=2)
    wq_s = wq_s.reshape(Q_RANK, HEADS * 128).astype(BF16)
    cos, sin = _rope_tables()
    tab_idx = lambda i: (jnp.where(i < NCT, LT, jnp.maximum(i - NCT, 0) & (LT - 1)), 0)
    return pl.pallas_call(
        _mla_proj_kernel,
        out_shape=[jax.ShapeDtypeStruct((T, HEADS * QH), BF16),
                   jax.ShapeDtypeStruct((T, KV_RANK), F32),
                   jax.ShapeDtypeStruct((T, 2 * QK_ROPE), F32)],
        grid=(NT,),
        in_specs=[
            pl.BlockSpec((TM, D), lambda i: (i, 0)),
            pl.BlockSpec((8, 6 * D), const2),
            pl.BlockSpec((1, D), const2),
            pl.BlockSpec((D, W_CAT), const2),
            pl.BlockSpec((1, Q_RANK), const2),
            pl.BlockSpec((1, KV_RANK), const2),
            pl.BlockSpec((Q_RANK, HEADS * QH), const2),
            pl.BlockSpec((Q_RANK, HEADS * 128), const2),
            pl.BlockSpec((TM, 2 * QK_ROPE), tab_idx),
            pl.BlockSpec((TM, 2 * QK_ROPE), tab_idx),
        ],
        out_specs=[pl.BlockSpec((TM, HEADS * QH), lambda i: (i, 0)),
                   pl.BlockSpec((TM, KV_RANK), lambda i: (i, 0)),
                   pl.BlockSpec((TM, 2 * QK_ROPE), lambda i: (i, 0))],
        compiler_params=_cparams(1, 48),
        name="mla_proj",
    )(x, mod_l, nm, wcat, q_norm, kv_norm, wq_a, wq_s, cos, sin)


def _kv_expand_kernel(ckv_ref, kpe_ref, wk_ref, wv_ref, k_ref, v_ref):
    c = ckv_ref[...].astype(BF16)
    kn = jnp.dot(c, wk_ref[...], preferred_element_type=F32).astype(BF16)
    kp = kpe_ref[...].astype(BF16)
    parts = []
    for hh in range(HEADS):
        parts.append(kn[:, hh * QK_NOPE:(hh + 1) * QK_NOPE])
        parts.append(kp)
    k_ref[...] = jnp.concatenate(parts, axis=1)
    v_ref[...] = jnp.dot(c, wv_ref[...], preferred_element_type=F32).astype(BF16)


def _kv_expand(ckv, kpe, wk, wv, name):
    n = ckv.shape[0]
    const2 = lambda i: (0, 0)
    return pl.pallas_call(
        _kv_expand_kernel,
        out_shape=[jax.ShapeDtypeStruct((n, HEADS * QH), BF16),
                   jax.ShapeDtypeStruct((n, HEADS * V_DIM), BF16)],
        grid=(n // TM,),
        in_specs=[
            pl.BlockSpec((TM, KV_RANK), lambda i: (i, 0)),
            pl.BlockSpec((TM, 2 * QK_ROPE), lambda i: (i, 0)),
            pl.BlockSpec((KV_RANK, HEADS * QK_NOPE), const2),
            pl.BlockSpec((KV_RANK, HEADS * V_DIM), const2),
        ],
        out_specs=[pl.BlockSpec((TM, HEADS * QH), lambda i: (i, 0)),
                   pl.BlockSpec((TM, HEADS * V_DIM), lambda i: (i, 0))],
        compiler_params=_cparams(1, 32),
        name=name,
    )(ckv, kpe, wk, wv)


def _attend(q, k, v):
    s = _dot_nt(q, k)
    p = jnp.exp2(s - jnp.max(s, axis=1, keepdims=True))
    l = jnp.sum(p, axis=1, keepdims=True)
    return jnp.dot(p.astype(BF16), v, preferred_element_type=F32) / l


KC = 512
NKC = NK_LAT // KC


def _attn_lat_kernel(q_ref, k_ref, v_ref, o_ref, s_ref, m_ref, l_ref, acc_ref):
    q = q_ref[...]
    m_ref[...] = jnp.full((TQ, 128), -jnp.inf, F32)

    def scores(c, carry):
        k = k_ref[pl.ds(pl.multiple_of(c * KC, KC), KC), :]
        s = _dot_nt(q, k)
        s_ref[c] = s
        m = m_ref[...]
        for j in range(KC // 128):
            m = jnp.maximum(m, s[:, j * 128:(j + 1) * 128])
        m_ref[...] = m
        return carry
    lax.fori_loop(0, NKC, scores, 0, unroll=True)

    mb = jnp.broadcast_to(jnp.max(m_ref[...], axis=1, keepdims=True), (TQ, 128))
    l_ref[...] = jnp.zeros((TQ, 128), F32)
    acc_ref[...] = jnp.zeros((TQ, V_DIM), F32)

    def weighted(c, carry):
        s = s_ref[c]
        ps = [jnp.exp2(s[:, j * 128:(j + 1) * 128] - mb) for j in range(KC // 128)]
        l = l_ref[...]
        for pj in ps:
            l = l + pj
        l_ref[...] = l
        p = jnp.concatenate(ps, axis=1).astype(BF16)
        v = v_ref[pl.ds(pl.multiple_of(c * KC, KC), KC), :]
        acc_ref[...] += jnp.dot(p, v, preferred_element_type=F32)
        return carry
    lax.fori_loop(0, NKC, weighted, 0, unroll=3)

    o_ref[...] = (acc_ref[...] / jnp.sum(l_ref[...], axis=1, keepdims=True)).astype(BF16)


def _attn_lat(q, k, v):
    nq = LAT_LEN // TQ
    q0 = T_CTX // TQ
    return pl.pallas_call(
        _attn_lat_kernel,
        scratch_shapes=[pltpu.VMEM((NKC, TQ, KC), F32), pltpu.VMEM((TQ, 128), F32),
                        pltpu.VMEM((TQ, 128), F32), pltpu.VMEM((TQ, V_DIM), F32)],
        out_shape=jax.ShapeDtypeStruct((T_LAT, HEADS * V_DIM), BF16),
        grid=(N_LAT_SEQ, HEADS, nq),
        in_specs=[
            pl.BlockSpec((TQ, QH), lambda b, h, t: (q0 + b * nq + t, h)),
            pl.BlockSpec((NK_LAT, QH), lambda b, h, t: (b, h)),
            pl.BlockSpec((NK_LAT, V_DIM), lambda b, h, t: (b, h)),
        ],
        out_specs=pl.BlockSpec((TQ, V_DIM), lambda b, h, t: (b * nq + t, h)),
        compiler_params=_cparams(3, 48),
        name="attn_latent",
    )(q, k, v)


def _attn_ctx_kernel(q_ref, k_ref, v_ref, o_ref):
    outs = []
    for hh in range(HEADS):
        outs.append(_attend(q_ref[:, hh * QH:(hh + 1) * QH], k_ref[:, hh * QH:(hh + 1) * QH],
                            v_ref[:, hh * V_DIM:(hh + 1) * V_DIM]))
    o_ref[...] = jnp.concatenate(outs, axis=1).astype(BF16)


def _attn_ctx(q, k, v):
    return pl.pallas_call(
        _attn_ctx_kernel,
        out_shape=jax.ShapeDtypeStruct((T_CTX, HEADS * V_DIM), BF16),
        grid=(N_CTX_SEQ,),
        in_specs=[
            pl.BlockSpec((CTX_LEN, HEADS * QH), lambda b: (b, 0)),
            pl.BlockSpec((CTX_LEN, HEADS * QH), lambda b: (b, 0)),
            pl.BlockSpec((CTX_LEN, HEADS * V_DIM), lambda b: (b, 0)),
        ],
        out_specs=pl.BlockSpec((CTX_LEN, HEADS * V_DIM), lambda b: (b, 0)),
        compiler_params=_cparams(1, 32),
        name="attn_context",
    )(q, k, v)


def _attn_out_kernel(x_ref, oc_ref, ol_ref, mod_ref, wo_ref,
                     nf_ref, wrh_ref, wrl_ref, rb_ref, tri_ref, ones_ref,
                     x1_ref, h2u_ref, ri_ref, rw_ref, cnt_ref, carry_ref):
    i = pl.program_id(0)
    is_ctx, _, mrow = _tile_info(i)
    att = jnp.where(is_ctx, oc_ref[...], ol_ref[...])
    o = jnp.dot(att, wo_ref[...], preferred_element_type=F32)
    x1 = x_ref[...] + _mod(mod_ref, mrow, 2) * o
    x1_ref[...] = x1
    _ffn_pre(x1, i, mrow, mod_ref, nf_ref, wrh_ref, wrl_ref, rb_ref, tri_ref, ones_ref,
             carry_ref, h2u_ref, ri_ref, rw_ref, cnt_ref)


def _attn_out(x, o_ctx, o_lat, mod_l, w_o, ffn_args):
    const2 = lambda i: (0, 0)
    return pl.pallas_call(
        _attn_out_kernel,
        out_shape=_ffn_out_shapes(),
        grid=(NT,),
        in_specs=[
            pl.BlockSpec((TM, D), lambda i: (i, 0)),
            pl.BlockSpec((TM, D), lambda i: (jnp.minimum(i, NCT - 1), 0)),
            pl.BlockSpec((TM, D), lambda i: (jnp.maximum(i - NCT, 0), 0)),
            pl.BlockSpec((8, 6 * D), const2),
            pl.BlockSpec((D, D), const2),
        ] + _ffn_in_specs(),
        out_specs=_ffn_out_specs(),
        scratch_shapes=[_CARRY],
        compiler_params=_cparams(1, 48),
        name="attn_out",
    )(x, o_ctx, o_lat, mod_l, w_o.astype(BF16), *ffn_args)


def _mla_layer(x, mod_l, nm, cache_ckv, cache_kpe, w_dq, q_norm, w_uq, w_dkv, kv_norm, w_ukv, w_o, ffn_args):
    q, ckv, kpe = _mla_proj(x, mod_l, nm, w_dq, q_norm, w_uq, w_dkv, kv_norm)
    wkv = w_ukv.reshape(KV_RANK, HEADS, QK_NOPE + V_DIM)
    wk = wkv[:, :, :QK_NOPE].reshape(KV_RANK, HEADS * QK_NOPE).astype(BF16)
    wv = wkv[:, :, QK_NOPE:].reshape(KV_RANK, HEADS * V_DIM).astype(BF16)
    k_c, v_c = _kv_expand(ckv[:T_CTX], kpe[:T_CTX], wk, wv, "kv_expand_context")
    kpe_cache = jnp.concatenate([cache_kpe, jnp.zeros_like(cache_kpe)], axis=-1)
    ckv_l = jnp.concatenate([cache_ckv, ckv[T_CTX:].reshape(N_LAT_SEQ, LAT_LEN, KV_RANK)], axis=1)
    kpe_l = jnp.concatenate([kpe_cache, kpe[T_CTX:].reshape(N_LAT_SEQ, LAT_LEN, 2 * QK_ROPE)], axis=1)
    k_l, v_l = _kv_expand(ckv_l.reshape(N_LAT_SEQ * NK_LAT, KV_RANK),
                          kpe_l.reshape(N_LAT_SEQ * NK_LAT, 2 * QK_ROPE), wk, wv, "kv_expand_latent")
    o_c = _attn_ctx(q, k_c, v_c)
    o_l = _attn_lat(q, k_l, v_l)
    outs = _attn_out(x, o_c, o_l, mod_l, w_o, ffn_args)
    new_ckv = ckv[:T_CTX].reshape(N_CTX_SEQ, 1, CTX_LEN, KV_RANK)
    new_kpe = kpe[:T_CTX, :QK_ROPE].reshape(N_CTX_SEQ, 1, CTX_LEN, QK_ROPE)
    return outs, new_ckv, new_kpe


SRC_BITS = 16
assert T <= 1 << SRC_BITS and 2 * TP <= 1 << (32 - SRC_BITS)
ROW_STEP = (1 << SRC_BITS) + 1


def _slot_kernel(meta_ref, psv_ref, ri_ref, sv_ref, stage, buf, sem):
    i = pl.program_id(0)

    @pl.when(i == 0)
    def _():
        def fill(p, c):
            tile_parity = lax.shift_right_logical(p, jnp.int32(TS.bit_length() - 1)) & 1
            sv_ref[p] = (T + tile_parity * TS + (p & (TS - 1))) * (1 << SRC_BITS)
            return c

        def fill_expert(e, c):
            return lax.fori_loop(meta_ref[1, e], meta_ref[2, e], fill, c)
        lax.fori_loop(0, N_EXPERTS, fill_expert, 0)
        lax.fori_loop(meta_ref[2, N_EXPERTS - 1], CAP, fill, 0)

    @pl.when(i < NT)
    def _():
        ri = ri_ref[...]
        rowe = lax.broadcasted_iota(I32, (N_EXPERTS, TM), 0)
        ps = psv_ref[...]
        pos0 = jnp.sum(jnp.where(rowe == ri[0:1], ps, 0), axis=0, keepdims=True) + ri[2:3]
        pos1 = jnp.sum(jnp.where(rowe == ri[1:2], ps, 0), axis=0, keepdims=True) + ri[3:4]
        row8 = lax.broadcasted_iota(I32, (8, TM), 0)
        stage[i % 2] = jnp.where(row8 == 0, pos0, jnp.where(row8 == 1, pos1, 0))
        pltpu.make_async_copy(stage.at[i % 2], buf.at[i % 2], sem.at[i % 2]).start()

    @pl.when(i >= 1)
    def _():
        j = i - 1
        b = j % 2
        pltpu.make_async_copy(stage.at[b], buf.at[b], sem.at[b]).wait()
        base = j * TM * ROW_STEP

        def tok(t, c):
            v = base + t * ROW_STEP
            sv_ref[buf[b, 0, t]] = v
            sv_ref[buf[b, 1, t]] = v + (TP << SRC_BITS)
            return c
        lax.fori_loop(0, TM, tok, 0, unroll=8)


def _slot_map(meta, pad_starts, route_i):
    return pl.pallas_call(
        _slot_kernel,
        out_shape=jax.ShapeDtypeStruct((CAP,), I32),
        grid=(NT + 1,),
        in_specs=[pl.BlockSpec(memory_space=pltpu.SMEM),
                  pl.BlockSpec((N_EXPERTS, TM), lambda i: (0, 0)),
                  pl.BlockSpec((None, 8, TM), lambda i: (jnp.minimum(i, NT - 1), 0, 0))],
        out_specs=pl.BlockSpec(memory_space=pltpu.SMEM),
        scratch_shapes=[pltpu.VMEM((2, 8, TM), I32), pltpu.SMEM((2, 8, TM), I32),
                        pltpu.SemaphoreType.DMA((2,))],
        compiler_params=_cparams(1, 16),
        name="slot_map",
    )(meta, jnp.broadcast_to(pad_starts[:, None], (N_EXPERTS, TM)), route_i)


def _moe_kernel(blk_ref, nu_ref, sv_ref, h2u_ref, w1_ref, w3_ref, w2_ref, o_ref,
                idx_ref, xbuf, obuf, wb1, wb3, wb2, pe_ref, isem, gsem, ssem):
    i = pl.program_id(0)
    n_used = nu_ref[0]
    slot = i % 2

    def idx_copy(tile):
        return pltpu.make_async_copy(sv_ref.at[tile], idx_ref.at[tile % 3], isem.at[tile % 3])

    def start_gather(tile):
        b = tile % 3
        sl = tile % 2
        for r in range(TS):
            src = idx_ref[b, 0, r] & ((1 << SRC_BITS) - 1)
            pltpu.make_async_copy(h2u_ref.at[pl.ds(pl.multiple_of(src * NS, NS), NS)],
                                  xbuf.at[sl, pl.ds(r * NS, NS)], gsem.at[sl]).start()

    def wait_gather(sl):
        pltpu.make_async_copy(h2u_ref.at[pl.ds(0, TS * NS)], xbuf.at[sl], gsem.at[sl]).wait()

    def wait_scatter(sl):
        pltpu.make_async_copy(obuf.at[sl], o_ref.at[pl.ds(0, TS * NS)], ssem.at[sl]).wait()

    @pl.when(i == 0)
    def _():
        pe_ref[0] = -1
        obuf[0] = jnp.zeros((TS * NS, 128), F32)
        fills = [pltpu.make_async_copy(obuf.at[0], o_ref.at[pl.ds((k * TP + T + h * TS) * NS, TS * NS)], ssem.at[0])
                 for k in range(2) for h in range(2)]
        for f in fills:
            f.start()
        for f in fills:
            f.wait()
        c0 = idx_copy(0)
        c0.start()
        c0.wait()
        start_gather(0)

        @pl.when(n_used > 1)
        def _():
            idx_copy(1).start()

    @pl.when(i < n_used)
    def _():
        @pl.when(i + 1 < n_used)
        def _():
            idx_copy(i + 1).wait()
            start_gather(i + 1)

        @pl.when(i + 2 < n_used)
        def _():
            idx_copy(i + 2).start()

        e = blk_ref[i]

        @pl.when(e != pe_ref[0])
        def _():
            wb1[...] = w1_ref[...].astype(BF16)
            wb3[...] = w3_ref[...].astype(BF16)
            wb2[...] = w2_ref[...].astype(BF16)
            pe_ref[0] = e

        wait_gather(slot)
        xb = _load_slabs(xbuf.at[slot], TS).astype(BF16)
        a = jnp.dot(xb, wb1[...], preferred_element_type=F32)
        b = jnp.dot(xb, wb3[...], preferred_element_type=F32)
        hm = (a * jax.nn.sigmoid(a) * b).astype(BF16)
        out = jnp.dot(hm, wb2[...], preferred_element_type=F32)

        @pl.when(i >= 2)
        def _():
            wait_scatter(slot)

        _store_slabs(obuf.at[slot], out)
        bi = i % 3
        for r in range(TS):
            dst = lax.shift_right_logical(idx_ref[bi, 0, r], jnp.int32(SRC_BITS))
            pltpu.make_async_copy(obuf.at[slot, pl.ds(r * NS, NS)],
                                  o_ref.at[pl.ds(pl.multiple_of(dst * NS, NS), NS)], ssem.at[slot]).start()

        @pl.when(i == n_used - 1)
        def _():
            wait_scatter(slot)

            @pl.when(i >= 1)
            def _():
                wait_scatter(1 - slot)


def _moe_experts(l, blk_e, n_used, slot_val, h2u, w1, w3, w2):
    grid_spec = pltpu.PrefetchScalarGridSpec(
        num_scalar_prefetch=2,
        grid=(NST,),
        in_specs=[
            pl.BlockSpec(memory_space=pl.ANY),
            pl.BlockSpec(memory_space=pl.ANY),
            pl.BlockSpec((None, None, D, D_EXPERT), lambda i, blk, nu: (l, blk[i], 0, 0)),
            pl.BlockSpec((None, None, D, D_EXPERT), lambda i, blk, nu: (l, blk[i], 0, 0)),
            pl.BlockSpec((None, None, D_EXPERT, D), lambda i, blk, nu: (l, blk[i], 0, 0)),
        ],
        out_specs=pl.BlockSpec(memory_space=pl.ANY),
        scratch_shapes=[
            pltpu.SMEM((3, 1, TS), I32),
            pltpu.VMEM((2, TS * NS, 128), F32),
            pltpu.VMEM((2, TS * NS, 128), F32),
            pltpu.VMEM((D, D_EXPERT), BF16),
            pltpu.VMEM((D, D_EXPERT), BF16),
            pltpu.VMEM((D_EXPERT, D), BF16),
            pltpu.SMEM((1,), I32),
            pltpu.SemaphoreType.DMA((3,)),
            pltpu.SemaphoreType.DMA((2,)),
            pltpu.SemaphoreType.DMA((2,)),
        ],
    )
    return pl.pallas_call(
        _moe_kernel,
        out_shape=jax.ShapeDtypeStruct((2 * TP * NS, 128), F32),
        grid_spec=grid_spec,
        compiler_params=_cparams(1, 48),
        name="moe_experts",
    )(blk_e, n_used, slot_val.reshape(NST, 1, TS), h2u, w1, w3, w2)


def _lane_to_col(row):
    r = lax.broadcasted_iota(I32, (TM, TM), 0)
    c = lax.broadcasted_iota(I32, (TM, TM), 1)
    return jnp.sum(jnp.where(r == c, row, 0.0), axis=1, keepdims=True)


def _combine_kernel(final, x1_ref, o0_ref, o1_ref, rw_ref, mod_ref, nfin_ref, *y_refs):
    i = pl.program_id(0)
    _, _, mrow = _tile_info(i)
    rw = rw_ref[...]
    w0 = _lane_to_col(rw[0:1])
    w1 = _lane_to_col(rw[1:2])
    g2 = _mod(mod_ref, mrow, 5)
    x2 = x1_ref[...] + g2 * (_load_slabs(o0_ref, TM) * w0 + _load_slabs(o1_ref, TM) * w1)
    if final:
        yc_ref, yl_ref = y_refs
        y = _rms(x2, nfin_ref[...])

        @pl.when(i < NCT)
        def _():
            yc_ref[...] = y

        @pl.when(i >= NCT)
        def _():
            yl_ref[...] = y
    else:
        y_refs[0][...] = x2


def _combine(x1, o_rows, route_w, mod_l, norm_final, final):
    const2 = lambda i: (0, 0)
    if final:
        out_shape = [jax.ShapeDtypeStruct((T_CTX, D), F32), jax.ShapeDtypeStruct((T_LAT, D), F32)]
        out_specs = [pl.BlockSpec((TM, D), lambda i: (jnp.minimum(i, NCT - 1), 0)),
                     pl.BlockSpec((TM, D), lambda i: (jnp.maximum(i - NCT, 0), 0))]
    else:
        out_shape = [jax.ShapeDtypeStruct((T, D), F32)]
        out_specs = [pl.BlockSpec((TM, D), lambda i: (i, 0))]
    return pl.pallas_call(
        functools.partial(_combine_kernel, final),
        out_shape=out_shape,
        grid=(NT,),
        in_specs=[
            pl.BlockSpec((TM, D), lambda i: (i, 0)),
            pl.BlockSpec((TM * NS, 128), lambda i: (i, 0)),
            pl.BlockSpec((TM * NS, 128), lambda i: (TP // TM + i, 0)),
            pl.BlockSpec((None, 8, TM), lambda i: (i, 0, 0)),
            pl.BlockSpec((8, 6 * D), const2),
            pl.BlockSpec((1, D), const2),
        ],
        out_specs=out_specs,
        compiler_params=_cparams(1, 32),
        name="moe_combine",
    )(x1, o_rows, o_rows, route_w, mod_l, norm_final)


def _moe_layer(l, x1, h2u, route_i, route_w, counts, mod_l, w1, w3, w2, norm_final, final):
    sizes = counts[:, 0].astype(I32)
    padded = (sizes + TS - 1) // TS * TS
    pad_ends = jnp.cumsum(padded)
    pad_starts = pad_ends - padded
    n_used = (pad_ends[-1:] // TS).astype(I32)
    tile_start = jnp.arange(NST, dtype=I32) * TS
    blk_e = jnp.minimum(jnp.sum((pad_ends[None, :] <= tile_start[:, None]).astype(I32), axis=1),
                        N_EXPERTS - 1).astype(I32)
    meta = jnp.stack([pad_starts, pad_starts + sizes, pad_ends]).astype(I32)
    slot_val = _slot_map(meta, pad_starts.astype(I32), route_i)
    o_rows = _moe_experts(l, blk_e, n_used, slot_val, h2u, w1, w3, w2)
    return _combine(x1, o_rows, route_w, mod_l, norm_final, final)


def _routing_params(w_rg, b_rg, w_re, b_re):
    wt = jnp.zeros((NR, D), F32).at[:N_GROUPS].set(w_rg.T).at[ER0:ER0 + N_EXPERTS].set(w_re.T)
    hi = wt.astype(BF16)
    lo = (wt - hi.astype(F32)).astype(BF16)
    bias = jnp.zeros((NR,), F32).at[:N_GROUPS].set(b_rg).at[ER0:ER0 + N_EXPERTS].set(b_re)
    return hi, lo, jnp.broadcast_to(bias[:, None], (NR, TM))


def kernel(x_prompt, x_sample, cache_ckv, cache_kpe, c, c_ctx, norm_mix, norm_ffn, norm_final, w_ada, b_ada, w_pool, pool_scale, w_conv_in, conv_w, w_conv_out, w_dq, q_norm, w_uq, w_dkv, kv_norm, w_ukv, w_o, w_route_g, b_route_g, w_route_e, b_route_e, w1, w3, w2):
    x = (x_prompt.reshape(T_CTX, D), x_sample.reshape(T_LAT, D))
    cs =jnp.concatenate([c_ctx[None, :], c, jnp.zeros((8 - 1 - N_LAT_SEQ, D), F32)], axis=0)
    mod_all = _modulation(cs, w_ada, b_ada)
    tri = jnp.asarray(np.triu(np.ones((TM, TM), np.float32), 1), BF16)
    ones = jnp.ones((TM, TM), BF16)
    nfin = norm_final[None, :]
    new_ckv = new_kpe = None
    for l in range(DEPTH):
        kind, j = l % 3, l // 3
        mod_l = mod_all[l]
        nm = norm_mix[l][None, :]
        wrh, wrl, rb = _routing_params(w_route_g[l], b_route_g[l], w_route_e[l], b_route_e[l])
        ffn_args = (norm_ffn[l][None, :], wrh, wrl, rb, tri, ones)
        if kind == 0:
            outs = _pool_layer(x, mod_l, nm, w_pool[j], pool_scale[j][None, :], ffn_args)
        elif kind == 1:
            outs = _conv_layer(x, mod_l, nm, w_conv_in[j], conv_w[j], w_conv_out[j], ffn_args)
        else:
            outs, new_ckv, new_kpe = _mla_layer(
                x, mod_l, nm, cache_ckv[:, j], cache_kpe[:, j], w_dq[j], q_norm[j][None, :], w_uq[j],
                w_dkv[j], kv_norm[j][None, :], w_ukv[j], w_o[j], ffn_args)
        x1, h2u, route_i, route_w, counts = outs
        ys = _moe_layer(l, x1, h2u, route_i, route_w, counts, mod_l, w1, w3, w2, nfin, l == DEPTH - 1)
        x = ys[0]
    y_prompt = ys[0].reshape(N_CTX_SEQ, CTX_LEN, D)
    y_sample = ys[1].reshape(N_LAT_SEQ, LAT_LEN, D)
    return (y_prompt, y_sample, new_ckv, new_kpe)
```

```python
import functools

import numpy as np
import jax
import jax.numpy as jnp
from jax import lax
from jax.experimental import pallas as pl
from jax.experimental.pallas import tpu as pltpu

F32 = jnp.float32
BF16 = jnp.bfloat16
I32 = jnp.int32

D = 1024
N_CTX_SEQ, CTX_LEN = 32, 256
N_LAT_SEQ, LAT_LEN = 4, 4096
PAST = 512
DEPTH = 4
GRID_W = 64
POOL_WINDOWS = (2, 4, 8, 16)
POOL_CH = D // 4
HEADS = 8
QK_NOPE, QK_ROPE, V_DIM = 128, 64, 128
Q_RANK, KV_RANK = 512, 256
AXIS_FREQS = QK_ROPE // 4
ROPE_BASE = 10000.0
SM_SCALE = (QK_NOPE + QK_ROPE) ** -0.5
LOG2E = 1.4426950408889634
N_GROUPS, EXP_PER_GROUP = 4, 8
N_EXPERTS = N_GROUPS * EXP_PER_GROUP
D_EXPERT = 512
EPS = 1e-6

T_CTX = N_CTX_SEQ * CTX_LEN
T_LAT = N_LAT_SEQ * LAT_LEN
T = T_CTX + T_LAT
TM = 256
NT = T // TM
NCT = T_CTX // TM
LT = LAT_LEN // TM
LT_SHIFT = LT.bit_length() - 1
HALO = 8
KB = TM + 128
NRL = 128
ER0 = 16
RG = 4
LC = TM // 128
assert 2 * LC <= 8 and NT % RG == 0
TS = 256
CAP = 2 * T + N_EXPERTS * TS
NST = CAP // TS
NS = D // 128
TP = T + 2 * TS
NK_LAT = PAST + LAT_LEN
TQ = 512
MIB = 1024 * 1024

_HI = lax.Precision.HIGHEST


def _cparams(n_axes, vmem_mib):
    return pltpu.CompilerParams(
        dimension_semantics=("arbitrary",) * n_axes,
        vmem_limit_bytes=vmem_mib * MIB)


def _rms(x, g):
    return x * lax.rsqrt(jnp.mean(x * x, axis=-1, keepdims=True) + EPS) * g


def _tile_info(i):
    is_ctx = i < NCT
    jl = jnp.maximum(i - NCT, 0)
    j = jnp.where(is_ctx, 0, jl & (LT - 1))
    mrow = jnp.where(is_ctx, 0, 1 + lax.shift_right_logical(jl, LT_SHIFT))
    return is_ctx, j, mrow


def _mod(mod_ref, mrow, k):
    return mod_ref[pl.ds(mrow, 1), k * D:(k + 1) * D]


def _store_slabs(ref, x):
    for s in range(NS):
        ref[pl.ds(s, x.shape[0], stride=NS), :] = x[:, s * 128:(s + 1) * 128]


def _load_slabs(ref, rows):
    return jnp.concatenate([ref[pl.ds(s, rows, stride=NS), :] for s in range(NS)], axis=1)


def _dot_nt(a, b):
    return lax.dot_general(a, b, (((1,), (1,)), ((), ())), preferred_element_type=F32)


def _ffn_pre(x1, mrow, mod_ref, nf_ref, wrc_ref, wrh_ref, rb_ref, h2u_ref, ri_ref, rw_ref):
    h2 = _rms(x1, nf_ref[...]) * (1.0 + _mod(mod_ref, mrow, 4)) + _mod(mod_ref, mrow, 3)
    h_hi = h2.astype(BF16)
    h_lo = (h2 - h_hi.astype(F32)).astype(BF16)
    _store_slabs(h2u_ref, h2)

    hw = jnp.dot(h_hi, wrc_ref[...], preferred_element_type=F32)
    lw = jnp.dot(h_lo, wrh_ref[...], preferred_element_type=F32)
    lt = jnp.transpose(hw[:, :NRL] + hw[:, NRL:] + lw + rb_ref[...])
    row8 = lax.broadcasted_iota(I32, (8, TM), 0)
    gl = jnp.where(row8 < N_GROUPS, lt[0:8], -jnp.inf)
    ge = jnp.exp(gl - jnp.max(gl, axis=0, keepdims=True))
    gprob = ge / jnp.sum(ge, axis=0, keepdims=True)
    g_p = jnp.max(gprob, axis=0, keepdims=True)
    g_idx = jnp.min(jnp.where(gprob == g_p, row8, 8), axis=0, keepdims=True)
    e_sel = lt[ER0:ER0 + 8]
    for g in range(1, N_GROUPS):
        e_sel = jnp.where(g_idx == g, lt[ER0 + 8 * g:ER0 + 8 * g + 8], e_sel)
    ee = jnp.exp(e_sel - jnp.max(e_sel, axis=0, keepdims=True))
    eprob = ee / jnp.sum(ee, axis=0, keepdims=True)
    p0 = jnp.max(eprob, axis=0, keepdims=True)
    i0 = jnp.min(jnp.where(eprob == p0, row8, 8), axis=0, keepdims=True)
    rest = jnp.where(row8 == i0, -1.0, eprob)
    p1 = jnp.max(rest, axis=0, keepdims=True)
    i1 = jnp.min(jnp.where(rest == p1, row8, 8), axis=0, keepdims=True)
    psum = p0 + p1
    w0 = g_p * p0 / psum
    w1 = g_p * p1 / psum
    id0 = g_idx * EXP_PER_GROUP + i0
    id1 = g_idx * EXP_PER_GROUP + i1

    ri_ref[...] = jnp.where(row8 == 0, id0, jnp.where(row8 == 1, id1, 0))
    rw_ref[...] = jnp.where(row8 == 0, w0, jnp.where(row8 == 1, w1, 0.0))


def _ffn_in_specs():
    const2 = lambda i: (0, 0)
    return [
        pl.BlockSpec((1, D), const2),
        pl.BlockSpec((D, 2 * NRL), const2),
        pl.BlockSpec((D, NRL), const2),
        pl.BlockSpec((1, NRL), const2),
    ]


def _ffn_out_shapes():
    return [
        jax.ShapeDtypeStruct((T, D), F32),
        jax.ShapeDtypeStruct((T * NS, 128), F32),
        jax.ShapeDtypeStruct((NT, 8, TM), I32),
        jax.ShapeDtypeStruct((NT, 8, TM), F32),
    ]


def _ffn_out_specs():
    return [
        pl.BlockSpec((TM, D), lambda i: (i, 0)),
        pl.BlockSpec((TM * NS, 128), lambda i: (i, 0)),
        pl.BlockSpec((None, 8, TM), lambda i: (i, 0, 0)),
        pl.BlockSpec((None, 8, TM), lambda i: (i, 0, 0)),
    ]


def _mod_kernel(cs_ref, w_ref, b_ref, o_ref):
    s = cs_ref[...]
    a = s * jax.nn.sigmoid(s)
    o_ref[...] = jnp.dot(a, w_ref[...], precision=_HI, preferred_element_type=F32) + b_ref[...]


def _modulation(cs, w_ada, b_ada):
    nb = 6
    return pl.pallas_call(
        _mod_kernel,
        out_shape=jax.ShapeDtypeStruct((DEPTH, 8, 6 * D), F32),
        grid=(DEPTH, nb),
        in_specs=[
            pl.BlockSpec((8, D), lambda l, n: (0, 0)),
            pl.BlockSpec((None, D, D), lambda l, n: (l, 0, n)),
            pl.BlockSpec((None, 1, D), lambda l, n: (l, 0, n)),
        ],
        out_specs=pl.BlockSpec((None, 8, D), lambda l, n: (l, 0, n)),
        compiler_params=_cparams(2, 32),
        name="modulation",
    )(cs, w_ada, b_ada.reshape(DEPTH, 1, 6 * D))


def _halo_specs(width):
    nb = T // HALO
    per = TM // HALO
    return [
        pl.BlockSpec((TM, width), lambda i: (i, 0)),
        pl.BlockSpec((HALO, width), lambda i: (jnp.maximum(i * per - 1, 0), 0)),
        pl.BlockSpec((HALO, width), lambda i: (jnp.minimum((i + 1) * per, nb - 1), 0)),
    ]


def _normed_halo(xt, xp, xn, i, mod_ref, nm_ref):
    is_ctx, j, mrow = _tile_info(i)
    g = nm_ref[...]
    sc = 1.0 + _mod(mod_ref, mrow, 1)
    sh = _mod(mod_ref, mrow, 0)
    pv = jnp.where(jnp.logical_and(jnp.logical_not(is_ctx), j > 0), 1.0, 0.0)
    nv = jnp.where(jnp.logical_and(jnp.logical_not(is_ctx), j < LT - 1), 1.0, 0.0)
    ht = _rms(xt, g) * sc + sh
    hp = (_rms(xp, g) * sc + sh) * pv
    hn = (_rms(xn, g) * sc + sh) * nv
    return ht, hp, hn, is_ctx, j, mrow


def _pool_kernel(split, *refs):
    if split:
        xc_ref, x_ref, xp_ref, xn_ref = refs[:4]
        refs = refs[4:]
    else:
        x_ref, xp_ref, xn_ref = refs[:3]
        refs = refs[3:]
    (mod_ref, nm_ref, band_ref, wp_ref, ps_ref, nf_ref, wrc_ref, wrh_ref, rb_ref,
     x1_ref, h2u_ref, ri_ref, rw_ref) = refs
    i = pl.program_id(0)
    x = x_ref[...]
    if split:
        x = jnp.where(i < NCT, xc_ref[...], x)
    ht, hp, hn, is_ctx, j, mrow = _normed_halo(x, xp_ref[...], xn_ref[...], i, mod_ref, nm_ref)
    hext = jnp.concatenate([ht, hp, hn, jnp.zeros((KB - TM - 2 * HALO, D), F32)], axis=0)
    e_hi = hext.astype(BF16)
    e_lo = (hext - e_hi.astype(F32)).astype(BF16)
    seq_len = jnp.where(is_ctx, CTX_LEN, LAT_LEN)
    t = j * TM + lax.broadcasted_iota(I32, (TM, POOL_CH), 0)
    outs = []
    for g, win in enumerate(POOL_WINDOWS):
        lo = win // 2
        hi = win - 1 - lo
        cols = slice(g * POOL_CH, (g + 1) * POOL_CH)
        band = band_ref[g]
        s = (jnp.dot(band, e_hi[:, cols], preferred_element_type=F32)
             + jnp.dot(band, e_lo[:, cols], preferred_element_type=F32))
        cnt = (jnp.minimum(t + hi, seq_len - 1) - jnp.maximum(t - lo, 0) + 1).astype(F32)
        d = s / cnt - ht[:, cols]
        outs.append(jnp.dot(d.astype(BF16), wp_ref[g], preferred_element_type=F32))
    o = jnp.concatenate(outs, axis=1) * ps_ref[...]
    x1 = x + _mod(mod_ref, mrow, 2) * o
    x1_ref[...] = x1
    _ffn_pre(x1, mrow, mod_ref, nf_ref, wrc_ref, wrh_ref, rb_ref, h2u_ref, ri_ref, rw_ref)


def _pool_band():
    band = np.zeros((len(POOL_WINDOWS), TM, KB), np.float32)
    r = np.arange(TM)[:, None]
    for g, win in enumerate(POOL_WINDOWS):
        lo = win // 2
        hi = win - 1 - lo
        pos = np.concatenate([np.arange(TM), np.arange(-HALO, 0), np.arange(TM, TM + HALO)])[None, :]
        band[g, :, :TM + 2 * HALO] = (pos >= r - lo) & (pos <= r + hi)
    return jnp.asarray(band, BF16)


def _split_halo_specs():
    nb = T_LAT // HALO
    per = TM // HALO
    lat = lambda i: jnp.maximum(i - NCT, 0)
    return [
        pl.BlockSpec((TM, D), lambda i: (jnp.minimum(i, NCT - 1), 0)),
        pl.BlockSpec((TM, D), lambda i: (lat(i), 0)),
        pl.BlockSpec((HALO, D), lambda i: (jnp.maximum(lat(i) * per - 1, 0), 0)),
        pl.BlockSpec((HALO, D), lambda i: (jnp.minimum((lat(i) + 1) * per, nb - 1), 0)),
    ]


def _pool_layer(xs, mod_l, nm, wp, ps, ffn_args):
    const2 = lambda i: (0, 0)
    const3 = lambda i: (0, 0, 0)
    split = isinstance(xs, tuple)
    x_args = (xs[0], xs[1], xs[1], xs[1]) if split else (xs, xs, xs)
    return pl.pallas_call(
        functools.partial(_pool_kernel, split),
        out_shape=_ffn_out_shapes(),
        grid=(NT,),
        in_specs=(_split_halo_specs() if split else _halo_specs(D)) + [
            pl.BlockSpec((8, 6 * D), const2),
            pl.BlockSpec((1, D), const2),
            pl.BlockSpec((len(POOL_WINDOWS), TM, KB), const3),
            pl.BlockSpec((len(POOL_WINDOWS), POOL_CH, POOL_CH), const3),
            pl.BlockSpec((1, D), const2),
        ] + _ffn_in_specs(),
        out_specs=_ffn_out_specs(),
        compiler_params=_cparams(1, 48),
        name="pool_mixer",
    )(*x_args, mod_l, nm, _pool_band(), wp.astype(BF16), ps, *ffn_args)


def _conv_kernel(x_ref, xp_ref, xn_ref, mod_ref, nm_ref, win_ref, cw_ref, wout_ref,
                 nf_ref, wrc_ref, wrh_ref, rb_ref,
                 x1_ref, h2u_ref, ri_ref, rw_ref):
    i = pl.program_id(0)
    x = x_ref[...]
    ht, hp, hn, is_ctx, j, mrow = _normed_halo(x, xp_ref[...], xn_ref[...], i, mod_ref, nm_ref)
    hext = jnp.concatenate([ht, hp, hn], axis=0).astype(BF16)
    bcv = jnp.dot(hext, win_ref[...], preferred_element_type=F32)
    b = bcv[:TM, :D]
    u = bcv[:, D:2 * D] * bcv[:, 2 * D:]
    um = u[:TM]
    u_before = u[TM + HALO - 1:TM + HALO]
    u_after = u[TM + HALO:TM + HALO + 1]
    row = lax.broadcasted_iota(I32, (TM, D), 0)
    up = jnp.where(row == 0, u_before, pltpu.roll(um, 1, 0))
    un = jnp.where(row == TM - 1, u_after, pltpu.roll(um, TM - 1, 0))
    cw = cw_ref[...]
    conv = up * cw[0:1] + um * cw[1:2] + un * cw[2:3]
    o = jnp.dot((b * conv).astype(BF16), wout_ref[...], preferred_element_type=F32)
    x1 = x + _mod(mod_ref, mrow, 2) * o
    x1_ref[...] = x1
    _ffn_pre(x1, mrow, mod_ref, nf_ref, wrc_ref, wrh_ref, rb_ref, h2u_ref, ri_ref, rw_ref)


def _conv_layer(x, mod_l, nm, w_in, cw, w_out, ffn_args):
    const2 = lambda i: (0, 0)
    cw8 = jnp.concatenate([cw, jnp.zeros((8 - cw.shape[0], D), F32)], axis=0)
    return pl.pallas_call(
        _conv_kernel,
        out_shape=_ffn_out_shapes(),
        grid=(NT,),
        in_specs=_halo_specs(D) + [
            pl.BlockSpec((8, 6 * D), const2),
            pl.BlockSpec((1, D), const2),
            pl.BlockSpec((D, 3 * D), const2),
            pl.BlockSpec((8, D), const2),
            pl.BlockSpec((D, D), const2),
        ] + _ffn_in_specs(),
        out_specs=_ffn_out_specs(),
        compiler_params=_cparams(1, 56),
        name="conv_mixer",
    )(x, x, x, mod_l, nm, w_in.astype(BF16), cw8, w_out.astype(BF16), *ffn_args)


W_CAT = Q_RANK + KV_RANK + 2 * QK_ROPE
QH = 2 * QK_NOPE


def _mla_proj_kernel(x_ref, mod_ref, nm_ref, wcat_ref, qn_ref, kvn_ref, wq_ref, wqs_ref,
                     c_ref, s_ref, q_ref, ckv_ref, kpe_ref):
    i = pl.program_id(0)
    _, _, mrow = _tile_info(i)
    h = _rms(x_ref[...], nm_ref[...]) * (1.0 + _mod(mod_ref, mrow, 1)) + _mod(mod_ref, mrow, 0)
    y = jnp.dot(h.astype(BF16), wcat_ref[...], preferred_element_type=F32)
    cqn = _rms(y[:, :Q_RANK], qn_ref[...]).astype(BF16)
    q = jnp.dot(cqn, wq_ref[...], preferred_element_type=F32)
    qs = jnp.dot(cqn, wqs_ref[...], preferred_element_type=F32)
    cos = c_ref[...]
    sin = s_ref[...]
    parts = []
    for hh in range(HEADS):
        parts.append(q[:, hh * QH:hh * QH + QK_NOPE])
        parts.append(q[:, hh * QH + QK_NOPE:(hh + 1) * QH] * cos + qs[:, hh * 128:(hh + 1) * 128] * sin)
    q_ref[...] = (jnp.concatenate(parts, axis=1) * (SM_SCALE * LOG2E)).astype(BF16)
    ckv_ref[...] = _rms(y[:, Q_RANK:Q_RANK + KV_RANK], kvn_ref[...])
    k2 = y[:, Q_RANK + KV_RANK:]
    kr = k2 * cos + pltpu.roll(k2, QK_ROPE, 1) * sin
    lane = lax.broadcasted_iota(I32, (TM, 2 * QK_ROPE), 1)
    kpe_ref[...] = jnp.where(lane < QK_ROPE, kr, 0.0)


def _rope_tables():
    rows_n = LAT_LEN // GRID_W
    rows = jnp.repeat(jnp.arange(rows_n), GRID_W).astype(F32)
    cols = jnp.tile(jnp.arange(GRID_W), rows_n).astype(F32)
    inv = ROPE_BASE ** (-(jnp.arange(AXIS_FREQS, dtype=F32) / AXIS_FREQS))
    ang = jnp.stack([rows[:, None] * inv, cols[:, None] * inv], axis=1)
    cos, sin = jnp.cos(ang), jnp.sin(ang)
    c64 = jnp.concatenate([cos[:, 0], cos[:, 0], cos[:, 1], cos[:, 1]], axis=1)
    s64 = jnp.concatenate([-sin[:, 0], sin[:, 0], -sin[:, 1], sin[:, 1]], axis=1)
    c = jnp.concatenate([c64, jnp.ones((LAT_LEN, QK_ROPE), F32)], axis=1)
    s = jnp.concatenate([s64, jnp.zeros((LAT_LEN, QK_ROPE), F32)], axis=1)
    c = jnp.concatenate([c, jnp.ones((TM, 2 * QK_ROPE), F32)], axis=0)
    s = jnp.concatenate([s, jnp.zeros((TM, 2 * QK_ROPE), F32)], axis=0)
    return c, s


def _swap_rope_cols(w):
    f = AXIS_FREQS
    return jnp.concatenate([w[..., f:2 * f], w[..., :f], w[..., 3 * f:], w[..., 2 * f:3 * f]], axis=-1)


def _mla_proj(x, mod_l, nm, w_dq, q_norm, w_uq, w_dkv, kv_norm):
    const2 = lambda i: (0, 0)
    w_kpe = w_dkv[:, KV_RANK:]
    wcat = jnp.concatenate([w_dq, w_dkv[:, :KV_RANK], w_kpe, _swap_rope_cols(w_kpe)], axis=1).astype(BF16)
    wq = w_uq.reshape(Q_RANK, HEADS, QK_NOPE + QK_ROPE)
    zpad = jnp.zeros((Q_RANK, HEADS, QK_ROPE), F32)
    wq_a = jnp.concatenate([wq, zpad], axis=2).reshape(Q_RANK, HEADS * QH).astype(BF16)
    wq_s = jnp.concatenate([_swap_rope_cols(wq[:, :, QK_NOPE:]), zpad], axis=2)
    wq_s = wq_s.reshape(Q_RANK, HEADS * 128).astype(BF16)
    cos, sin = _rope_tables()
    tab_idx = lambda i: (jnp.where(i < NCT, LT, jnp.maximum(i - NCT, 0) & (LT - 1)), 0)
    return pl.pallas_call(
        _mla_proj_kernel,
        out_shape=[jax.ShapeDtypeStruct((T, HEADS * QH), BF16),
                   jax.ShapeDtypeStruct((T, KV_RANK), F32),
                   jax.ShapeDtypeStruct((T, 2 * QK_ROPE), F32)],
        grid=(NT,),
        in_specs=[
            pl.BlockSpec((TM, D), lambda i: (i, 0)),
            pl.BlockSpec((8, 6 * D), const2),
            pl.BlockSpec((1, D), const2),
            pl.BlockSpec((D, W_CAT), const2),
            pl.BlockSpec((1, Q_RANK), const2),
            pl.BlockSpec((1, KV_RANK), const2),
            pl.BlockSpec((Q_RANK, HEADS * QH), const2),
            pl.BlockSpec((Q_RANK, HEADS * 128), const2),
            pl.BlockSpec((TM, 2 * QK_ROPE), tab_idx),
            pl.BlockSpec((TM, 2 * QK_ROPE), tab_idx),
        ],
        out_specs=[pl.BlockSpec((TM, HEADS * QH), lambda i: (i, 0)),
                   pl.BlockSpec((TM, KV_RANK), lambda i: (i, 0)),
                   pl.BlockSpec((TM, 2 * QK_ROPE), lambda i: (i, 0))],
        compiler_params=_cparams(1, 48),
        name="mla_proj",
    )(x, mod_l, nm, wcat, q_norm, kv_norm, wq_a, wq_s, cos, sin)


def _kv_expand_kernel(ckv_ref, kpe_ref, wk_ref, wv_ref, k_ref, v_ref):
    c = ckv_ref[...].astype(BF16)
    kn = jnp.dot(c, wk_ref[...], preferred_element_type=F32).astype(BF16)
    kp = kpe_ref[...].astype(BF16)
    parts = []
    for hh in range(HEADS):
        parts.append(kn[:, hh * QK_NOPE:(hh + 1) * QK_NOPE])
        parts.append(kp)
    k_ref[...] = jnp.concatenate(parts, axis=1)
    v_ref[...] = jnp.dot(c, wv_ref[...], preferred_element_type=F32).astype(BF16)


def _kv_expand(ckv, kpe, wk, wv, name):
    n = ckv.shape[0]
    const2 = lambda i: (0, 0)
    return pl.pallas_call(
        _kv_expand_kernel,
        out_shape=[jax.ShapeDtypeStruct((n, HEADS * QH), BF16),
                   jax.ShapeDtypeStruct((n, HEADS * V_DIM), BF16)],
        grid=(n // TM,),
        in_specs=[
            pl.BlockSpec((TM, KV_RANK), lambda i: (i, 0)),
            pl.BlockSpec((TM, 2 * QK_ROPE), lambda i: (i, 0)),
            pl.BlockSpec((KV_RANK, HEADS * QK_NOPE), const2),
            pl.BlockSpec((KV_RANK, HEADS * V_DIM), const2),
        ],
        out_specs=[pl.BlockSpec((TM, HEADS * QH), lambda i: (i, 0)),
                   pl.BlockSpec((TM, HEADS * V_DIM), lambda i: (i, 0))],
        compiler_params=_cparams(1, 32),
        name=name,
    )(ckv, kpe, wk, wv)


def _attend(q, k, v):
    s = _dot_nt(q, k)
    p = jnp.exp2(s - jnp.max(s, axis=1, keepdims=True))
    l = jnp.sum(p, axis=1, keepdims=True)
    return jnp.dot(p.astype(BF16), v, preferred_element_type=F32) / l


KC = 512
NKC = NK_LAT // KC


def _attn_lat_kernel(q_ref, k_ref, v_ref, o_ref, s_ref, m_ref, l_ref, acc_ref):
    q = q_ref[...]
    m_ref[...] = jnp.full((TQ, 128), -jnp.inf, F32)

    def scores(c, carry):
        k = k_ref[pl.ds(pl.multiple_of(c * KC, KC), KC), :]
        s = _dot_nt(q, k)
        s_ref[c] = s
        m = m_ref[...]
        for j in range(KC // 128):
            m = jnp.maximum(m, s[:, j * 128:(j + 1) * 128])
        m_ref[...] = m
        return carry
    lax.fori_loop(0, NKC, scores, 0, unroll=True)

    mb = jnp.broadcast_to(jnp.max(m_ref[...], axis=1, keepdims=True), (TQ, 128))
    l_ref[...] = jnp.zeros((TQ, 128), F32)
    acc_ref[...] = jnp.zeros((TQ, V_DIM), F32)

    def weighted(c, carry):
        s = s_ref[c]
        ps = [jnp.exp2(s[:, j * 128:(j + 1) * 128] - mb) for j in range(KC // 128)]
        l = l_ref[...]
        for pj in ps:
            l = l + pj
        l_ref[...] = l
        p = jnp.concatenate(ps, axis=1).astype(BF16)
        v = v_ref[pl.ds(pl.multiple_of(c * KC, KC), KC), :]
        acc_ref[...] += jnp.dot(p, v, preferred_element_type=F32)
        return carry
    lax.fori_loop(0, NKC, weighted, 0, unroll=3)

    o_ref[...] = (acc_ref[...] / jnp.sum(l_ref[...], axis=1, keepdims=True)).astype(BF16)


def _attn_lat(q, k, v):
    nq = LAT_LEN // TQ
    q0 = T_CTX // TQ
    return pl.pallas_call(
        _attn_lat_kernel,
        scratch_shapes=[pltpu.VMEM((NKC, TQ, KC), F32), pltpu.VMEM((TQ, 128), F32),
                        pltpu.VMEM((TQ, 128), F32), pltpu.VMEM((TQ, V_DIM), F32)],
        out_shape=jax.ShapeDtypeStruct((T_LAT, HEADS * V_DIM), BF16),
        grid=(N_LAT_SEQ, HEADS, nq),
        in_specs=[
            pl.BlockSpec((TQ, QH), lambda b, h, t: (q0 + b * nq + t, h)),
            pl.BlockSpec((NK_LAT, QH), lambda b, h, t: (b, h)),
            pl.BlockSpec((NK_LAT, V_DIM), lambda b, h, t: (b, h)),
        ],
        out_specs=pl.BlockSpec((TQ, V_DIM), lambda b, h, t: (b * nq + t, h)),
        compiler_params=_cparams(3, 48),
        name="attn_latent",
    )(q, k, v)


def _attn_ctx_kernel(q_ref, k_ref, v_ref, o_ref):
    outs = []
    for hh in range(HEADS):
        outs.append(_attend(q_ref[:, hh * QH:(hh + 1) * QH], k_ref[:, hh * QH:(hh + 1) * QH],
                            v_ref[:, hh * V_DIM:(hh + 1) * V_DIM]))
    o_ref[...] = jnp.concatenate(outs, axis=1).astype(BF16)


def _attn_ctx(q, k, v):
    return pl.pallas_call(
        _attn_ctx_kernel,
        out_shape=jax.ShapeDtypeStruct((T_CTX, HEADS * V_DIM), BF16),
        grid=(N_CTX_SEQ,),
        in_specs=[
            pl.BlockSpec((CTX_LEN, HEADS * QH), lambda b: (b, 0)),
            pl.BlockSpec((CTX_LEN, HEADS * QH), lambda b: (b, 0)),
            pl.BlockSpec((CTX_LEN, HEADS * V_DIM), lambda b: (b, 0)),
        ],
        out_specs=pl.BlockSpec((CTX_LEN, HEADS * V_DIM), lambda b: (b, 0)),
        compiler_params=_cparams(1, 32),
        name="attn_context",
    )(q, k, v)


def _attn_out_kernel(x_ref, oc_ref, ol_ref, mod_ref, wo_ref,
                     nf_ref, wrc_ref, wrh_ref, rb_ref,
                     x1_ref, h2u_ref, ri_ref, rw_ref):
    i = pl.program_id(0)
    is_ctx, _, mrow = _tile_info(i)
    att = jnp.where(is_ctx, oc_ref[...], ol_ref[...])
    o = jnp.dot(att, wo_ref[...], preferred_element_type=F32)
    x1 = x_ref[...] + _mod(mod_ref, mrow, 2) * o
    x1_ref[...] = x1
    _ffn_pre(x1, mrow, mod_ref, nf_ref, wrc_ref, wrh_ref, rb_ref, h2u_ref, ri_ref, rw_ref)


def _attn_out(x, o_ctx, o_lat, mod_l, w_o, ffn_args):
    const2 = lambda i: (0, 0)
    return pl.pallas_call(
        _attn_out_kernel,
        out_shape=_ffn_out_shapes(),
        grid=(NT,),
        in_specs=[
            pl.BlockSpec((TM, D), lambda i: (i, 0)),
            pl.BlockSpec((TM, D), lambda i: (jnp.minimum(i, NCT - 1), 0)),
            pl.BlockSpec((TM, D), lambda i: (jnp.maximum(i - NCT, 0), 0)),
            pl.BlockSpec((8, 6 * D), const2),
            pl.BlockSpec((D, D), const2),
        ] + _ffn_in_specs(),
        out_specs=_ffn_out_specs(),
        compiler_params=_cparams(1, 48),
        name="attn_out",
    )(x, o_ctx, o_lat, mod_l, w_o.astype(BF16), *ffn_args)


def _mla_layer(x, mod_l, nm, cache_ckv, cache_kpe, w_dq, q_norm, w_uq, w_dkv, kv_norm, w_ukv, w_o, ffn_args):
    q, ckv, kpe = _mla_proj(x, mod_l, nm, w_dq, q_norm, w_uq, w_dkv, kv_norm)
    wkv = w_ukv.reshape(KV_RANK, HEADS, QK_NOPE + V_DIM)
    wk = wkv[:, :, :QK_NOPE].reshape(KV_RANK, HEADS * QK_NOPE).astype(BF16)
    wv = wkv[:, :, QK_NOPE:].reshape(KV_RANK, HEADS * V_DIM).astype(BF16)
    k_c, v_c = _kv_expand(ckv[:T_CTX], kpe[:T_CTX], wk, wv, "kv_expand_context")
    kpe_cache = jnp.concatenate([cache_kpe, jnp.zeros_like(cache_kpe)], axis=-1)
    ckv_l = jnp.concatenate([cache_ckv, ckv[T_CTX:].reshape(N_LAT_SEQ, LAT_LEN, KV_RANK)], axis=1)
    kpe_l = jnp.concatenate([kpe_cache, kpe[T_CTX:].reshape(N_LAT_SEQ, LAT_LEN, 2 * QK_ROPE)], axis=1)
    k_l, v_l = _kv_expand(ckv_l.reshape(N_LAT_SEQ * NK_LAT, KV_RANK),
                          kpe_l.reshape(N_LAT_SEQ * NK_LAT, 2 * QK_ROPE), wk, wv, "kv_expand_latent")
    o_c = _attn_ctx(q, k_c, v_c)
    o_l = _attn_lat(q, k_l, v_l)
    outs = _attn_out(x, o_c, o_l, mod_l, w_o, ffn_args)
    new_ckv = ckv[:T_CTX].reshape(N_CTX_SEQ, 1, CTX_LEN, KV_RANK)
    new_kpe = kpe[:T_CTX, :QK_ROPE].reshape(N_CTX_SEQ, 1, CTX_LEN, QK_ROPE)
    return outs, new_ckv, new_kpe


SRC_BITS = 16
assert T <= 1 << SRC_BITS and 2 * TP <= 1 << (32 - SRC_BITS)
ROW_STEP = (1 << SRC_BITS) + 1


def _slot_kernel(meta_ref, psv_ref, ri_ref, sv_ref, stage, buf, sem):
    i = pl.program_id(0)

    @pl.when(i == 0)
    def _():
        def fill(p, c):
            tile_parity = lax.shift_right_logical(p, jnp.int32(TS.bit_length() - 1)) & 1
            sv_ref[p] = (T + tile_parity * TS + (p & (TS - 1))) * (1 << SRC_BITS)
            return c

        def fill_expert(e, c):
            return lax.fori_loop(meta_ref[1, e], meta_ref[2, e], fill, c)
        lax.fori_loop(0, N_EXPERTS, fill_expert, 0)
        lax.fori_loop(meta_ref[2, N_EXPERTS - 1], CAP, fill, 0)

    @pl.when(i < NT)
    def _():
        ri = ri_ref[...]
        rowe = lax.broadcasted_iota(I32, (N_EXPERTS, TM), 0)
        ps = psv_ref[...]
        pos0 = jnp.sum(jnp.where(rowe == ri[0:1], ps, 0), axis=0, keepdims=True) + ri[2:3]
        pos1 = jnp.sum(jnp.where(rowe == ri[1:2], ps, 0), axis=0, keepdims=True) + ri[3:4]
        row8 = lax.broadcasted_iota(I32, (8, 128), 0)
        st = jnp.zeros((8, 128), I32)
        for k, pos in enumerate((pos0, pos1)):
            for h in range(LC):
                st = jnp.where(row8 == k * LC + h, pos[:, h * 128:(h + 1) * 128], st)
        stage[i % 2] = st
        pltpu.make_async_copy(stage.at[i % 2], buf.at[i % 2], sem.at[i % 2]).start()

    @pl.when(i >= 1)
    def _():
        j = i - 1
        b = j % 2
        pltpu.make_async_copy(stage.at[b], buf.at[b], sem.at[b]).wait()
        for h in range(LC):
            def tok(t, v, h=h):
                sv_ref[buf[b, h, t]] = v
                sv_ref[buf[b, LC + h, t]] = v + (TP << SRC_BITS)
                return v + ROW_STEP
            lax.fori_loop(0, 128, tok, (j * TM + h * 128) * ROW_STEP, unroll=8)


def _rank_kernel(ri_ref, trio_ref, ro_ref, cnt_ref, carry_ref):
    @pl.when(pl.program_id(0) == 0)
    def _():
        carry_ref[...] = jnp.zeros_like(carry_ref)

    rowe = lax.broadcasted_iota(I32, (N_EXPERTS, TM), 0)
    row8 = lax.broadcasted_iota(I32, (8, TM), 0)
    ids = [ri_ref[tt] for tt in range(RG)]
    ohs = [rowe == ids[tt][k:k + 1] for tt in range(RG) for k in range(2)]
    ohb = jnp.concatenate([jnp.where(oh, 1.0, 0.0).astype(BF16) for oh in ohs], axis=0)
    pt = jnp.dot(ohb, trio_ref[...], preferred_element_type=F32)
    carry = carry_ref[...]
    for tt in range(RG):
        ranks = []
        for k in range(2):
            r0 = (tt * 2 + k) * N_EXPERTS
            before = carry + pt[r0:r0 + N_EXPERTS, :TM]
            ranks.append(jnp.sum(jnp.where(ohs[tt * 2 + k], before, 0.0), axis=0, keepdims=True).astype(I32))
            carry = carry + pt[r0:r0 + N_EXPERTS, TM:]
        ro_ref[tt] = jnp.where(row8 == 0, ids[tt][0:1],
                     jnp.where(row8 == 1, ids[tt][1:2],
                     jnp.where(row8 == 2, ranks[0],
                     jnp.where(row8 == 3, ranks[1], 0))))
    carry_ref[...] = carry
    cnt_ref[...] = carry


def _rank(route_i):
    tri = np.triu(np.ones((TM, TM), np.float32), 1)
    trio = jnp.asarray(np.concatenate([tri, np.ones((TM, TM), np.float32)], axis=1), BF16)
    return pl.pallas_call(
        _rank_kernel,
        out_shape=[jax.ShapeDtypeStruct((NT, 8, TM), I32), jax.ShapeDtypeStruct((N_EXPERTS, TM), F32)],
        grid=(NT // RG,),
        in_specs=[pl.BlockSpec((RG, 8, TM), lambda g: (g, 0, 0)),
                  pl.BlockSpec((TM, 2 * TM), lambda g: (0, 0))],
        out_specs=[pl.BlockSpec((RG, 8, TM), lambda g: (g, 0, 0)),
                   pl.BlockSpec((N_EXPERTS, TM), lambda g: (0, 0))],
        scratch_shapes=[pltpu.VMEM((N_EXPERTS, TM), F32)],
        compiler_params=_cparams(1, 16),
        name="expert_rank",
    )(route_i, trio)


def _slot_map(meta, pad_starts, route_i):
    return pl.pallas_call(
        _slot_kernel,
        out_shape=jax.ShapeDtypeStruct((CAP,), I32),
        grid=(NT + 1,),
        in_specs=[pl.BlockSpec(memory_space=pltpu.SMEM),
                  pl.BlockSpec((N_EXPERTS, TM), lambda i: (0, 0)),
                  pl.BlockSpec((None, 8, TM), lambda i: (jnp.minimum(i, NT - 1), 0, 0))],
        out_specs=pl.BlockSpec(memory_space=pltpu.SMEM),
        scratch_shapes=[pltpu.VMEM((2, 8, 128), I32), pltpu.SMEM((2, 8, 128), I32),
                        pltpu.SemaphoreType.DMA((2,))],
        compiler_params=_cparams(1, 16),
        name="slot_map",
    )(meta, jnp.broadcast_to(pad_starts[:, None], (N_EXPERTS, TM)), route_i)


def _moe_kernel(blk_ref, nu_ref, sv_ref, h2u_ref, w1_ref, w3_ref, w2_ref, o_ref,
                idx_ref, xbuf, obuf, wb1, wb3, wb2, pe_ref, isem, gsem, ssem):
    i = pl.program_id(0)
    n_used = nu_ref[0]
    slot = i % 2

    def idx_copy(tile):
        return pltpu.make_async_copy(sv_ref.at[tile], idx_ref.at[tile % 3], isem.at[tile % 3])

    def start_gather(tile):
        b = tile % 3
        sl = tile % 2
        for r in range(TS):
            src = idx_ref[b, 0, r] & ((1 << SRC_BITS) - 1)
            pltpu.make_async_copy(h2u_ref.at[pl.ds(pl.multiple_of(src * NS, NS), NS)],
                                  xbuf.at[sl, pl.ds(r * NS, NS)], gsem.at[sl]).start(priority=r % 2)

    def wait_gather(sl):
        pltpu.make_async_copy(h2u_ref.at[pl.ds(0, TS * NS)], xbuf.at[sl], gsem.at[sl]).wait()

    def wait_scatter(sl):
        pltpu.make_async_copy(obuf.at[sl], o_ref.at[pl.ds(0, TS * NS)], ssem.at[sl]).wait()

    @pl.when(i == 0)
    def _():
        pe_ref[0] = -1
        obuf[0] = jnp.zeros((TS * NS, 128), F32)
        fills = [pltpu.make_async_copy(obuf.at[0], o_ref.at[pl.ds((k * TP + T + h * TS) * NS, TS * NS)], ssem.at[0])
                 for k in range(2) for h in range(2)]
        for f in fills:
            f.start()
        for f in fills:
            f.wait()
        c0 = idx_copy(0)
        c0.start()
        c0.wait()
        start_gather(0)

        @pl.when(n_used > 1)
        def _():
            idx_copy(1).start()

    @pl.when(i < n_used)
    def _():
        @pl.when(i + 1 < n_used)
        def _():
            idx_copy(i + 1).wait()
            start_gather(i + 1)

        @pl.when(i + 2 < n_used)
        def _():
            idx_copy(i + 2).start()

        e = blk_ref[i]

        @pl.when(e != pe_ref[0])
        def _():
            wb1[...] = w1_ref[...].astype(BF16)
            wb3[...] = w3_ref[...].astype(BF16)
            wb2[...] = w2_ref[...].astype(BF16)
            pe_ref[0] = e

        wait_gather(slot)
        xb = _load_slabs(xbuf.at[slot], TS).astype(BF16)
        a = jnp.dot(xb, wb1[...], preferred_element_type=F32)
        b = jnp.dot(xb, wb3[...], preferred_element_type=F32)
        hm = (a * jax.nn.sigmoid(a) * b).astype(BF16)
        out = jnp.dot(hm, wb2[...], preferred_element_type=F32)

        @pl.when(i >= 2)
        def _():
            wait_scatter(slot)

        _store_slabs(obuf.at[slot], out)
        bi = i % 3
        for r in range(TS):
            dst = lax.shift_right_logical(idx_ref[bi, 0, r], jnp.int32(SRC_BITS))
            pltpu.make_async_copy(obuf.at[slot, pl.ds(r * NS, NS)],
                                  o_ref.at[pl.ds(pl.multiple_of(dst * NS, NS), NS)],
                                  ssem.at[slot]).start(priority=r % 2)

        @pl.when(i == n_used - 1)
        def _():
            wait_scatter(slot)

            @pl.when(i >= 1)
            def _():
                wait_scatter(1 - slot)


def _moe_experts(l, blk_e, n_used, slot_val, h2u, w1, w3, w2):
    grid_spec = pltpu.PrefetchScalarGridSpec(
        num_scalar_prefetch=2,
        grid=(NST,),
        in_specs=[
            pl.BlockSpec(memory_space=pl.ANY),
            pl.BlockSpec(memory_space=pl.ANY),
            pl.BlockSpec((None, None, D, D_EXPERT), lambda i, blk, nu: (l, blk[i], 0, 0)),
            pl.BlockSpec((None, None, D, D_EXPERT), lambda i, blk, nu: (l, blk[i], 0, 0)),
            pl.BlockSpec((None, None, D_EXPERT, D), lambda i, blk, nu: (l, blk[i], 0, 0)),
        ],
        out_specs=pl.BlockSpec(memory_space=pl.ANY),
        scratch_shapes=[
            pltpu.SMEM((3, 1, TS), I32),
            pltpu.VMEM((2, TS * NS, 128), F32),
            pltpu.VMEM((2, TS * NS, 128), F32),
            pltpu.VMEM((D, D_EXPERT), BF16),
            pltpu.VMEM((D, D_EXPERT), BF16),
            pltpu.VMEM((D_EXPERT, D), BF16),
            pltpu.SMEM((1,), I32),
            pltpu.SemaphoreType.DMA((3,)),
            pltpu.SemaphoreType.DMA((2,)),
            pltpu.SemaphoreType.DMA((2,)),
        ],
    )
    return pl.pallas_call(
        _moe_kernel,
        out_shape=jax.ShapeDtypeStruct((2 * TP * NS, 128), F32),
        grid_spec=grid_spec,
        compiler_params=_cparams(1, 48),
        name="moe_experts",
    )(blk_e, n_used, slot_val.reshape(NST, 1, TS), h2u, w1, w3, w2)


def _lane_to_col(row):
    r = lax.broadcasted_iota(I32, (TM, TM), 0)
    c = lax.broadcasted_iota(I32, (TM, TM), 1)
    return jnp.sum(jnp.where(r == c, row, 0.0), axis=1, keepdims=True)


def _combine_kernel(final, x1_ref, o0_ref, o1_ref, rw_ref, mod_ref, nfin_ref, *y_refs):
    i = pl.program_id(0)
    _, _, mrow = _tile_info(i)
    rw = rw_ref[...]
    w0 = _lane_to_col(rw[0:1])
    w1 = _lane_to_col(rw[1:2])
    g2 = _mod(mod_ref, mrow, 5)
    x2 = x1_ref[...] + g2 * (_load_slabs(o0_ref, TM) * w0 + _load_slabs(o1_ref, TM) * w1)
    if final:
        yc_ref, yl_ref = y_refs
        y = _rms(x2, nfin_ref[...])

        @pl.when(i < NCT)
        def _():
            yc_ref[...] = y

        @pl.when(i >= NCT)
        def _():
            yl_ref[...] = y
    else:
        y_refs[0][...] = x2


def _combine(x1, o_rows, route_w, mod_l, norm_final, final):
    const2 = lambda i: (0, 0)
    if final:
        out_shape = [jax.ShapeDtypeStruct((T_CTX, D), F32), jax.ShapeDtypeStruct((T_LAT, D), F32)]
        out_specs = [pl.BlockSpec((TM, D), lambda i: (jnp.minimum(i, NCT - 1), 0)),
                     pl.BlockSpec((TM, D), lambda i: (jnp.maximum(i - NCT, 0), 0))]
    else:
        out_shape = [jax.ShapeDtypeStruct((T, D), F32)]
        out_specs = [pl.BlockSpec((TM, D), lambda i: (i, 0))]
    return pl.pallas_call(
        functools.partial(_combine_kernel, final),
        out_shape=out_shape,
        grid=(NT,),
        in_specs=[
            pl.BlockSpec((TM, D), lambda i: (i, 0)),
            pl.BlockSpec((TM * NS, 128), lambda i: (i, 0)),
            pl.BlockSpec((TM * NS, 128), lambda i: (TP // TM + i, 0)),
            pl.BlockSpec((None, 8, TM), lambda i: (i, 0, 0)),
            pl.BlockSpec((8, 6 * D), const2),
            pl.BlockSpec((1, D), const2),
        ],
        out_specs=out_specs,
        compiler_params=_cparams(1, 32),
        name="moe_combine",
    )(x1, o_rows, o_rows, route_w, mod_l, norm_final)


def _moe_layer(l, x1, h2u, route_ids, route_w, mod_l, w1, w3, w2, norm_final, final):
    route_i, counts = _rank(route_ids)
    sizes = counts[:, 0].astype(I32)
    padded = (sizes + TS - 1) // TS * TS
    pad_ends = jnp.cumsum(padded)
    pad_starts = pad_ends - padded
    n_used = (pad_ends[-1:] // TS).astype(I32)
    tile_start = jnp.arange(NST, dtype=I32) * TS
    blk_e = jnp.minimum(jnp.sum((pad_ends[None, :] <= tile_start[:, None]).astype(I32), axis=1),
                        N_EXPERTS - 1).astype(I32)
    meta = jnp.stack([pad_starts, pad_starts + sizes, pad_ends]).astype(I32)
    slot_val = _slot_map(meta, pad_starts.astype(I32), route_i)
    o_rows = _moe_experts(l, blk_e, n_used, slot_val, h2u, w1, w3, w2)
    return _combine(x1, o_rows, route_w, mod_l, norm_final, final)


def _routing_params(w_rg, b_rg, w_re, b_re):
    w = jnp.zeros((D, NRL), F32).at[:, :N_GROUPS].set(w_rg).at[:, ER0:ER0 + N_EXPERTS].set(w_re)
    hi = w.astype(BF16)
    lo = (w - hi.astype(F32)).astype(BF16)
    bias = jnp.zeros((1, NRL), F32).at[0, :N_GROUPS].set(b_rg).at[0, ER0:ER0 + N_EXPERTS].set(b_re)
    return jnp.concatenate([hi, lo], axis=1), hi, bias


def kernel(x_prompt, x_sample, cache_ckv, cache_kpe, c, c_ctx, norm_mix, norm_ffn, norm_final, w_ada, b_ada, w_pool, pool_scale, w_conv_in, conv_w, w_conv_out, w_dq, q_norm, w_uq, w_dkv, kv_norm, w_ukv, w_o, w_route_g, b_route_g, w_route_e, b_route_e, w1, w3, w2):
    x = (x_prompt.reshape(T_CTX, D), x_sample.reshape(T_LAT, D))
    cs =jnp.concatenate([c_ctx[None, :], c, jnp.zeros((8 - 1 - N_LAT_SEQ, D), F32)], axis=0)
    mod_all = _modulation(cs, w_ada, b_ada)
    nfin = norm_final[None, :]
    new_ckv = new_kpe = None
    for l in range(DEPTH):
        kind, j = l % 3, l // 3
        mod_l = mod_all[l]
        nm = norm_mix[l][None, :]
        ffn_args = (norm_ffn[l][None, :],) + _routing_params(w_route_g[l], b_route_g[l], w_route_e[l], b_route_e[l])
        if kind == 0:
            outs = _pool_layer(x, mod_l, nm, w_pool[j], pool_scale[j][None, :], ffn_args)
        elif kind == 1:
            outs = _conv_layer(x, mod_l, nm, w_conv_in[j], conv_w[j], w_conv_out[j], ffn_args)
        else:
            outs, new_ckv, new_kpe = _mla_layer(
                x, mod_l, nm, cache_ckv[:, j], cache_kpe[:, j], w_dq[j], q_norm[j][None, :], w_uq[j],
                w_dkv[j], kv_norm[j][None, :], w_ukv[j], w_o[j], ffn_args)
        x1, h2u, route_ids, route_w = outs
        ys = _moe_layer(l, x1, h2u, route_ids, route_w, mod_l, w1, w3, w2, nfin, l == DEPTH - 1)
        x = ys[0]
    y_prompt = ys[0].reshape(N_CTX_SEQ, CTX_LEN, D)
    y_sample = ys[1].reshape(N_LAT_SEQ, LAT_LEN, D)
    return (y_prompt, y_sample, new_ckv, new_kpe)
```

```python
import functools

import numpy as np
import jax
import jax.numpy as jnp
from jax import lax
from jax.experimental import pallas as pl
from jax.experimental.pallas import tpu as pltpu

F32 = jnp.float32
BF16 = jnp.bfloat16
I32 = jnp.int32

D = 1024
N_CTX_SEQ, CTX_LEN = 32, 256
N_LAT_SEQ, LAT_LEN = 4, 4096
PAST = 512
DEPTH = 4
GRID_W = 64
POOL_WINDOWS = (2, 4, 8, 16)
POOL_CH = D // 4
HEADS = 8
QK_NOPE, QK_ROPE, V_DIM = 128, 64, 128
Q_RANK, KV_RANK = 512, 256
AXIS_FREQS = QK_ROPE // 4
ROPE_BASE = 10000.0
SM_SCALE = (QK_NOPE + QK_ROPE) ** -0.5
LOG2E = 1.4426950408889634
N_GROUPS, EXP_PER_GROUP = 4, 8
N_EXPERTS = N_GROUPS * EXP_PER_GROUP
D_EXPERT = 512
EPS = 1e-6

T_CTX = N_CTX_SEQ * CTX_LEN
T_LAT = N_LAT_SEQ * LAT_LEN
T = T_CTX + T_LAT
TM = 256
NT = T // TM
NCT = T_CTX // TM
LT = LAT_LEN // TM
LT_SHIFT = LT.bit_length() - 1
HALO = 8
KB = TM + 128
NRL = 128
ER0 = 16
RG = 4
LC = TM // 128
assert 2 * LC <= 8 and NT % RG == 0
TS = 256
CAP = 2 * T + N_EXPERTS * TS
NST = CAP // TS
NS = D // 128
TP = T + 2 * TS
NK_LAT = PAST + LAT_LEN
TQ = 512
MIB = 1024 * 1024

_HI = lax.Precision.HIGHEST


def _cparams(n_axes, vmem_mib):
    return pltpu.CompilerParams(
        dimension_semantics=("arbitrary",) * n_axes,
        vmem_limit_bytes=vmem_mib * MIB)


def _rms(x, g):
    return x * lax.rsqrt(jnp.mean(x * x, axis=-1, keepdims=True) + EPS) * g


def _tile_info(i):
    is_ctx = i < NCT
    jl = jnp.maximum(i - NCT, 0)
    j = jnp.where(is_ctx, 0, jl & (LT - 1))
    mrow = jnp.where(is_ctx, 0, 1 + lax.shift_right_logical(jl, LT_SHIFT))
    return is_ctx, j, mrow


def _mod(mod_ref, mrow, k):
    return mod_ref[pl.ds(mrow, 1), k * D:(k + 1) * D]


def _store_slabs(ref, x):
    for s in range(NS):
        ref[pl.ds(s, x.shape[0], stride=NS), :] = x[:, s * 128:(s + 1) * 128]


def _load_slabs(ref, rows):
    return jnp.concatenate([ref[pl.ds(s, rows, stride=NS), :] for s in range(NS)], axis=1)


def _dot_nt(a, b):
    return lax.dot_general(a, b, (((1,), (1,)), ((), ())), preferred_element_type=F32)


def _ffn_pre(x1, mrow, mod_ref, nf_ref, wrc_ref, wrh_ref, rb_ref, h2u_ref, ri_ref, rw_ref):
    h2 = _rms(x1, nf_ref[...]) * (1.0 + _mod(mod_ref, mrow, 4)) + _mod(mod_ref, mrow, 3)
    h_hi = h2.astype(BF16)
    h_lo = (h2 - h_hi.astype(F32)).astype(BF16)
    _store_slabs(h2u_ref, h2)

    hw = jnp.dot(h_hi, wrc_ref[...], preferred_element_type=F32)
    lw = jnp.dot(h_lo, wrh_ref[...], preferred_element_type=F32)
    lt = jnp.transpose(hw[:, :NRL] + hw[:, NRL:] + lw + rb_ref[...])
    row8 = lax.broadcasted_iota(I32, (8, TM), 0)
    gl = jnp.where(row8 < N_GROUPS, lt[0:8], -jnp.inf)
    ge = jnp.exp(gl - jnp.max(gl, axis=0, keepdims=True))
    gprob = ge / jnp.sum(ge, axis=0, keepdims=True)
    g_p = jnp.max(gprob, axis=0, keepdims=True)
    g_idx = jnp.min(jnp.where(gprob == g_p, row8, 8), axis=0, keepdims=True)
    e_sel = lt[ER0:ER0 + 8]
    for g in range(1, N_GROUPS):
        e_sel = jnp.where(g_idx == g, lt[ER0 + 8 * g:ER0 + 8 * g + 8], e_sel)
    ee = jnp.exp(e_sel - jnp.max(e_sel, axis=0, keepdims=True))
    eprob = ee / jnp.sum(ee, axis=0, keepdims=True)
    p0 = jnp.max(eprob, axis=0, keepdims=True)
    i0 = jnp.min(jnp.where(eprob == p0, row8, 8), axis=0, keepdims=True)
    rest = jnp.where(row8 == i0, -1.0, eprob)
    p1 = jnp.max(rest, axis=0, keepdims=True)
    i1 = jnp.min(jnp.where(rest == p1, row8, 8), axis=0, keepdims=True)
    psum = p0 + p1
    w0 = g_p * p0 / psum
    w1 = g_p * p1 / psum
    id0 = g_idx * EXP_PER_GROUP + i0
    id1 = g_idx * EXP_PER_GROUP + i1

    ri_ref[...] = jnp.where(row8 == 0, id0, jnp.where(row8 == 1, id1, 0))
    rw_ref[...] = jnp.where(row8 == 0, w0, jnp.where(row8 == 1, w1, 0.0))


def _ffn_in_specs():
    const2 = lambda i: (0, 0)
    return [
        pl.BlockSpec((1, D), const2),
        pl.BlockSpec((D, 2 * NRL), const2),
        pl.BlockSpec((D, NRL), const2),
        pl.BlockSpec((1, NRL), const2),
    ]


def _ffn_out_shapes():
    return [
        jax.ShapeDtypeStruct((T, D), F32),
        jax.ShapeDtypeStruct((T * NS, 128), F32),
        jax.ShapeDtypeStruct((NT, 8, TM), I32),
        jax.ShapeDtypeStruct((NT, 8, TM), F32),
    ]


def _ffn_out_specs():
    return [
        pl.BlockSpec((TM, D), lambda i: (i, 0)),
        pl.BlockSpec((TM * NS, 128), lambda i: (i, 0)),
        pl.BlockSpec((None, 8, TM), lambda i: (i, 0, 0)),
        pl.BlockSpec((None, 8, TM), lambda i: (i, 0, 0)),
    ]


def _mod_kernel(cs_ref, w_ref, b_ref, o_ref):
    s = cs_ref[...]
    a = s * jax.nn.sigmoid(s)
    o_ref[...] = jnp.dot(a, w_ref[...], precision=_HI, preferred_element_type=F32) + b_ref[...]


def _modulation(cs, w_ada, b_ada):
    nb = 6
    return pl.pallas_call(
        _mod_kernel,
        out_shape=jax.ShapeDtypeStruct((DEPTH, 8, 6 * D), F32),
        grid=(DEPTH, nb),
        in_specs=[
            pl.BlockSpec((8, D), lambda l, n: (0, 0)),
            pl.BlockSpec((None, D, D), lambda l, n: (l, 0, n)),
            pl.BlockSpec((None, 1, D), lambda l, n: (l, 0, n)),
        ],
        out_specs=pl.BlockSpec((None, 8, D), lambda l, n: (l, 0, n)),
        compiler_params=_cparams(2, 32),
        name="modulation",
    )(cs, w_ada, b_ada.reshape(DEPTH, 1, 6 * D))


def _halo_specs(width):
    nb = T // HALO
    per = TM // HALO
    return [
        pl.BlockSpec((TM, width), lambda i: (i, 0)),
        pl.BlockSpec((HALO, width), lambda i: (jnp.maximum(i * per - 1, 0), 0)),
        pl.BlockSpec((HALO, width), lambda i: (jnp.minimum((i + 1) * per, nb - 1), 0)),
    ]


def _normed_halo(xt, xp, xn, i, mod_ref, nm_ref):
    is_ctx, j, mrow = _tile_info(i)
    g = nm_ref[...]
    sc = 1.0 + _mod(mod_ref, mrow, 1)
    sh = _mod(mod_ref, mrow, 0)
    pv = jnp.where(jnp.logical_and(jnp.logical_not(is_ctx), j > 0), 1.0, 0.0)
    nv = jnp.where(jnp.logical_and(jnp.logical_not(is_ctx), j < LT - 1), 1.0, 0.0)
    ht = _rms(xt, g) * sc + sh
    hp = (_rms(xp, g) * sc + sh) * pv
    hn = (_rms(xn, g) * sc + sh) * nv
    return ht, hp, hn, is_ctx, j, mrow


def _pool_kernel(split, *refs):
    if split:
        xc_ref, x_ref, xp_ref, xn_ref = refs[:4]
        refs = refs[4:]
    else:
        x_ref, xp_ref, xn_ref = refs[:3]
        refs = refs[3:]
    (mod_ref, nm_ref, band_ref, wp_ref, ps_ref, nf_ref, wrc_ref, wrh_ref, rb_ref,
     x1_ref, h2u_ref, ri_ref, rw_ref) = refs
    i = pl.program_id(0)
    x = x_ref[...]
    if split:
        x = jnp.where(i < NCT, xc_ref[...], x)
    ht, hp, hn, is_ctx, j, mrow = _normed_halo(x, xp_ref[...], xn_ref[...], i, mod_ref, nm_ref)
    hext = jnp.concatenate([ht, hp, hn, jnp.zeros((KB - TM - 2 * HALO, D), F32)], axis=0)
    e_hi = hext.astype(BF16)
    e_lo = (hext - e_hi.astype(F32)).astype(BF16)
    seq_len = jnp.where(is_ctx, CTX_LEN, LAT_LEN)
    t = j * TM + lax.broadcasted_iota(I32, (TM, POOL_CH), 0)
    outs = []
    for g, win in enumerate(POOL_WINDOWS):
        lo = win // 2
        hi = win - 1 - lo
        cols = slice(g * POOL_CH, (g + 1) * POOL_CH)
        band = band_ref[g]
        s = (jnp.dot(band, e_hi[:, cols], preferred_element_type=F32)
             + jnp.dot(band, e_lo[:, cols], preferred_element_type=F32))
        cnt = (jnp.minimum(t + hi, seq_len - 1) - jnp.maximum(t - lo, 0) + 1).astype(F32)
        d = s / cnt - ht[:, cols]
        outs.append(jnp.dot(d.astype(BF16), wp_ref[g], preferred_element_type=F32))
    o = jnp.concatenate(outs, axis=1) * ps_ref[...]
    x1 = x + _mod(mod_ref, mrow, 2) * o
    x1_ref[...] = x1
    _ffn_pre(x1, mrow, mod_ref, nf_ref, wrc_ref, wrh_ref, rb_ref, h2u_ref, ri_ref, rw_ref)


def _pool_band():
    band = np.zeros((len(POOL_WINDOWS), TM, KB), np.float32)
    r = np.arange(TM)[:, None]
    for g, win in enumerate(POOL_WINDOWS):
        lo = win // 2
        hi = win - 1 - lo
        pos = np.concatenate([np.arange(TM), np.arange(-HALO, 0), np.arange(TM, TM + HALO)])[None, :]
        band[g, :, :TM + 2 * HALO] = (pos >= r - lo) & (pos <= r + hi)
    return jnp.asarray(band, BF16)


def _split_halo_specs():
    nb = T_LAT // HALO
    per = TM // HALO
    lat = lambda i: jnp.maximum(i - NCT, 0)
    return [
        pl.BlockSpec((TM, D), lambda i: (jnp.minimum(i, NCT - 1), 0)),
        pl.BlockSpec((TM, D), lambda i: (lat(i), 0)),
        pl.BlockSpec((HALO, D), lambda i: (jnp.maximum(lat(i) * per - 1, 0), 0)),
        pl.BlockSpec((HALO, D), lambda i: (jnp.minimum((lat(i) + 1) * per, nb - 1), 0)),
    ]


def _pool_layer(xs, mod_l, nm, wp, ps, ffn_args):
    const2 = lambda i: (0, 0)
    const3 = lambda i: (0, 0, 0)
    split = isinstance(xs, tuple)
    x_args = (xs[0], xs[1], xs[1], xs[1]) if split else (xs, xs, xs)
    return pl.pallas_call(
        functools.partial(_pool_kernel, split),
        out_shape=_ffn_out_shapes(),
        grid=(NT,),
        in_specs=(_split_halo_specs() if split else _halo_specs(D)) + [
            pl.BlockSpec((8, 6 * D), const2),
            pl.BlockSpec((1, D), const2),
            pl.BlockSpec((len(POOL_WINDOWS), TM, KB), const3),
            pl.BlockSpec((len(POOL_WINDOWS), POOL_CH, POOL_CH), const3),
            pl.BlockSpec((1, D), const2),
        ] + _ffn_in_specs(),
        out_specs=_ffn_out_specs(),
        compiler_params=_cparams(1, 48),
        name="pool_mixer",
    )(*x_args, mod_l, nm, _pool_band(), wp.astype(BF16), ps, *ffn_args)


def _conv_kernel(x_ref, xp_ref, xn_ref, mod_ref, nm_ref, win_ref, cw_ref, wout_ref,
                 nf_ref, wrc_ref, wrh_ref, rb_ref,
                 x1_ref, h2u_ref, ri_ref, rw_ref):
    i = pl.program_id(0)
    x = x_ref[...]
    ht, hp, hn, is_ctx, j, mrow = _normed_halo(x, xp_ref[...], xn_ref[...], i, mod_ref, nm_ref)
    hext = jnp.concatenate([ht, hp, hn], axis=0).astype(BF16)
    bcv = jnp.dot(hext, win_ref[...], preferred_element_type=F32)
    b = bcv[:TM, :D]
    u = bcv[:, D:2 * D] * bcv[:, 2 * D:]
    um = u[:TM]
    u_before = u[TM + HALO - 1:TM + HALO]
    u_after = u[TM + HALO:TM + HALO + 1]
    row = lax.broadcasted_iota(I32, (TM, D), 0)
    up = jnp.where(row == 0, u_before, pltpu.roll(um, 1, 0))
    un = jnp.where(row == TM - 1, u_after, pltpu.roll(um, TM - 1, 0))
    cw = cw_ref[...]
    conv = up * cw[0:1] + um * cw[1:2] + un * cw[2:3]
    o = jnp.dot((b * conv).astype(BF16), wout_ref[...], preferred_element_type=F32)
    x1 = x + _mod(mod_ref, mrow, 2) * o
    x1_ref[...] = x1
    _ffn_pre(x1, mrow, mod_ref, nf_ref, wrc_ref, wrh_ref, rb_ref, h2u_ref, ri_ref, rw_ref)


def _conv_layer(x, mod_l, nm, w_in, cw, w_out, ffn_args):
    const2 = lambda i: (0, 0)
    cw8 = jnp.concatenate([cw, jnp.zeros((8 - cw.shape[0], D), F32)], axis=0)
    return pl.pallas_call(
        _conv_kernel,
        out_shape=_ffn_out_shapes(),
        grid=(NT,),
        in_specs=_halo_specs(D) + [
            pl.BlockSpec((8, 6 * D), const2),
            pl.BlockSpec((1, D), const2),
            pl.BlockSpec((D, 3 * D), const2),
            pl.BlockSpec((8, D), const2),
            pl.BlockSpec((D, D), const2),
        ] + _ffn_in_specs(),
        out_specs=_ffn_out_specs(),
        compiler_params=_cparams(1, 56),
        name="conv_mixer",
    )(x, x, x, mod_l, nm, w_in.astype(BF16), cw8, w_out.astype(BF16), *ffn_args)


W_CAT = Q_RANK + KV_RANK + 2 * QK_ROPE
QH = 2 * QK_NOPE


def _mla_proj_kernel(x_ref, mod_ref, nm_ref, wcat_ref, qn_ref, kvn_ref, wq_ref, wqs_ref,
                     c_ref, s_ref, q_ref, ckv_ref, kpe_ref):
    i = pl.program_id(0)
    _, _, mrow = _tile_info(i)
    h = _rms(x_ref[...], nm_ref[...]) * (1.0 + _mod(mod_ref, mrow, 1)) + _mod(mod_ref, mrow, 0)
    y = jnp.dot(h.astype(BF16), wcat_ref[...], preferred_element_type=F32)
    cqn = _rms(y[:, :Q_RANK], qn_ref[...]).astype(BF16)
    q = jnp.dot(cqn, wq_ref[...], preferred_element_type=F32)
    qs = jnp.dot(cqn, wqs_ref[...], preferred_element_type=F32)
    cos = c_ref[...]
    sin = s_ref[...]
    parts = []
    for hh in range(HEADS):
        parts.append(q[:, hh * QH:hh * QH + QK_NOPE])
        parts.append(q[:, hh * QH + QK_NOPE:(hh + 1) * QH] * cos + qs[:, hh * 128:(hh + 1) * 128] * sin)
    q_ref[...] = (jnp.concatenate(parts, axis=1) * (SM_SCALE * LOG2E)).astype(BF16)
    ckv_ref[...] = _rms(y[:, Q_RANK:Q_RANK + KV_RANK], kvn_ref[...])
    k2 = y[:, Q_RANK + KV_RANK:]
    kr = k2 * cos + pltpu.roll(k2, QK_ROPE, 1) * sin
    lane = lax.broadcasted_iota(I32, (TM, 2 * QK_ROPE), 1)
    kpe_ref[...] = jnp.where(lane < QK_ROPE, kr, 0.0)


def _rope_tables():
    rows_n = LAT_LEN // GRID_W
    rows = jnp.repeat(jnp.arange(rows_n), GRID_W).astype(F32)
    cols = jnp.tile(jnp.arange(GRID_W), rows_n).astype(F32)
    inv = ROPE_BASE ** (-(jnp.arange(AXIS_FREQS, dtype=F32) / AXIS_FREQS))
    ang = jnp.stack([rows[:, None] * inv, cols[:, None] * inv], axis=1)
    cos, sin = jnp.cos(ang), jnp.sin(ang)
    c64 = jnp.concatenate([cos[:, 0], cos[:, 0], cos[:, 1], cos[:, 1]], axis=1)
    s64 = jnp.concatenate([-sin[:, 0], sin[:, 0], -sin[:, 1], sin[:, 1]], axis=1)
    c = jnp.concatenate([c64, jnp.ones((LAT_LEN, QK_ROPE), F32)], axis=1)
    s = jnp.concatenate([s64, jnp.zeros((LAT_LEN, QK_ROPE), F32)], axis=1)
    c = jnp.concatenate([c, jnp.ones((TM, 2 * QK_ROPE), F32)], axis=0)
    s = jnp.concatenate([s, jnp.zeros((TM, 2 * QK_ROPE), F32)], axis=0)
    return c, s


def _swap_rope_cols(w):
    f = AXIS_FREQS
    return jnp.concatenate([w[..., f:2 * f], w[..., :f], w[..., 3 * f:], w[..., 2 * f:3 * f]], axis=-1)


def _mla_proj(x, mod_l, nm, w_dq, q_norm, w_uq, w_dkv, kv_norm):
    const2 = lambda i: (0, 0)
    w_kpe = w_dkv[:, KV_RANK:]
    wcat = jnp.concatenate([w_dq, w_dkv[:, :KV_RANK], w_kpe, _swap_rope_cols(w_kpe)], axis=1).astype(BF16)
    wq = w_uq.reshape(Q_RANK, HEADS, QK_NOPE + QK_ROPE)
    zpad = jnp.zeros((Q_RANK, HEADS, QK_ROPE), F32)
    wq_a = jnp.concatenate([wq, zpad], axis=2).reshape(Q_RANK, HEADS * QH).astype(BF16)
    wq_s = jnp.concatenate([_swap_rope_cols(wq[:, :, QK_NOPE:]), zpad], axis=2)
    wq_s = wq_s.reshape(Q_RANK, HEADS * 128).astype(BF16)
    cos, sin = _rope_tables()
    tab_idx = lambda i: (jnp.where(i < NCT, LT, jnp.maximum(i - NCT, 0) & (LT - 1)), 0)
    return pl.pallas_call(
        _mla_proj_kernel,
        out_shape=[jax.ShapeDtypeStruct((T, HEADS * QH), BF16),
                   jax.ShapeDtypeStruct((T, KV_RANK), F32),
                   jax.ShapeDtypeStruct((T, 2 * QK_ROPE), F32)],
        grid=(NT,),
        in_specs=[
            pl.BlockSpec((TM, D), lambda i: (i, 0)),
            pl.BlockSpec((8, 6 * D), const2),
            pl.BlockSpec((1, D), const2),
            pl.BlockSpec((D, W_CAT), const2),
            pl.BlockSpec((1, Q_RANK), const2),
            pl.BlockSpec((1, KV_RANK), const2),
            pl.BlockSpec((Q_RANK, HEADS * QH), const2),
            pl.BlockSpec((Q_RANK, HEADS * 128), const2),
            pl.BlockSpec((TM, 2 * QK_ROPE), tab_idx),
            pl.BlockSpec((TM, 2 * QK_ROPE), tab_idx),
        ],
        out_specs=[pl.BlockSpec((TM, HEADS * QH), lambda i: (i, 0)),
                   pl.BlockSpec((TM, KV_RANK), lambda i: (i, 0)),
                   pl.BlockSpec((TM, 2 * QK_ROPE), lambda i: (i, 0))],
        compiler_params=_cparams(1, 48),
        name="mla_proj",
    )(x, mod_l, nm, wcat, q_norm, kv_norm, wq_a, wq_s, cos, sin)


def _kv_expand_kernel(ckv_ref, kpe_ref, wk_ref, wv_ref, k_ref, v_ref):
    c = ckv_ref[...].astype(BF16)
    kn = jnp.dot(c, wk_ref[...], preferred_element_type=F32).astype(BF16)
    kp = kpe_ref[...].astype(BF16)
    parts = []
    for hh in range(HEADS):
        parts.append(kn[:, hh * QK_NOPE:(hh + 1) * QK_NOPE])
        parts.append(kp)
    k_ref[...] = jnp.concatenate(parts, axis=1)
    v_ref[...] = jnp.dot(c, wv_ref[...], preferred_element_type=F32).astype(BF16)


def _kv_expand(ckv, kpe, wk, wv, name):
    n = ckv.shape[0]
    const2 = lambda i: (0, 0)
    return pl.pallas_call(
        _kv_expand_kernel,
        out_shape=[jax.ShapeDtypeStruct((n, HEADS * QH), BF16),
                   jax.ShapeDtypeStruct((n, HEADS * V_DIM), BF16)],
        grid=(n // TM,),
        in_specs=[
            pl.BlockSpec((TM, KV_RANK), lambda i: (i, 0)),
            pl.BlockSpec((TM, 2 * QK_ROPE), lambda i: (i, 0)),
            pl.BlockSpec((KV_RANK, HEADS * QK_NOPE), const2),
            pl.BlockSpec((KV_RANK, HEADS * V_DIM), const2),
        ],
        out_specs=[pl.BlockSpec((TM, HEADS * QH), lambda i: (i, 0)),
                   pl.BlockSpec((TM, HEADS * V_DIM), lambda i: (i, 0))],
        compiler_params=_cparams(1, 32),
        name=name,
    )(ckv, kpe, wk, wv)


def _attend(q, k, v):
    s = _dot_nt(q, k)
    p = jnp.exp2(s - jnp.max(s, axis=1, keepdims=True))
    l = jnp.sum(p, axis=1, keepdims=True)
    return jnp.dot(p.astype(BF16), v, preferred_element_type=F32) / l


KC = 512
NKC = NK_LAT // KC


def _attn_lat_kernel(q_ref, k_ref, v_ref, o_ref, s_ref, m_ref, l_ref, acc_ref):
    q = q_ref[...]
    m_ref[...] = jnp.full((TQ, 128), -jnp.inf, F32)

    def scores(c, carry):
        k = k_ref[pl.ds(pl.multiple_of(c * KC, KC), KC), :]
        s = _dot_nt(q, k)
        s_ref[c] = s
        m = m_ref[...]
        for j in range(KC // 128):
            m = jnp.maximum(m, s[:, j * 128:(j + 1) * 128])
        m_ref[...] = m
        return carry
    lax.fori_loop(0, NKC, scores, 0, unroll=True)

    mb = jnp.broadcast_to(jnp.max(m_ref[...], axis=1, keepdims=True), (TQ, 128))
    l_ref[...] = jnp.zeros((TQ, 128), F32)
    acc_ref[...] = jnp.zeros((TQ, V_DIM), F32)

    def weighted(c, carry):
        s = s_ref[c]
        ps = [jnp.exp2(s[:, j * 128:(j + 1) * 128] - mb) for j in range(KC // 128)]
        l = l_ref[...]
        for pj in ps:
            l = l + pj
        l_ref[...] = l
        p = jnp.concatenate(ps, axis=1).astype(BF16)
        v = v_ref[pl.ds(pl.multiple_of(c * KC, KC), KC), :]
        acc_ref[...] += jnp.dot(p, v, preferred_element_type=F32)
        return carry
    lax.fori_loop(0, NKC, weighted, 0, unroll=3)

    o_ref[...] = (acc_ref[...] / jnp.sum(l_ref[...], axis=1, keepdims=True)).astype(BF16)


def _attn_lat(q, k, v):
    nq = LAT_LEN // TQ
    q0 = T_CTX // TQ
    return pl.pallas_call(
        _attn_lat_kernel,
        scratch_shapes=[pltpu.VMEM((NKC, TQ, KC), F32), pltpu.VMEM((TQ, 128), F32),
                        pltpu.VMEM((TQ, 128), F32), pltpu.VMEM((TQ, V_DIM), F32)],
        out_shape=jax.ShapeDtypeStruct((T_LAT, HEADS * V_DIM), BF16),
        grid=(N_LAT_SEQ, HEADS, nq),
        in_specs=[
            pl.BlockSpec((TQ, QH), lambda b, h, t: (q0 + b * nq + t, h)),
            pl.BlockSpec((NK_LAT, QH), lambda b, h, t: (b, h)),
            pl.BlockSpec((NK_LAT, V_DIM), lambda b, h, t: (b, h)),
        ],
        out_specs=pl.BlockSpec((TQ, V_DIM), lambda b, h, t: (b * nq + t, h)),
        compiler_params=_cparams(3, 48),
        name="attn_latent",
    )(q, k, v)


def _attn_ctx_kernel(q_ref, k_ref, v_ref, o_ref):
    outs = []
    for hh in range(HEADS):
        outs.append(_attend(q_ref[:, hh * QH:(hh + 1) * QH], k_ref[:, hh * QH:(hh + 1) * QH],
                            v_ref[:, hh * V_DIM:(hh + 1) * V_DIM]))
    o_ref[...] = jnp.concatenate(outs, axis=1).astype(BF16)


def _attn_ctx(q, k, v):
    return pl.pallas_call(
        _attn_ctx_kernel,
        out_shape=jax.ShapeDtypeStruct((T_CTX, HEADS * V_DIM), BF16),
        grid=(N_CTX_SEQ,),
        in_specs=[
            pl.BlockSpec((CTX_LEN, HEADS * QH), lambda b: (b, 0)),
            pl.BlockSpec((CTX_LEN, HEADS * QH), lambda b: (b, 0)),
            pl.BlockSpec((CTX_LEN, HEADS * V_DIM), lambda b: (b, 0)),
        ],
        out_specs=pl.BlockSpec((CTX_LEN, HEADS * V_DIM), lambda b: (b, 0)),
        compiler_params=_cparams(1, 32),
        name="attn_context",
    )(q, k, v)


def _attn_out_kernel(x_ref, oc_ref, ol_ref, mod_ref, wo_ref,
                     nf_ref, wrc_ref, wrh_ref, rb_ref,
                     x1_ref, h2u_ref, ri_ref, rw_ref):
    i = pl.program_id(0)
    is_ctx, _, mrow = _tile_info(i)
    att = jnp.where(is_ctx, oc_ref[...], ol_ref[...])
    o = jnp.dot(att, wo_ref[...], preferred_element_type=F32)
    x1 = x_ref[...] + _mod(mod_ref, mrow, 2) * o
    x1_ref[...] = x1
    _ffn_pre(x1, mrow, mod_ref, nf_ref, wrc_ref, wrh_ref, rb_ref, h2u_ref, ri_ref, rw_ref)


def _attn_out(x, o_ctx, o_lat, mod_l, w_o, ffn_args):
    const2 = lambda i: (0, 0)
    return pl.pallas_call(
        _attn_out_kernel,
        out_shape=_ffn_out_shapes(),
        grid=(NT,),
        in_specs=[
            pl.BlockSpec((TM, D), lambda i: (i, 0)),
            pl.BlockSpec((TM, D), lambda i: (jnp.minimum(i, NCT - 1), 0)),
            pl.BlockSpec((TM, D), lambda i: (jnp.maximum(i - NCT, 0), 0)),
            pl.BlockSpec((8, 6 * D), const2),
            pl.BlockSpec((D, D), const2),
        ] + _ffn_in_specs(),
        out_specs=_ffn_out_specs(),
        compiler_params=_cparams(1, 48),
        name="attn_out",
    )(x, o_ctx, o_lat, mod_l, w_o.astype(BF16), *ffn_args)


def _mla_layer(x, mod_l, nm, cache_ckv, cache_kpe, w_dq, q_norm, w_uq, w_dkv, kv_norm, w_ukv, w_o, ffn_args):
    q, ckv, kpe = _mla_proj(x, mod_l, nm, w_dq, q_norm, w_uq, w_dkv, kv_norm)
    wkv = w_ukv.reshape(KV_RANK, HEADS, QK_NOPE + V_DIM)
    wk = wkv[:, :, :QK_NOPE].reshape(KV_RANK, HEADS * QK_NOPE).astype(BF16)
    wv = wkv[:, :, QK_NOPE:].reshape(KV_RANK, HEADS * V_DIM).astype(BF16)
    k_c, v_c = _kv_expand(ckv[:T_CTX], kpe[:T_CTX], wk, wv, "kv_expand_context")
    kpe_cache = jnp.concatenate([cache_kpe, jnp.zeros_like(cache_kpe)], axis=-1)
    ckv_l = jnp.concatenate([cache_ckv, ckv[T_CTX:].reshape(N_LAT_SEQ, LAT_LEN, KV_RANK)], axis=1)
    kpe_l = jnp.concatenate([kpe_cache, kpe[T_CTX:].reshape(N_LAT_SEQ, LAT_LEN, 2 * QK_ROPE)], axis=1)
    k_l, v_l = _kv_expand(ckv_l.reshape(N_LAT_SEQ * NK_LAT, KV_RANK),
                          kpe_l.reshape(N_LAT_SEQ * NK_LAT, 2 * QK_ROPE), wk, wv, "kv_expand_latent")
    o_c = _attn_ctx(q, k_c, v_c)
    o_l = _attn_lat(q, k_l, v_l)
    outs = _attn_out(x, o_c, o_l, mod_l, w_o, ffn_args)
    new_ckv = ckv[:T_CTX].reshape(N_CTX_SEQ, 1, CTX_LEN, KV_RANK)
    new_kpe = kpe[:T_CTX, :QK_ROPE].reshape(N_CTX_SEQ, 1, CTX_LEN, QK_ROPE)
    return outs, new_ckv, new_kpe


GA = 2
NIB = GA + 2
SRC_BITS = 16
assert T <= 1 << SRC_BITS and 2 * TP <= 1 << (32 - SRC_BITS)
ROW_STEP = (1 << SRC_BITS) + 1


def _slot_kernel(meta_ref, psv_ref, ri_ref, sv_ref, stage, *rest):
    bufs, sem = rest[:-1], rest[-1]
    i = pl.program_id(0)

    def row_copies(b):
        return [pltpu.make_async_copy(stage.at[b, row], bufs[b * 2 * LC + row], sem.at[b])
                for row in range(2 * LC)]

    @pl.when(i == 0)
    def _():
        def fill(p, c):
            tile_parity = lax.shift_right_logical(p, jnp.int32(TS.bit_length() - 1)) & 1
            sv_ref[p] = (T + tile_parity * TS + (p & (TS - 1))) * (1 << SRC_BITS)
            return c

        def fill_expert(e, c):
            return lax.fori_loop(meta_ref[1, e], meta_ref[2, e], fill, c)
        lax.fori_loop(0, N_EXPERTS, fill_expert, 0)
        lax.fori_loop(meta_ref[2, N_EXPERTS - 1], CAP, fill, 0)

    @pl.when(i < NT)
    def _():
        ri = ri_ref[...]
        rowe = lax.broadcasted_iota(I32, (N_EXPERTS, TM), 0)
        ps = psv_ref[...]
        pos0 = jnp.sum(jnp.where(rowe == ri[0:1], ps, 0), axis=0, keepdims=True) + ri[2:3]
        pos1 = jnp.sum(jnp.where(rowe == ri[1:2], ps, 0), axis=0, keepdims=True) + ri[3:4]
        row8 = lax.broadcasted_iota(I32, (8, 128), 0)
        st = jnp.zeros((8, 128), I32)
        for k, pos in enumerate((pos0, pos1)):
            for h in range(LC):
                st = jnp.where(row8 == k * LC + h, pos[:, h * 128:(h + 1) * 128], st)
        for b in range(2):
            @pl.when(i % 2 == b)
            def _(b=b):
                stage[b] = st
                for c in row_copies(b):
                    c.start()

    j = i - 1
    for b in range(2):
        @pl.when(jnp.logical_and(i >= 1, j % 2 == b))
        def _(b=b):
            for c in row_copies(b):
                c.wait()
            for h in range(LC):
                def tok(t, v, h=h):
                    sv_ref[bufs[b * 2 * LC + h][t]] = v
                    sv_ref[bufs[b * 2 * LC + LC + h][t]] = v + (TP << SRC_BITS)
                    return v + ROW_STEP
                lax.fori_loop(0, 128, tok, (j * TM + h * 128) * ROW_STEP, unroll=8)


def _rank_kernel(ri_ref, trio_ref, ro_ref, cnt_ref, carry_ref):
    @pl.when(pl.program_id(0) == 0)
    def _():
        carry_ref[...] = jnp.zeros_like(carry_ref)

    rowe = lax.broadcasted_iota(I32, (N_EXPERTS, TM), 0)
    row8 = lax.broadcasted_iota(I32, (8, TM), 0)
    ids = [ri_ref[tt] for tt in range(RG)]
    ohs = [rowe == ids[tt][k:k + 1] for tt in range(RG) for k in range(2)]
    ohb = jnp.concatenate([jnp.where(oh, 1.0, 0.0).astype(BF16) for oh in ohs], axis=0)
    pt = jnp.dot(ohb, trio_ref[...], preferred_element_type=F32)
    carry = carry_ref[...]
    for tt in range(RG):
        ranks = []
        for k in range(2):
            r0 = (tt * 2 + k) * N_EXPERTS
            before = carry + pt[r0:r0 + N_EXPERTS, :TM]
            ranks.append(jnp.sum(jnp.where(ohs[tt * 2 + k], before, 0.0), axis=0, keepdims=True).astype(I32))
            carry = carry + pt[r0:r0 + N_EXPERTS, TM:]
        ro_ref[tt] = jnp.where(row8 == 0, ids[tt][0:1],
                     jnp.where(row8 == 1, ids[tt][1:2],
                     jnp.where(row8 == 2, ranks[0],
                     jnp.where(row8 == 3, ranks[1], 0))))
    carry_ref[...] = carry
    cnt_ref[...] = carry


def _rank(route_i):
    tri = np.triu(np.ones((TM, TM), np.float32), 1)
    trio = jnp.asarray(np.concatenate([tri, np.ones((TM, TM), np.float32)], axis=1), BF16)
    return pl.pallas_call(
        _rank_kernel,
        out_shape=[jax.ShapeDtypeStruct((NT, 8, TM), I32), jax.ShapeDtypeStruct((N_EXPERTS, TM), F32)],
        grid=(NT // RG,),
        in_specs=[pl.BlockSpec((RG, 8, TM), lambda g: (g, 0, 0)),
                  pl.BlockSpec((TM, 2 * TM), lambda g: (0, 0))],
        out_specs=[pl.BlockSpec((RG, 8, TM), lambda g: (g, 0, 0)),
                   pl.BlockSpec((N_EXPERTS, TM), lambda g: (0, 0))],
        scratch_shapes=[pltpu.VMEM((N_EXPERTS, TM), F32)],
        compiler_params=_cparams(1, 16),
        name="expert_rank",
    )(route_i, trio)


def _slot_map(meta, pad_starts, route_i):
    return pl.pallas_call(
        _slot_kernel,
        out_shape=jax.ShapeDtypeStruct((CAP,), I32),
        grid=(NT + 1,),
        in_specs=[pl.BlockSpec(memory_space=pltpu.SMEM),
                  pl.BlockSpec((N_EXPERTS, TM), lambda i: (0, 0)),
                  pl.BlockSpec((None, 8, TM), lambda i: (jnp.minimum(i, NT - 1), 0, 0))],
        out_specs=pl.BlockSpec(memory_space=pltpu.SMEM),
        scratch_shapes=([pltpu.VMEM((2, 8, 128), I32)] + [pltpu.SMEM((128,), I32)] * (4 * LC)
                        + [pltpu.SemaphoreType.DMA((2,))]),
        compiler_params=_cparams(1, 16),
        name="slot_map",
    )(meta, jnp.broadcast_to(pad_starts[:, None], (N_EXPERTS, TM)), route_i)


def _moe_kernel(blk_ref, nu_ref, sv_ref, h2u_ref, w1_ref, w3_ref, w2_ref, o_ref,
                idx_ref, xbuf, obuf, wb1, wb3, wb2, pe_ref, isem, gsem, ssem):
    i = pl.program_id(0)
    n_used = nu_ref[0]
    slot = i % 2

    def idx_copy(tile):
        return pltpu.make_async_copy(sv_ref.at[tile], idx_ref.at[tile % NIB], isem.at[tile % NIB])

    def start_gather(tile):
        b = tile % NIB
        sl = tile % (GA + 1)
        for r in range(TS):
            src = idx_ref[b, 0, r] & ((1 << SRC_BITS) - 1)
            pltpu.make_async_copy(h2u_ref.at[pl.ds(pl.multiple_of(src * NS, NS), NS)],
                                  xbuf.at[sl, pl.ds(r * NS, NS)], gsem.at[sl]).start(priority=r % 2)

    def wait_gather(sl):
        pltpu.make_async_copy(h2u_ref.at[pl.ds(0, TS * NS)], xbuf.at[sl], gsem.at[sl]).wait()

    def wait_scatter(sl):
        pltpu.make_async_copy(obuf.at[sl], o_ref.at[pl.ds(0, TS * NS)], ssem.at[sl]).wait()

    @pl.when(i == 0)
    def _():
        pe_ref[0] = -1
        obuf[0] = jnp.zeros((TS * NS, 128), F32)
        fills = [pltpu.make_async_copy(obuf.at[0], o_ref.at[pl.ds((k * TP + T + h * TS) * NS, TS * NS)], ssem.at[0])
                 for k in range(2) for h in range(2)]
        for f in fills:
            f.start()
        for f in fills:
            f.wait()
        for t0 in range(GA):
            @pl.when(t0 < n_used)
            def _(t0=t0):
                c0 = idx_copy(t0)
                c0.start()
                c0.wait()
                start_gather(t0)

        @pl.when(GA < n_used)
        def _():
            idx_copy(GA).start()

    @pl.when(i < n_used)
    def _():
        @pl.when(i + GA < n_used)
        def _():
            idx_copy(i + GA).wait()
            start_gather(i + GA)

        @pl.when(i + GA + 1 < n_used)
        def _():
            idx_copy(i + GA + 1).start()

        e = blk_ref[i]

        @pl.when(e != pe_ref[0])
        def _():
            wb1[...] = w1_ref[...].astype(BF16)
            wb3[...] = w3_ref[...].astype(BF16)
            wb2[...] = w2_ref[...].astype(BF16)
            pe_ref[0] = e

        gslot = i % (GA + 1)
        wait_gather(gslot)
        xb = _load_slabs(xbuf.at[gslot], TS).astype(BF16)
        a = jnp.dot(xb, wb1[...], preferred_element_type=F32)
        b = jnp.dot(xb, wb3[...], preferred_element_type=F32)
        hm = (a * jax.nn.sigmoid(a) * b).astype(BF16)
        out = jnp.dot(hm, wb2[...], preferred_element_type=F32)

        @pl.when(i >= 2)
        def _():
            wait_scatter(slot)

        _store_slabs(obuf.at[slot], out)
        bi = i % NIB
        for r in range(TS):
            dst = lax.shift_right_logical(idx_ref[bi, 0, r], jnp.int32(SRC_BITS))
            pltpu.make_async_copy(obuf.at[slot, pl.ds(r * NS, NS)],
                                  o_ref.at[pl.ds(pl.multiple_of(dst * NS, NS), NS)],
                                  ssem.at[slot]).start(priority=r % 2)

        @pl.when(i == n_used - 1)
        def _():
            wait_scatter(slot)

            @pl.when(i >= 1)
            def _():
                wait_scatter(1 - slot)


def _moe_experts(l, blk_e, n_used, slot_val, h2u, w1, w3, w2):
    grid_spec = pltpu.PrefetchScalarGridSpec(
        num_scalar_prefetch=2,
        grid=(NST,),
        in_specs=[
            pl.BlockSpec(memory_space=pl.ANY),
            pl.BlockSpec(memory_space=pl.ANY),
            pl.BlockSpec((None, None, D, D_EXPERT), lambda i, blk, nu: (l, blk[i], 0, 0)),
            pl.BlockSpec((None, None, D, D_EXPERT), lambda i, blk, nu: (l, blk[i], 0, 0)),
            pl.BlockSpec((None, None, D_EXPERT, D), lambda i, blk, nu: (l, blk[i], 0, 0)),
        ],
        out_specs=pl.BlockSpec(memory_space=pl.ANY),
        scratch_shapes=[
            pltpu.SMEM((NIB, 1, TS), I32),
            pltpu.VMEM((GA + 1, TS * NS, 128), F32),
            pltpu.VMEM((2, TS * NS, 128), F32),
            pltpu.VMEM((D, D_EXPERT), BF16),
            pltpu.VMEM((D, D_EXPERT), BF16),
            pltpu.VMEM((D_EXPERT, D), BF16),
            pltpu.SMEM((1,), I32),
            pltpu.SemaphoreType.DMA((NIB,)),
            pltpu.SemaphoreType.DMA((GA + 1,)),
            pltpu.SemaphoreType.DMA((2,)),
        ],
    )
    return pl.pallas_call(
        _moe_kernel,
        out_shape=jax.ShapeDtypeStruct((2 * TP * NS, 128), F32),
        grid_spec=grid_spec,
        compiler_params=_cparams(1, 48),
        name="moe_experts",
    )(blk_e, n_used, slot_val.reshape(NST, 1, TS), h2u, w1, w3, w2)


def _lane_to_col(row):
    r = lax.broadcasted_iota(I32, (TM, TM), 0)
    c = lax.broadcasted_iota(I32, (TM, TM), 1)
    return jnp.sum(jnp.where(r == c, row, 0.0), axis=1, keepdims=True)


def _combine_kernel(final, x1_ref, o0_ref, o1_ref, rw_ref, mod_ref, nfin_ref, *y_refs):
    i = pl.program_id(0)
    _, _, mrow = _tile_info(i)
    rw = rw_ref[...]
    w0 = _lane_to_col(rw[0:1])
    w1 = _lane_to_col(rw[1:2])
    g2 = _mod(mod_ref, mrow, 5)
    x2 = x1_ref[...] + g2 * (_load_slabs(o0_ref, TM) * w0 + _load_slabs(o1_ref, TM) * w1)
    if final:
        yc_ref, yl_ref = y_refs
        y = _rms(x2, nfin_ref[...])

        @pl.when(i < NCT)
        def _():
            yc_ref[...] = y

        @pl.when(i >= NCT)
        def _():
            yl_ref[...] = y
    else:
        y_refs[0][...] = x2


def _combine(x1, o_rows, route_w, mod_l, norm_final, final):
    const2 = lambda i: (0, 0)
    if final:
        out_shape = [jax.ShapeDtypeStruct((T_CTX, D), F32), jax.ShapeDtypeStruct((T_LAT, D), F32)]
        out_specs = [pl.BlockSpec((TM, D), lambda i: (jnp.minimum(i, NCT - 1), 0)),
                     pl.BlockSpec((TM, D), lambda i: (jnp.maximum(i - NCT, 0), 0))]
    else:
        out_shape = [jax.ShapeDtypeStruct((T, D), F32)]
        out_specs = [pl.BlockSpec((TM, D), lambda i: (i, 0))]
    return pl.pallas_call(
        functools.partial(_combine_kernel, final),
        out_shape=out_shape,
        grid=(NT,),
        in_specs=[
            pl.BlockSpec((TM, D), lambda i: (i, 0)),
            pl.BlockSpec((TM * NS, 128), lambda i: (i, 0)),
            pl.BlockSpec((TM * NS, 128), lambda i: (TP // TM + i, 0)),
            pl.BlockSpec((None, 8, TM), lambda i: (i, 0, 0)),
            pl.BlockSpec((8, 6 * D), const2),
            pl.BlockSpec((1, D), const2),
        ],
        out_specs=out_specs,
        compiler_params=_cparams(1, 32),
        name="moe_combine",
    )(x1, o_rows, o_rows, route_w, mod_l, norm_final)


def _moe_layer(l, x1, h2u, route_ids, route_w, mod_l, w1, w3, w2, norm_final, final):
    route_i, counts = _rank(route_ids)
    sizes = counts[:, 0].astype(I32)
    padded = (sizes + TS - 1) // TS * TS
    pad_ends = jnp.cumsum(padded)
    pad_starts = pad_ends - padded
    n_used = (pad_ends[-1:] // TS).astype(I32)
    tile_start = jnp.arange(NST, dtype=I32) * TS
    blk_e = jnp.minimum(jnp.sum((pad_ends[None, :] <= tile_start[:, None]).astype(I32), axis=1),
                        N_EXPERTS - 1).astype(I32)
    meta = jnp.stack([pad_starts, pad_starts + sizes, pad_ends]).astype(I32)
    slot_val = _slot_map(meta, pad_starts.astype(I32), route_i)
    o_rows = _moe_experts(l, blk_e, n_used, slot_val, h2u, w1, w3, w2)
    return _combine(x1, o_rows, route_w, mod_l, norm_final, final)


def _routing_params(w_rg, b_rg, w_re, b_re):
    w = jnp.zeros((D, NRL), F32).at[:, :N_GROUPS].set(w_rg).at[:, ER0:ER0 + N_EXPERTS].set(w_re)
    hi = w.astype(BF16)
    lo = (w - hi.astype(F32)).astype(BF16)
    bias = jnp.zeros((1, NRL), F32).at[0, :N_GROUPS].set(b_rg).at[0, ER0:ER0 + N_EXPERTS].set(b_re)
    return jnp.concatenate([hi, lo], axis=1), hi, bias


def kernel(x_prompt, x_sample, cache_ckv, cache_kpe, c, c_ctx, norm_mix, norm_ffn, norm_final, w_ada, b_ada, w_pool, pool_scale, w_conv_in, conv_w, w_conv_out, w_dq, q_norm, w_uq, w_dkv, kv_norm, w_ukv, w_o, w_route_g, b_route_g, w_route_e, b_route_e, w1, w3, w2):
    x = (x_prompt.reshape(T_CTX, D), x_sample.reshape(T_LAT, D))
    cs =jnp.concatenate([c_ctx[None, :], c, jnp.zeros((8 - 1 - N_LAT_SEQ, D), F32)], axis=0)
    mod_all = _modulation(cs, w_ada, b_ada)
    nfin = norm_final[None, :]
    new_ckv = new_kpe = None
    for l in range(DEPTH):
        kind, j = l % 3, l // 3
        mod_l = mod_all[l]
        nm = norm_mix[l][None, :]
        ffn_args = (norm_ffn[l][None, :],) + _routing_params(w_route_g[l], b_route_g[l], w_route_e[l], b_route_e[l])
        if kind == 0:
            outs = _pool_layer(x, mod_l, nm, w_pool[j], pool_scale[j][None, :], ffn_args)
        elif kind == 1:
            outs = _conv_layer(x, mod_l, nm, w_conv_in[j], conv_w[j], w_conv_out[j], ffn_args)
        else:
            outs, new_ckv, new_kpe = _mla_layer(
                x, mod_l, nm, cache_ckv[:, j], cache_kpe[:, j], w_dq[j], q_norm[j][None, :], w_uq[j],
                w_dkv[j], kv_norm[j][None, :], w_ukv[j], w_o[j], ffn_args)
        x1, h2u, route_ids, route_w = outs
        ys = _moe_layer(l, x1, h2u, route_ids, route_w, mod_l, w1, w3, w2, nfin, l == DEPTH - 1)
        x = ys[0]
    y_prompt = ys[0].reshape(N_CTX_SEQ, CTX_LEN, D)
    y_sample = ys[1].reshape(N_LAT_SEQ, LAT_LEN, D)
    return (y_prompt, y_sample, new_ckv, new_kpe)
```

```python
import functools

import numpy as np
import jax
import jax.numpy as jnp
from jax import lax
from jax.experimental import pallas as pl
from jax.experimental.pallas import tpu as pltpu

F32 = jnp.float32
BF16 = jnp.bfloat16
I32 = jnp.int32

D = 1024
N_CTX_SEQ, CTX_LEN = 32, 256
N_LAT_SEQ, LAT_LEN = 4, 4096
PAST = 512
DEPTH = 4
GRID_W = 64
POOL_WINDOWS = (2, 4, 8, 16)
POOL_CH = D // 4
HEADS = 8
QK_NOPE, QK_ROPE, V_DIM = 128, 64, 128
Q_RANK, KV_RANK = 512, 256
AXIS_FREQS = QK_ROPE // 4
ROPE_BASE = 10000.0
SM_SCALE = (QK_NOPE + QK_ROPE) ** -0.5
LOG2E = 1.4426950408889634
N_GROUPS, EXP_PER_GROUP = 4, 8
N_EXPERTS = N_GROUPS * EXP_PER_GROUP
D_EXPERT = 512
EPS = 1e-6

T_CTX = N_CTX_SEQ * CTX_LEN
T_LAT = N_LAT_SEQ * LAT_LEN
T = T_CTX + T_LAT
TM = 256
NT = T // TM
NCT = T_CTX // TM
LT = LAT_LEN // TM
LT_SHIFT = LT.bit_length() - 1
HALO = 8
KB = TM + 128
NRL = 128
ER0 = 16
RG = 4
LC = TM // 128
assert 2 * LC <= 8 and NT % RG == 0
TS = 256
CAP = 2 * T + N_EXPERTS * TS
NST = CAP // TS
NS = D // 128
NK_LAT = PAST + LAT_LEN
TQ = 512
MIB = 1024 * 1024

_HI = lax.Precision.HIGHEST


def _cparams(n_axes, vmem_mib):
    return pltpu.CompilerParams(
        dimension_semantics=("arbitrary",) * n_axes,
        vmem_limit_bytes=vmem_mib * MIB)


def _rms(x, g):
    return x * lax.rsqrt(jnp.mean(x * x, axis=-1, keepdims=True) + EPS) * g


def _tile_info(i):
    is_ctx = i < NCT
    jl = jnp.maximum(i - NCT, 0)
    j = jnp.where(is_ctx, 0, jl & (LT - 1))
    mrow = jnp.where(is_ctx, 0, 1 + lax.shift_right_logical(jl, LT_SHIFT))
    return is_ctx, j, mrow


def _mod(mod_ref, mrow, k):
    return mod_ref[pl.ds(mrow, 1), k * D:(k + 1) * D]


def _store_slabs(ref, x):
    for s in range(NS):
        ref[pl.ds(s, x.shape[0], stride=NS), :] = x[:, s * 128:(s + 1) * 128]


def _load_slabs(ref, rows):
    return jnp.concatenate([ref[pl.ds(s, rows, stride=NS), :] for s in range(NS)], axis=1)


def _dot_nt(a, b):
    return lax.dot_general(a, b, (((1,), (1,)), ((), ())), preferred_element_type=F32)


def _ffn_pre(x1, mrow, mod_ref, nf_ref, wrc_ref, wrh_ref, rb_ref, h2u_ref, ri_ref, rw_ref):
    h2 = _rms(x1, nf_ref[...]) * (1.0 + _mod(mod_ref, mrow, 4)) + _mod(mod_ref, mrow, 3)
    h_hi = h2.astype(BF16)
    h_lo = (h2 - h_hi.astype(F32)).astype(BF16)
    _store_slabs(h2u_ref, h2)

    hw = jnp.dot(h_hi, wrc_ref[...], preferred_element_type=F32)
    lw = jnp.dot(h_lo, wrh_ref[...], preferred_element_type=F32)
    lt = jnp.transpose(hw[:, :NRL] + hw[:, NRL:] + lw + rb_ref[...])
    row8 = lax.broadcasted_iota(I32, (8, TM), 0)
    gl = jnp.where(row8 < N_GROUPS, lt[0:8], -jnp.inf)
    ge = jnp.exp(gl - jnp.max(gl, axis=0, keepdims=True))
    gprob = ge / jnp.sum(ge, axis=0, keepdims=True)
    g_p = jnp.max(gprob, axis=0, keepdims=True)
    g_idx = jnp.min(jnp.where(gprob == g_p, row8, 8), axis=0, keepdims=True)
    e_sel = lt[ER0:ER0 + 8]
    for g in range(1, N_GROUPS):
        e_sel = jnp.where(g_idx == g, lt[ER0 + 8 * g:ER0 + 8 * g + 8], e_sel)
    ee = jnp.exp(e_sel - jnp.max(e_sel, axis=0, keepdims=True))
    eprob = ee / jnp.sum(ee, axis=0, keepdims=True)
    p0 = jnp.max(eprob, axis=0, keepdims=True)
    i0 = jnp.min(jnp.where(eprob == p0, row8, 8), axis=0, keepdims=True)
    rest = jnp.where(row8 == i0, -1.0, eprob)
    p1 = jnp.max(rest, axis=0, keepdims=True)
    i1 = jnp.min(jnp.where(rest == p1, row8, 8), axis=0, keepdims=True)
    psum = p0 + p1
    w0 = g_p * p0 / psum
    w1 = g_p * p1 / psum
    id0 = g_idx * EXP_PER_GROUP + i0
    id1 = g_idx * EXP_PER_GROUP + i1

    ri_ref[...] = jnp.where(row8 == 0, id0, jnp.where(row8 == 1, id1, 0))
    rw_ref[...] = jnp.where(row8 == 0, w0, jnp.where(row8 == 1, w1, 0.0))


def _ffn_in_specs():
    const2 = lambda i: (0, 0)
    return [
        pl.BlockSpec((1, D), const2),
        pl.BlockSpec((D, 2 * NRL), const2),
        pl.BlockSpec((D, NRL), const2),
        pl.BlockSpec((1, NRL), const2),
    ]


def _ffn_out_shapes():
    return [
        jax.ShapeDtypeStruct((T, D), F32),
        jax.ShapeDtypeStruct((T * NS, 128), F32),
        jax.ShapeDtypeStruct((NT, 8, TM), I32),
        jax.ShapeDtypeStruct((NT, 8, TM), F32),
    ]


def _ffn_out_specs():
    return [
        pl.BlockSpec((TM, D), lambda i: (i, 0)),
        pl.BlockSpec((TM * NS, 128), lambda i: (i, 0)),
        pl.BlockSpec((None, 8, TM), lambda i: (i, 0, 0)),
        pl.BlockSpec((None, 8, TM), lambda i: (i, 0, 0)),
    ]


def _mod_kernel(cs_ref, w_ref, b_ref, o_ref):
    s = cs_ref[...]
    a = s * jax.nn.sigmoid(s)
    o_ref[...] = jnp.dot(a, w_ref[...], precision=_HI, preferred_element_type=F32) + b_ref[...]


def _modulation(cs, w_ada, b_ada):
    nb = 6
    return pl.pallas_call(
        _mod_kernel,
        out_shape=jax.ShapeDtypeStruct((DEPTH, 8, 6 * D), F32),
        grid=(DEPTH, nb),
        in_specs=[
            pl.BlockSpec((8, D), lambda l, n: (0, 0)),
            pl.BlockSpec((None, D, D), lambda l, n: (l, 0, n)),
            pl.BlockSpec((None, 1, D), lambda l, n: (l, 0, n)),
        ],
        out_specs=pl.BlockSpec((None, 8, D), lambda l, n: (l, 0, n)),
        compiler_params=_cparams(2, 32),
        name="modulation",
    )(cs, w_ada, b_ada.reshape(DEPTH, 1, 6 * D))


def _halo_specs(width):
    nb = T // HALO
    per = TM // HALO
    return [
        pl.BlockSpec((TM, width), lambda i: (i, 0)),
        pl.BlockSpec((HALO, width), lambda i: (jnp.maximum(i * per - 1, 0), 0)),
        pl.BlockSpec((HALO, width), lambda i: (jnp.minimum((i + 1) * per, nb - 1), 0)),
    ]


def _normed_halo(xt, xp, xn, i, mod_ref, nm_ref):
    is_ctx, j, mrow = _tile_info(i)
    g = nm_ref[...]
    sc = 1.0 + _mod(mod_ref, mrow, 1)
    sh = _mod(mod_ref, mrow, 0)
    pv = jnp.where(jnp.logical_and(jnp.logical_not(is_ctx), j > 0), 1.0, 0.0)
    nv = jnp.where(jnp.logical_and(jnp.logical_not(is_ctx), j < LT - 1), 1.0, 0.0)
    ht = _rms(xt, g) * sc + sh
    hp = (_rms(xp, g) * sc + sh) * pv
    hn = (_rms(xn, g) * sc + sh) * nv
    return ht, hp, hn, is_ctx, j, mrow


def _pool_kernel(split, *refs):
    if split:
        xc_ref, x_ref, xp_ref, xn_ref = refs[:4]
        refs = refs[4:]
    else:
        x_ref, xp_ref, xn_ref = refs[:3]
        refs = refs[3:]
    (mod_ref, nm_ref, band_ref, wp_ref, ps_ref, nf_ref, wrc_ref, wrh_ref, rb_ref,
     x1_ref, h2u_ref, ri_ref, rw_ref) = refs
    i = pl.program_id(0)
    x = x_ref[...]
    if split:
        x = jnp.where(i < NCT, xc_ref[...], x)
    ht, hp, hn, is_ctx, j, mrow = _normed_halo(x, xp_ref[...], xn_ref[...], i, mod_ref, nm_ref)
    hext = jnp.concatenate([ht, hp, hn, jnp.zeros((KB - TM - 2 * HALO, D), F32)], axis=0)
    e_hi = hext.astype(BF16)
    e_lo = (hext - e_hi.astype(F32)).astype(BF16)
    seq_len = jnp.where(is_ctx, CTX_LEN, LAT_LEN)
    t = j * TM + lax.broadcasted_iota(I32, (TM, POOL_CH), 0)
    outs = []
    for g, win in enumerate(POOL_WINDOWS):
        lo = win // 2
        hi = win - 1 - lo
        cols = slice(g * POOL_CH, (g + 1) * POOL_CH)
        band = band_ref[g]
        s = (jnp.dot(band, e_hi[:, cols], preferred_element_type=F32)
             + jnp.dot(band, e_lo[:, cols], preferred_element_type=F32))
        cnt = (jnp.minimum(t + hi, seq_len - 1) - jnp.maximum(t - lo, 0) + 1).astype(F32)
        d = s / cnt - ht[:, cols]
        outs.append(jnp.dot(d.astype(BF16), wp_ref[g], preferred_element_type=F32))
    o = jnp.concatenate(outs, axis=1) * ps_ref[...]
    x1 = x + _mod(mod_ref, mrow, 2) * o
    x1_ref[...] = x1
    _ffn_pre(x1, mrow, mod_ref, nf_ref, wrc_ref, wrh_ref, rb_ref, h2u_ref, ri_ref, rw_ref)


def _pool_band():
    band = np.zeros((len(POOL_WINDOWS), TM, KB), np.float32)
    r = np.arange(TM)[:, None]
    for g, win in enumerate(POOL_WINDOWS):
        lo = win // 2
        hi = win - 1 - lo
        pos = np.concatenate([np.arange(TM), np.arange(-HALO, 0), np.arange(TM, TM + HALO)])[None, :]
        band[g, :, :TM + 2 * HALO] = (pos >= r - lo) & (pos <= r + hi)
    return jnp.asarray(band, BF16)


def _split_halo_specs():
    nb = T_LAT // HALO
    per = TM // HALO
    lat = lambda i: jnp.maximum(i - NCT, 0)
    return [
        pl.BlockSpec((TM, D), lambda i: (jnp.minimum(i, NCT - 1), 0)),
        pl.BlockSpec((TM, D), lambda i: (lat(i), 0)),
        pl.BlockSpec((HALO, D), lambda i: (jnp.maximum(lat(i) * per - 1, 0), 0)),
        pl.BlockSpec((HALO, D), lambda i: (jnp.minimum((lat(i) + 1) * per, nb - 1), 0)),
    ]


def _pool_layer(xs, mod_l, nm, wp, ps, ffn_args):
    const2 = lambda i: (0, 0)
    const3 = lambda i: (0, 0, 0)
    split = isinstance(xs, tuple)
    x_args = (xs[0], xs[1], xs[1], xs[1]) if split else (xs, xs, xs)
    return pl.pallas_call(
        functools.partial(_pool_kernel, split),
        out_shape=_ffn_out_shapes(),
        grid=(NT,),
        in_specs=(_split_halo_specs() if split else _halo_specs(D)) + [
            pl.BlockSpec((8, 6 * D), const2),
            pl.BlockSpec((1, D), const2),
            pl.BlockSpec((len(POOL_WINDOWS), TM, KB), const3),
            pl.BlockSpec((len(POOL_WINDOWS), POOL_CH, POOL_CH), const3),
            pl.BlockSpec((1, D), const2),
        ] + _ffn_in_specs(),
        out_specs=_ffn_out_specs(),
        compiler_params=_cparams(1, 48),
        name="pool_mixer",
    )(*x_args, mod_l, nm, _pool_band(), wp.astype(BF16), ps, *ffn_args)


def _conv_kernel(x_ref, xp_ref, xn_ref, mod_ref, nm_ref, win_ref, cw_ref, wout_ref,
                 nf_ref, wrc_ref, wrh_ref, rb_ref,
                 x1_ref, h2u_ref, ri_ref, rw_ref):
    i = pl.program_id(0)
    x = x_ref[...]
    ht, hp, hn, is_ctx, j, mrow = _normed_halo(x, xp_ref[...], xn_ref[...], i, mod_ref, nm_ref)
    hext = jnp.concatenate([ht, hp, hn], axis=0).astype(BF16)
    bcv = jnp.dot(hext, win_ref[...], preferred_element_type=F32)
    b = bcv[:TM, :D]
    u = bcv[:, D:2 * D] * bcv[:, 2 * D:]
    um = u[:TM]
    u_before = u[TM + HALO - 1:TM + HALO]
    u_after = u[TM + HALO:TM + HALO + 1]
    row = lax.broadcasted_iota(I32, (TM, D), 0)
    up = jnp.where(row == 0, u_before, pltpu.roll(um, 1, 0))
    un = jnp.where(row == TM - 1, u_after, pltpu.roll(um, TM - 1, 0))
    cw = cw_ref[...]
    conv = up * cw[0:1] + um * cw[1:2] + un * cw[2:3]
    o = jnp.dot((b * conv).astype(BF16), wout_ref[...], preferred_element_type=F32)
    x1 = x + _mod(mod_ref, mrow, 2) * o
    x1_ref[...] = x1
    _ffn_pre(x1, mrow, mod_ref, nf_ref, wrc_ref, wrh_ref, rb_ref, h2u_ref, ri_ref, rw_ref)


def _conv_layer(x, mod_l, nm, w_in, cw, w_out, ffn_args):
    const2 = lambda i: (0, 0)
    cw8 = jnp.concatenate([cw, jnp.zeros((8 - cw.shape[0], D), F32)], axis=0)
    return pl.pallas_call(
        _conv_kernel,
        out_shape=_ffn_out_shapes(),
        grid=(NT,),
        in_specs=_halo_specs(D) + [
            pl.BlockSpec((8, 6 * D), const2),
            pl.BlockSpec((1, D), const2),
            pl.BlockSpec((D, 3 * D), const2),
            pl.BlockSpec((8, D), const2),
            pl.BlockSpec((D, D), const2),
        ] + _ffn_in_specs(),
        out_specs=_ffn_out_specs(),
        compiler_params=_cparams(1, 56),
        name="conv_mixer",
    )(x, x, x, mod_l, nm, w_in.astype(BF16), cw8, w_out.astype(BF16), *ffn_args)


W_CAT = Q_RANK + KV_RANK + 2 * QK_ROPE
QH = 2 * QK_NOPE


def _mla_proj_kernel(x_ref, mod_ref, nm_ref, wcat_ref, qn_ref, kvn_ref, wq_ref, wqs_ref,
                     c_ref, s_ref, q_ref, ckv_ref, kpe_ref):
    i = pl.program_id(0)
    _, _, mrow = _tile_info(i)
    h = _rms(x_ref[...], nm_ref[...]) * (1.0 + _mod(mod_ref, mrow, 1)) + _mod(mod_ref, mrow, 0)
    y = jnp.dot(h.astype(BF16), wcat_ref[...], preferred_element_type=F32)
    cqn = _rms(y[:, :Q_RANK], qn_ref[...]).astype(BF16)
    q = jnp.dot(cqn, wq_ref[...], preferred_element_type=F32)
    qs = jnp.dot(cqn, wqs_ref[...], preferred_element_type=F32)
    cos = c_ref[...]
    sin = s_ref[...]
    parts = []
    for hh in range(HEADS):
        parts.append(q[:, hh * QH:hh * QH + QK_NOPE])
        parts.append(q[:, hh * QH + QK_NOPE:(hh + 1) * QH] * cos + qs[:, hh * 128:(hh + 1) * 128] * sin)
    q_ref[...] = (jnp.concatenate(parts, axis=1) * (SM_SCALE * LOG2E)).astype(BF16)
    ckv_ref[...] = _rms(y[:, Q_RANK:Q_RANK + KV_RANK], kvn_ref[...])
    k2 = y[:, Q_RANK + KV_RANK:]
    kr = k2 * cos + pltpu.roll(k2, QK_ROPE, 1) * sin
    lane = lax.broadcasted_iota(I32, (TM, 2 * QK_ROPE), 1)
    kpe_ref[...] = jnp.where(lane < QK_ROPE, kr, 0.0)


def _rope_tables():
    rows_n = LAT_LEN // GRID_W
    rows = jnp.repeat(jnp.arange(rows_n), GRID_W).astype(F32)
    cols = jnp.tile(jnp.arange(GRID_W), rows_n).astype(F32)
    inv = ROPE_BASE ** (-(jnp.arange(AXIS_FREQS, dtype=F32) / AXIS_FREQS))
    ang = jnp.stack([rows[:, None] * inv, cols[:, None] * inv], axis=1)
    cos, sin = jnp.cos(ang), jnp.sin(ang)
    c64 = jnp.concatenate([cos[:, 0], cos[:, 0], cos[:, 1], cos[:, 1]], axis=1)
    s64 = jnp.concatenate([-sin[:, 0], sin[:, 0], -sin[:, 1], sin[:, 1]], axis=1)
    c = jnp.concatenate([c64, jnp.ones((LAT_LEN, QK_ROPE), F32)], axis=1)
    s = jnp.concatenate([s64, jnp.zeros((LAT_LEN, QK_ROPE), F32)], axis=1)
    c = jnp.concatenate([c, jnp.ones((TM, 2 * QK_ROPE), F32)], axis=0)
    s = jnp.concatenate([s, jnp.zeros((TM, 2 * QK_ROPE), F32)], axis=0)
    return c, s


def _swap_rope_cols(w):
    f = AXIS_FREQS
    return jnp.concatenate([w[..., f:2 * f], w[..., :f], w[..., 3 * f:], w[..., 2 * f:3 * f]], axis=-1)


def _mla_proj(x, mod_l, nm, w_dq, q_norm, w_uq, w_dkv, kv_norm):
    const2 = lambda i: (0, 0)
    w_kpe = w_dkv[:, KV_RANK:]
    wcat = jnp.concatenate([w_dq, w_dkv[:, :KV_RANK], w_kpe, _swap_rope_cols(w_kpe)], axis=1).astype(BF16)
    wq = w_uq.reshape(Q_RANK, HEADS, QK_NOPE + QK_ROPE)
    zpad = jnp.zeros((Q_RANK, HEADS, QK_ROPE), F32)
    wq_a = jnp.concatenate([wq, zpad], axis=2).reshape(Q_RANK, HEADS * QH).astype(BF16)
    wq_s = jnp.concatenate([_swap_rope_cols(wq[:, :, QK_NOPE:]), zpad], axis=2)
    wq_s = wq_s.reshape(Q_RANK, HEADS * 128).astype(BF16)
    cos, sin = _rope_tables()
    tab_idx = lambda i: (jnp.where(i < NCT, LT, jnp.maximum(i - NCT, 0) & (LT - 1)), 0)
    return pl.pallas_call(
        _mla_proj_kernel,
        out_shape=[jax.ShapeDtypeStruct((T, HEADS * QH), BF16),
                   jax.ShapeDtypeStruct((T, KV_RANK), F32),
                   jax.ShapeDtypeStruct((T, 2 * QK_ROPE), F32)],
        grid=(NT,),
        in_specs=[
            pl.BlockSpec((TM, D), lambda i: (i, 0)),
            pl.BlockSpec((8, 6 * D), const2),
            pl.BlockSpec((1, D), const2),
            pl.BlockSpec((D, W_CAT), const2),
            pl.BlockSpec((1, Q_RANK), const2),
            pl.BlockSpec((1, KV_RANK), const2),
            pl.BlockSpec((Q_RANK, HEADS * QH), const2),
            pl.BlockSpec((Q_RANK, HEADS * 128), const2),
            pl.BlockSpec((TM, 2 * QK_ROPE), tab_idx),
            pl.BlockSpec((TM, 2 * QK_ROPE), tab_idx),
        ],
        out_specs=[pl.BlockSpec((TM, HEADS * QH), lambda i: (i, 0)),
                   pl.BlockSpec((TM, KV_RANK), lambda i: (i, 0)),
                   pl.BlockSpec((TM, 2 * QK_ROPE), lambda i: (i, 0))],
        compiler_params=_cparams(1, 48),
        name="mla_proj",
    )(x, mod_l, nm, wcat, q_norm, kv_norm, wq_a, wq_s, cos, sin)


def _kv_expand_kernel(ckv_ref, kpe_ref, wk_ref, wv_ref, k_ref, v_ref):
    c = ckv_ref[...].astype(BF16)
    kn = jnp.dot(c, wk_ref[...], preferred_element_type=F32).astype(BF16)
    kp = kpe_ref[...].astype(BF16)
    parts = []
    for hh in range(HEADS):
        parts.append(kn[:, hh * QK_NOPE:(hh + 1) * QK_NOPE])
        parts.append(kp)
    k_ref[...] = jnp.concatenate(parts, axis=1)
    v_ref[...] = jnp.dot(c, wv_ref[...], preferred_element_type=F32).astype(BF16)


def _kv_expand(ckv, kpe, wk, wv, name):
    n = ckv.shape[0]
    const2 = lambda i: (0, 0)
    return pl.pallas_call(
        _kv_expand_kernel,
        out_shape=[jax.ShapeDtypeStruct((n, HEADS * QH), BF16),
                   jax.ShapeDtypeStruct((n, HEADS * V_DIM), BF16)],
        grid=(n // TM,),
        in_specs=[
            pl.BlockSpec((TM, KV_RANK), lambda i: (i, 0)),
            pl.BlockSpec((TM, 2 * QK_ROPE), lambda i: (i, 0)),
            pl.BlockSpec((KV_RANK, HEADS * QK_NOPE), const2),
            pl.BlockSpec((KV_RANK, HEADS * V_DIM), const2),
        ],
        out_specs=[pl.BlockSpec((TM, HEADS * QH), lambda i: (i, 0)),
                   pl.BlockSpec((TM, HEADS * V_DIM), lambda i: (i, 0))],
        compiler_params=_cparams(1, 32),
        name=name,
    )(ckv, kpe, wk, wv)


def _attend(q, k, v):
    s = _dot_nt(q, k)
    p = jnp.exp2(s - jnp.max(s, axis=1, keepdims=True))
    l = jnp.sum(p, axis=1, keepdims=True)
    return jnp.dot(p.astype(BF16), v, preferred_element_type=F32) / l


KC = 512
NKC = NK_LAT // KC


def _attn_lat_kernel(q_ref, k_ref, v_ref, o_ref, s_ref, m_ref, l_ref, acc_ref):
    q = q_ref[...]
    m_ref[...] = jnp.full((TQ, 128), -jnp.inf, F32)

    def scores(c, carry):
        k = k_ref[pl.ds(pl.multiple_of(c * KC, KC), KC), :]
        s = _dot_nt(q, k)
        s_ref[c] = s
        m = m_ref[...]
        for j in range(KC // 128):
            m = jnp.maximum(m, s[:, j * 128:(j + 1) * 128])
        m_ref[...] = m
        return carry
    lax.fori_loop(0, NKC, scores, 0, unroll=True)

    mb = jnp.broadcast_to(jnp.max(m_ref[...], axis=1, keepdims=True), (TQ, 128))
    l_ref[...] = jnp.zeros((TQ, 128), F32)
    acc_ref[...] = jnp.zeros((TQ, V_DIM), F32)

    def weighted(c, carry):
        s = s_ref[c]
        ps = [jnp.exp2(s[:, j * 128:(j + 1) * 128] - mb) for j in range(KC // 128)]
        l = l_ref[...]
        for pj in ps:
            l = l + pj
        l_ref[...] = l
        p = jnp.concatenate(ps, axis=1).astype(BF16)
        v = v_ref[pl.ds(pl.multiple_of(c * KC, KC), KC), :]
        acc_ref[...] += jnp.dot(p, v, preferred_element_type=F32)
        return carry
    lax.fori_loop(0, NKC, weighted, 0, unroll=3)

    o_ref[...] = (acc_ref[...] / jnp.sum(l_ref[...], axis=1, keepdims=True)).astype(BF16)


def _attn_lat(q, k, v):
    nq = LAT_LEN // TQ
    q0 = T_CTX // TQ
    return pl.pallas_call(
        _attn_lat_kernel,
        scratch_shapes=[pltpu.VMEM((NKC, TQ, KC), F32), pltpu.VMEM((TQ, 128), F32),
                        pltpu.VMEM((TQ, 128), F32), pltpu.VMEM((TQ, V_DIM), F32)],
        out_shape=jax.ShapeDtypeStruct((T_LAT, HEADS * V_DIM), BF16),
        grid=(N_LAT_SEQ, HEADS, nq),
        in_specs=[
            pl.BlockSpec((TQ, QH), lambda b, h, t: (q0 + b * nq + t, h)),
            pl.BlockSpec((NK_LAT, QH), lambda b, h, t: (b, h)),
            pl.BlockSpec((NK_LAT, V_DIM), lambda b, h, t: (b, h)),
        ],
        out_specs=pl.BlockSpec((TQ, V_DIM), lambda b, h, t: (b * nq + t, h)),
        compiler_params=_cparams(3, 48),
        name="attn_latent",
    )(q, k, v)


def _attn_ctx_kernel(q_ref, k_ref, v_ref, o_ref):
    outs = []
    for hh in range(HEADS):
        outs.append(_attend(q_ref[:, hh * QH:(hh + 1) * QH], k_ref[:, hh * QH:(hh + 1) * QH],
                            v_ref[:, hh * V_DIM:(hh + 1) * V_DIM]))
    o_ref[...] = jnp.concatenate(outs, axis=1).astype(BF16)


def _attn_ctx(q, k, v):
    return pl.pallas_call(
        _attn_ctx_kernel,
        out_shape=jax.ShapeDtypeStruct((T_CTX, HEADS * V_DIM), BF16),
        grid=(N_CTX_SEQ,),
        in_specs=[
            pl.BlockSpec((CTX_LEN, HEADS * QH), lambda b: (b, 0)),
            pl.BlockSpec((CTX_LEN, HEADS * QH), lambda b: (b, 0)),
            pl.BlockSpec((CTX_LEN, HEADS * V_DIM), lambda b: (b, 0)),
        ],
        out_specs=pl.BlockSpec((CTX_LEN, HEADS * V_DIM), lambda b: (b, 0)),
        compiler_params=_cparams(1, 32),
        name="attn_context",
    )(q, k, v)


def _attn_out_kernel(x_ref, oc_ref, ol_ref, mod_ref, wo_ref,
                     nf_ref, wrc_ref, wrh_ref, rb_ref,
                     x1_ref, h2u_ref, ri_ref, rw_ref):
    i = pl.program_id(0)
    is_ctx, _, mrow = _tile_info(i)
    att = jnp.where(is_ctx, oc_ref[...], ol_ref[...])
    o = jnp.dot(att, wo_ref[...], preferred_element_type=F32)
    x1 = x_ref[...] + _mod(mod_ref, mrow, 2) * o
    x1_ref[...] = x1
    _ffn_pre(x1, mrow, mod_ref, nf_ref, wrc_ref, wrh_ref, rb_ref, h2u_ref, ri_ref, rw_ref)


def _attn_out(x, o_ctx, o_lat, mod_l, w_o, ffn_args):
    const2 = lambda i: (0, 0)
    return pl.pallas_call(
        _attn_out_kernel,
        out_shape=_ffn_out_shapes(),
        grid=(NT,),
        in_specs=[
            pl.BlockSpec((TM, D), lambda i: (i, 0)),
            pl.BlockSpec((TM, D), lambda i: (jnp.minimum(i, NCT - 1), 0)),
            pl.BlockSpec((TM, D), lambda i: (jnp.maximum(i - NCT, 0), 0)),
            pl.BlockSpec((8, 6 * D), const2),
            pl.BlockSpec((D, D), const2),
        ] + _ffn_in_specs(),
        out_specs=_ffn_out_specs(),
        compiler_params=_cparams(1, 48),
        name="attn_out",
    )(x, o_ctx, o_lat, mod_l, w_o.astype(BF16), *ffn_args)


def _mla_layer(x, mod_l, nm, cache_ckv, cache_kpe, w_dq, q_norm, w_uq, w_dkv, kv_norm, w_ukv, w_o, ffn_args):
    q, ckv, kpe = _mla_proj(x, mod_l, nm, w_dq, q_norm, w_uq, w_dkv, kv_norm)
    wkv = w_ukv.reshape(KV_RANK, HEADS, QK_NOPE + V_DIM)
    wk = wkv[:, :, :QK_NOPE].reshape(KV_RANK, HEADS * QK_NOPE).astype(BF16)
    wv = wkv[:, :, QK_NOPE:].reshape(KV_RANK, HEADS * V_DIM).astype(BF16)
    k_c, v_c = _kv_expand(ckv[:T_CTX], kpe[:T_CTX], wk, wv, "kv_expand_context")
    kpe_cache = jnp.concatenate([cache_kpe, jnp.zeros_like(cache_kpe)], axis=-1)
    ckv_l = jnp.concatenate([cache_ckv, ckv[T_CTX:].reshape(N_LAT_SEQ, LAT_LEN, KV_RANK)], axis=1)
    kpe_l = jnp.concatenate([kpe_cache, kpe[T_CTX:].reshape(N_LAT_SEQ, LAT_LEN, 2 * QK_ROPE)], axis=1)
    k_l, v_l = _kv_expand(ckv_l.reshape(N_LAT_SEQ * NK_LAT, KV_RANK),
                          kpe_l.reshape(N_LAT_SEQ * NK_LAT, 2 * QK_ROPE), wk, wv, "kv_expand_latent")
    o_c = _attn_ctx(q, k_c, v_c)
    o_l = _attn_lat(q, k_l, v_l)
    outs = _attn_out(x, o_c, o_l, mod_l, w_o, ffn_args)
    new_ckv = ckv[:T_CTX].reshape(N_CTX_SEQ, 1, CTX_LEN, KV_RANK)
    new_kpe = kpe[:T_CTX, :QK_ROPE].reshape(N_CTX_SEQ, 1, CTX_LEN, QK_ROPE)
    return outs, new_ckv, new_kpe


GROUP = 8


def _slab(ref, row, n=1):
    return ref.at[pl.ds(pl.multiple_of(row * NS, NS), n * NS)]


def _dispatch_kernel(meta_ref, psv_ref, ri_ref, h2_ref, xs_ref, pos_ref, stage, zbuf, *rest):
    bufs, (sem, dsem, zsem) = rest[:-3], rest[-3:]
    i = pl.program_id(0)

    def row_copies(b):
        return [pltpu.make_async_copy(stage.at[b, row], bufs[b * 2 * LC + row], sem.at[b])
                for row in range(2 * LC)]

    def tile_rows_done(b):
        return pltpu.make_async_copy(_slab(h2_ref, 0, 2 * TM), _slab(xs_ref, 0, 2 * TM), dsem.at[b])

    @pl.when(i == 0)
    def _():
        zbuf[...] = jnp.zeros(zbuf.shape, F32)

        def pad_fill(e):
            return pltpu.make_async_copy(zbuf, _slab(xs_ref, meta_ref[2, e] - TS, TS), zsem)

        def start(e, c):
            @pl.when(meta_ref[2, e] > meta_ref[0, e])
            def _():
                pad_fill(e).start()
            return c

        def wait(e, c):
            @pl.when(meta_ref[2, e] > meta_ref[0, e])
            def _():
                pad_fill(e).wait()
            return c
        lax.fori_loop(0, N_EXPERTS, start, 0)
        lax.fori_loop(0, N_EXPERTS, wait, 0)

        def tail_fill(tile):
            return pltpu.make_async_copy(zbuf, _slab(xs_ref, tile * TS, TS), zsem)

        def tail_start(tile, c):
            tail_fill(tile).start()
            return c

        def tail_wait(tile, c):
            tail_fill(tile).wait()
            return c
        first_unused = lax.shift_right_logical(meta_ref[2, N_EXPERTS - 1], jnp.int32(TS.bit_length() - 1))
        lax.fori_loop(first_unused, NST, tail_start, 0)
        lax.fori_loop(first_unused, NST, tail_wait, 0)

    @pl.when(i < NT)
    def _():
        ri = ri_ref[...]
        rowe = lax.broadcasted_iota(I32, (N_EXPERTS, TM), 0)
        ps = psv_ref[...]
        pos0 = jnp.sum(jnp.where(rowe == ri[0:1], ps, 0), axis=0, keepdims=True) + ri[2:3]
        pos1 = jnp.sum(jnp.where(rowe == ri[1:2], ps, 0), axis=0, keepdims=True) + ri[3:4]
        row8 = lax.broadcasted_iota(I32, (8, 128), 0)
        st = jnp.zeros((8, 128), I32)
        for k, pos in enumerate((pos0, pos1)):
            for h in range(LC):
                st = jnp.where(row8 == k * LC + h, pos[:, h * 128:(h + 1) * 128], st)
        pos_ref[...] = st
        for b in range(2):
            @pl.when(i % 2 == b)
            def _(b=b):
                stage[b] = st
                for c in row_copies(b):
                    c.start()

    j = i - 1
    for b in range(2):
        @pl.when(jnp.logical_and(i >= 1, j % 2 == b))
        def _(b=b):
            for c in row_copies(b):
                c.wait()
            for h in range(LC):
                def toks(g, c, h=h):
                    t0 = g * GROUP
                    slots = [[bufs[b * 2 * LC + k * LC + h][t0 + u] for k in range(2)] for u in range(GROUP)]
                    for u in range(GROUP):
                        src = _slab(h2_ref, j * TM + h * 128 + t0 + u)
                        for k in range(2):
                            pltpu.make_async_copy(src, _slab(xs_ref, slots[u][k]), dsem.at[b]).start(priority=k)
                    return c
                lax.fori_loop(0, 128 // GROUP, toks, 0)

            @pl.when(j >= 1)
            def _():
                tile_rows_done(1 - b).wait()

            @pl.when(i == NT)
            def _():
                tile_rows_done(b).wait()


def _rank_kernel(ri_ref, trio_ref, ro_ref, cnt_ref, carry_ref):
    @pl.when(pl.program_id(0) == 0)
    def _():
        carry_ref[...] = jnp.zeros_like(carry_ref)

    rowe = lax.broadcasted_iota(I32, (N_EXPERTS, TM), 0)
    row8 = lax.broadcasted_iota(I32, (8, TM), 0)
    ids = [ri_ref[tt] for tt in range(RG)]
    ohs = [rowe == ids[tt][k:k + 1] for tt in range(RG) for k in range(2)]
    ohb = jnp.concatenate([jnp.where(oh, 1.0, 0.0).astype(BF16) for oh in ohs], axis=0)
    pt = jnp.dot(ohb, trio_ref[...], preferred_element_type=F32)
    carry = carry_ref[...]
    for tt in range(RG):
        ranks = []
        for k in range(2):
            r0 = (tt * 2 + k) * N_EXPERTS
            before = carry + pt[r0:r0 + N_EXPERTS, :TM]
            ranks.append(jnp.sum(jnp.where(ohs[tt * 2 + k], before, 0.0), axis=0, keepdims=True).astype(I32))
            carry = carry + pt[r0:r0 + N_EXPERTS, TM:]
        ro_ref[tt] = jnp.where(row8 == 0, ids[tt][0:1],
                     jnp.where(row8 == 1, ids[tt][1:2],
                     jnp.where(row8 == 2, ranks[0],
                     jnp.where(row8 == 3, ranks[1], 0))))
    carry_ref[...] = carry
    cnt_ref[...] = carry


def _rank(route_i):
    tri = np.triu(np.ones((TM, TM), np.float32), 1)
    trio = jnp.asarray(np.concatenate([tri, np.ones((TM, TM), np.float32)], axis=1), BF16)
    return pl.pallas_call(
        _rank_kernel,
        out_shape=[jax.ShapeDtypeStruct((NT, 8, TM), I32), jax.ShapeDtypeStruct((N_EXPERTS, TM), F32)],
        grid=(NT // RG,),
        in_specs=[pl.BlockSpec((RG, 8, TM), lambda g: (g, 0, 0)),
                  pl.BlockSpec((TM, 2 * TM), lambda g: (0, 0))],
        out_specs=[pl.BlockSpec((RG, 8, TM), lambda g: (g, 0, 0)),
                   pl.BlockSpec((N_EXPERTS, TM), lambda g: (0, 0))],
        scratch_shapes=[pltpu.VMEM((N_EXPERTS, TM), F32)],
        compiler_params=_cparams(1, 16),
        name="expert_rank",
    )(route_i, trio)


def _dispatch(meta, pad_starts, route_i, h2):
    tile = lambda i: (jnp.minimum(i, NT - 1), 0, 0)
    return pl.pallas_call(
        _dispatch_kernel,
        out_shape=[jax.ShapeDtypeStruct((CAP * NS, 128), F32),
                   jax.ShapeDtypeStruct((NT, 8, 128), I32)],
        grid=(NT + 1,),
        in_specs=[pl.BlockSpec(memory_space=pltpu.SMEM),
                  pl.BlockSpec((N_EXPERTS, TM), lambda i: (0, 0)),
                  pl.BlockSpec((None, 8, TM), tile),
                  pl.BlockSpec(memory_space=pl.ANY)],
        out_specs=[pl.BlockSpec(memory_space=pl.ANY), pl.BlockSpec((None, 8, 128), tile)],
        scratch_shapes=([pltpu.VMEM((2, 8, 128), I32), pltpu.VMEM((TS * NS, 128), F32)]
                        + [pltpu.SMEM((128,), I32)] * (4 * LC)
                        + [pltpu.SemaphoreType.DMA((2,)), pltpu.SemaphoreType.DMA((2,)), pltpu.SemaphoreType.DMA]),
        compiler_params=_cparams(1, 16),
        name="moe_dispatch",
    )(meta, jnp.broadcast_to(pad_starts[:, None], (N_EXPERTS, TM)), route_i, h2)


def _moe_kernel(blk_ref, nu_ref, xs_ref, w1_ref, w3_ref, w2_ref, o_ref, wb1, wb3, wb2, pe_ref):
    i = pl.program_id(0)
    n_used = nu_ref[0]

    @pl.when(i == 0)
    def _():
        pe_ref[0] = -1

    @pl.when(i >= n_used)
    def _():
        o_ref[...] = jnp.zeros(o_ref.shape, F32)

    @pl.when(i < n_used)
    def _():
        e = blk_ref[i]

        @pl.when(e != pe_ref[0])
        def _():
            wb1[...] = w1_ref[...].astype(BF16)
            wb3[...] = w3_ref[...].astype(BF16)
            wb2[...] = w2_ref[...].astype(BF16)
            pe_ref[0] = e

        xb = _load_slabs(xs_ref, TS).astype(BF16)
        a = jnp.dot(xb, wb1[...], preferred_element_type=F32)
        b = jnp.dot(xb, wb3[...], preferred_element_type=F32)
        hm = (a * jax.nn.sigmoid(a) * b).astype(BF16)
        _store_slabs(o_ref, jnp.dot(hm, wb2[...], preferred_element_type=F32))


def _moe_experts(l, blk_e, n_used, xs, w1, w3, w2):
    grid_spec = pltpu.PrefetchScalarGridSpec(
        num_scalar_prefetch=2,
        grid=(NST,),
        in_specs=[
            pl.BlockSpec((TS * NS, 128), lambda i, blk, nu: (jnp.minimum(i, nu[0] - 1), 0)),
            pl.BlockSpec((None, None, D, D_EXPERT), lambda i, blk, nu: (l, blk[i], 0, 0)),
            pl.BlockSpec((None, None, D, D_EXPERT), lambda i, blk, nu: (l, blk[i], 0, 0)),
            pl.BlockSpec((None, None, D_EXPERT, D), lambda i, blk, nu: (l, blk[i], 0, 0)),
        ],
        out_specs=pl.BlockSpec((TS * NS, 128), lambda i, blk, nu: (i, 0)),
        scratch_shapes=[
            pltpu.VMEM((D, D_EXPERT), BF16),
            pltpu.VMEM((D, D_EXPERT), BF16),
            pltpu.VMEM((D_EXPERT, D), BF16),
            pltpu.SMEM((1,), I32),
        ],
    )
    return pl.pallas_call(
        _moe_kernel,
        out_shape=jax.ShapeDtypeStruct((CAP * NS, 128), F32),
        grid_spec=grid_spec,
        compiler_params=_cparams(1, 48),
        name="moe_experts",
    )(blk_e, n_used, xs, w1, w3, w2)


def _lane_to_col(row):
    r = lax.broadcasted_iota(I32, (TM, TM), 0)
    c = lax.broadcasted_iota(I32, (TM, TM), 1)
    return jnp.sum(jnp.where(r == c, row, 0.0), axis=1, keepdims=True)


def _combine_kernel(final, x1_ref, pos_ref, os_ref, rw_ref, mod_ref, nfin_ref, *rest):
    n_out = 2 if final else 1
    y_refs = rest[:n_out]
    gbuf = rest[n_out]
    bufs = rest[n_out + 1:-2]
    psem, gsem = rest[-2:]
    i = pl.program_id(0)

    def pos_copies(tile, b):
        return [pltpu.make_async_copy(pos_ref.at[tile, row], bufs[b * 2 * LC + row], psem.at[b])
                for row in range(2 * LC)]

    def start_gathers(b):
        for k in range(2):
            for h in range(LC):
                def tok(t, c, k=k, h=h):
                    src = _slab(os_ref, bufs[b * 2 * LC + k * LC + h][t])
                    pltpu.make_async_copy(src, _slab(gbuf.at[2 * b + k], h * 128 + t), gsem.at[b]).start(priority=k)
                    return c
                lax.fori_loop(0, 128, tok, 0, unroll=8)

    def wait_gathers(b):
        for k in range(2):
            pltpu.make_async_copy(_slab(os_ref, 0, TM), gbuf.at[2 * b + k], gsem.at[b]).wait()

    @pl.when(i == 0)
    def _():
        for c in pos_copies(0, 0):
            c.start()
        for c in pos_copies(0, 0):
            c.wait()
        start_gathers(0)
        if NT > 1:
            for c in pos_copies(1, 1):
                c.start()

    for b in range(2):
        @pl.when(i % 2 == b)
        def _(b=b):
            @pl.when(i + 1 < NT)
            def _():
                for c in pos_copies(i + 1, 1 - b):
                    c.wait()
                start_gathers(1 - b)

            @pl.when(i + 2 < NT)
            def _():
                for c in pos_copies(i + 2, b):
                    c.start()
            wait_gathers(b)

    _, _, mrow = _tile_info(i)
    rw = rw_ref[...]
    w0 = _lane_to_col(rw[0:1])
    w1 = _lane_to_col(rw[1:2])
    g2 = _mod(mod_ref, mrow, 5)
    par = i % 2
    x2 = x1_ref[...] + g2 * (_load_slabs(gbuf.at[2 * par], TM) * w0 + _load_slabs(gbuf.at[2 * par + 1], TM) * w1)
    if final:
        yc_ref, yl_ref = y_refs
        y = _rms(x2, nfin_ref[...])

        @pl.when(i < NCT)
        def _():
            yc_ref[...] = y

        @pl.when(i >= NCT)
        def _():
            yl_ref[...] = y
    else:
        y_refs[0][...] = x2


def _combine(x1, pos, o_sorted, route_w, mod_l, norm_final, final):
    const2 = lambda i: (0, 0)
    if final:
        out_shape = [jax.ShapeDtypeStruct((T_CTX, D), F32), jax.ShapeDtypeStruct((T_LAT, D), F32)]
        out_specs = [pl.BlockSpec((TM, D), lambda i: (jnp.minimum(i, NCT - 1), 0)),
                     pl.BlockSpec((TM, D), lambda i: (jnp.maximum(i - NCT, 0), 0))]
    else:
        out_shape = [jax.ShapeDtypeStruct((T, D), F32)]
        out_specs = [pl.BlockSpec((TM, D), lambda i: (i, 0))]
    return pl.pallas_call(
        functools.partial(_combine_kernel, final),
        out_shape=out_shape,
        grid=(NT,),
        in_specs=[
            pl.BlockSpec((TM, D), lambda i: (i, 0)),
            pl.BlockSpec(memory_space=pl.ANY),
            pl.BlockSpec(memory_space=pl.ANY),
            pl.BlockSpec((None, 8, TM), lambda i: (i, 0, 0)),
            pl.BlockSpec((8, 6 * D), const2),
            pl.BlockSpec((1, D), const2),
        ],
        out_specs=out_specs,
        scratch_shapes=([pltpu.VMEM((4, TM * NS, 128), F32)] + [pltpu.SMEM((128,), I32)] * (4 * LC)
                        + [pltpu.SemaphoreType.DMA((2,)), pltpu.SemaphoreType.DMA((2,))]),
        compiler_params=_cparams(1, 32),
        name="moe_combine",
    )(x1, pos, o_sorted, route_w, mod_l, norm_final)


def _moe_layer(l, x1, h2u, route_ids, route_w, mod_l, w1, w3, w2, norm_final, final):
    route_i, counts = _rank(route_ids)
    sizes = counts[:, 0].astype(I32)
    padded = (sizes + TS - 1) // TS * TS
    pad_ends = jnp.cumsum(padded)
    pad_starts = pad_ends - padded
    n_used = (pad_ends[-1:] // TS).astype(I32)
    tile_start = jnp.arange(NST, dtype=I32) * TS
    blk_e = jnp.minimum(jnp.sum((pad_ends[None, :] <= tile_start[:, None]).astype(I32), axis=1),
                        N_EXPERTS - 1).astype(I32)
    meta = jnp.stack([pad_starts, pad_starts + sizes, pad_ends]).astype(I32)
    xs, pos = _dispatch(meta, pad_starts.astype(I32), route_i, h2u)
    o_sorted = _moe_experts(l, blk_e, n_used, xs, w1, w3, w2)
    return _combine(x1, pos, o_sorted, route_w, mod_l, norm_final, final)


def _routing_params(w_rg, b_rg, w_re, b_re):
    w = jnp.zeros((D, NRL), F32).at[:, :N_GROUPS].set(w_rg).at[:, ER0:ER0 + N_EXPERTS].set(w_re)
    hi = w.astype(BF16)
    lo = (w - hi.astype(F32)).astype(BF16)
    bias = jnp.zeros((1, NRL), F32).at[0, :N_GROUPS].set(b_rg).at[0, ER0:ER0 + N_EXPERTS].set(b_re)
    return jnp.concatenate([hi, lo], axis=1), hi, bias


def kernel(x_prompt, x_sample, cache_ckv, cache_kpe, c, c_ctx, norm_mix, norm_ffn, norm_final, w_ada, b_ada, w_pool, pool_scale, w_conv_in, conv_w, w_conv_out, w_dq, q_norm, w_uq, w_dkv, kv_norm, w_ukv, w_o, w_route_g, b_route_g, w_route_e, b_route_e, w1, w3, w2):
    x = (x_prompt.reshape(T_CTX, D), x_sample.reshape(T_LAT, D))
    cs =jnp.concatenate([c_ctx[None, :], c, jnp.zeros((8 - 1 - N_LAT_SEQ, D), F32)], axis=0)
    mod_all = _modulation(cs, w_ada, b_ada)
    nfin = norm_final[None, :]
    new_ckv = new_kpe = None
    for l in range(DEPTH):
        kind, j = l % 3, l // 3
        mod_l = mod_all[l]
        nm = norm_mix[l][None, :]
        ffn_args = (norm_ffn[l][None, :],) + _routing_params(w_route_g[l], b_route_g[l], w_route_e[l], b_route_e[l])
        if kind == 0:
            outs = _pool_layer(x, mod_l, nm, w_pool[j], pool_scale[j][None, :], ffn_args)
        elif kind == 1:
            outs = _conv_layer(x, mod_l, nm, w_conv_in[j], conv_w[j], w_conv_out[j], ffn_args)
        else:
            outs, new_ckv, new_kpe = _mla_layer(
                x, mod_l, nm, cache_ckv[:, j], cache_kpe[:, j], w_dq[j], q_norm[j][None, :], w_uq[j],
                w_dkv[j], kv_norm[j][None, :], w_ukv[j], w_o[j], ffn_args)
        x1, h2u, route_ids, route_w = outs
        ys = _moe_layer(l, x1, h2u, route_ids, route_w, mod_l, w1, w3, w2, nfin, l == DEPTH - 1)
        x = ys[0]
    y_prompt = ys[0].reshape(N_CTX_SEQ, CTX_LEN, D)
    y_sample = ys[1].reshape(N_LAT_SEQ, LAT_LEN, D)
    return (y_prompt, y_sample, new_ckv, new_kpe)
```

```python
import functools

import numpy as np
import jax
import jax.numpy as jnp
from jax import lax
from jax.experimental import pallas as pl
from jax.experimental.pallas import tpu as pltpu

F32 = jnp.float32
BF16 = jnp.bfloat16
I32 = jnp.int32

D = 1024
N_CTX_SEQ, CTX_LEN = 32, 256
N_LAT_SEQ, LAT_LEN = 4, 4096
PAST = 512
DEPTH = 4
GRID_W = 64
POOL_WINDOWS = (2, 4, 8, 16)
POOL_CH = D // 4
HEADS = 8
QK_NOPE, QK_ROPE, V_DIM = 128, 64, 128
Q_RANK, KV_RANK = 512, 256
AXIS_FREQS = QK_ROPE // 4
ROPE_BASE = 10000.0
SM_SCALE = (QK_NOPE + QK_ROPE) ** -0.5
LOG2E = 1.4426950408889634
N_GROUPS, EXP_PER_GROUP = 4, 8
N_EXPERTS = N_GROUPS * EXP_PER_GROUP
D_EXPERT = 512
EPS = 1e-6

T_CTX = N_CTX_SEQ * CTX_LEN
T_LAT = N_LAT_SEQ * LAT_LEN
T = T_CTX + T_LAT
TM = 256
NT = T // TM
NCT = T_CTX // TM
LT = LAT_LEN // TM
LT_SHIFT = LT.bit_length() - 1
HALO = 8
KB = TM + 128
NRL = 128
ER0 = 16
RG = 4
LC = TM // 128
assert 2 * LC <= 8 and NT % RG == 0
TS = 256
CAP = 2 * T + N_EXPERTS * TS
NST = CAP // TS
NS = D // 128
NK_LAT = PAST + LAT_LEN
TQ = 512
MIB = 1024 * 1024

_HI = lax.Precision.HIGHEST


def _cparams(n_axes, vmem_mib):
    return pltpu.CompilerParams(
        dimension_semantics=("arbitrary",) * n_axes,
        vmem_limit_bytes=vmem_mib * MIB)


def _rms(x, g):
    return x * lax.rsqrt(jnp.mean(x * x, axis=-1, keepdims=True) + EPS) * g


def _tile_info(i):
    is_ctx = i < NCT
    jl = jnp.maximum(i - NCT, 0)
    j = jnp.where(is_ctx, 0, jl & (LT - 1))
    mrow = jnp.where(is_ctx, 0, 1 + lax.shift_right_logical(jl, LT_SHIFT))
    return is_ctx, j, mrow


def _mod(mod_ref, mrow, k):
    return mod_ref[pl.ds(mrow, 1), k * D:(k + 1) * D]


def _store_slabs(ref, x):
    for s in range(NS):
        ref[pl.ds(s, x.shape[0], stride=NS), :] = x[:, s * 128:(s + 1) * 128]


def _load_slabs(ref, rows):
    return jnp.concatenate([ref[pl.ds(s, rows, stride=NS), :] for s in range(NS)], axis=1)


def _dot_nt(a, b):
    return lax.dot_general(a, b, (((1,), (1,)), ((), ())), preferred_element_type=F32)


def _ffn_pre(x1, mrow, mod_ref, nf_ref, wrc_ref, wrh_ref, rb_ref, h2u_ref, ri_ref, rw_ref):
    h2 = _rms(x1, nf_ref[...]) * (1.0 + _mod(mod_ref, mrow, 4)) + _mod(mod_ref, mrow, 3)
    h_hi = h2.astype(BF16)
    h_lo = (h2 - h_hi.astype(F32)).astype(BF16)
    _store_slabs(h2u_ref, h2)

    hw = jnp.dot(h_hi, wrc_ref[...], preferred_element_type=F32)
    lw = jnp.dot(h_lo, wrh_ref[...], preferred_element_type=F32)
    lt = jnp.transpose(hw[:, :NRL] + hw[:, NRL:] + lw + rb_ref[...])
    row8 = lax.broadcasted_iota(I32, (8, TM), 0)
    gl = jnp.where(row8 < N_GROUPS, lt[0:8], -jnp.inf)
    ge = jnp.exp(gl - jnp.max(gl, axis=0, keepdims=True))
    gprob = ge / jnp.sum(ge, axis=0, keepdims=True)
    g_p = jnp.max(gprob, axis=0, keepdims=True)
    g_idx = jnp.min(jnp.where(gprob == g_p, row8, 8), axis=0, keepdims=True)
    e_sel = lt[ER0:ER0 + 8]
    for g in range(1, N_GROUPS):
        e_sel = jnp.where(g_idx == g, lt[ER0 + 8 * g:ER0 + 8 * g + 8], e_sel)
    ee = jnp.exp(e_sel - jnp.max(e_sel, axis=0, keepdims=True))
    eprob = ee / jnp.sum(ee, axis=0, keepdims=True)
    p0 = jnp.max(eprob, axis=0, keepdims=True)
    i0 = jnp.min(jnp.where(eprob == p0, row8, 8), axis=0, keepdims=True)
    rest = jnp.where(row8 == i0, -1.0, eprob)
    p1 = jnp.max(rest, axis=0, keepdims=True)
    i1 = jnp.min(jnp.where(rest == p1, row8, 8), axis=0, keepdims=True)
    psum = p0 + p1
    w0 = g_p * p0 / psum
    w1 = g_p * p1 / psum
    id0 = g_idx * EXP_PER_GROUP + i0
    id1 = g_idx * EXP_PER_GROUP + i1

    ri_ref[...] = jnp.where(row8 == 0, id0, jnp.where(row8 == 1, id1, 0))
    rw_ref[...] = jnp.where(row8 == 0, w0, jnp.where(row8 == 1, w1, 0.0))


def _ffn_in_specs():
    const2 = lambda i: (0, 0)
    return [
        pl.BlockSpec((1, D), const2),
        pl.BlockSpec((D, 2 * NRL), const2),
        pl.BlockSpec((D, NRL), const2),
        pl.BlockSpec((1, NRL), const2),
    ]


def _ffn_out_shapes():
    return [
        jax.ShapeDtypeStruct((T, D), F32),
        jax.ShapeDtypeStruct((T * NS, 128), F32),
        jax.ShapeDtypeStruct((NT, 8, TM), I32),
        jax.ShapeDtypeStruct((NT, 8, TM), F32),
    ]


def _ffn_out_specs():
    return [
        pl.BlockSpec((TM, D), lambda i: (i, 0)),
        pl.BlockSpec((TM * NS, 128), lambda i: (i, 0)),
        pl.BlockSpec((None, 8, TM), lambda i: (i, 0, 0)),
        pl.BlockSpec((None, 8, TM), lambda i: (i, 0, 0)),
    ]


def _mod_kernel(cs_ref, w_ref, b_ref, o_ref):
    s = cs_ref[...]
    a = s * jax.nn.sigmoid(s)
    o_ref[...] = jnp.dot(a, w_ref[...], precision=_HI, preferred_element_type=F32) + b_ref[...]


def _modulation(cs, w_ada, b_ada):
    nb = 6
    return pl.pallas_call(
        _mod_kernel,
        out_shape=jax.ShapeDtypeStruct((DEPTH, 8, 6 * D), F32),
        grid=(DEPTH, nb),
        in_specs=[
            pl.BlockSpec((8, D), lambda l, n: (0, 0)),
            pl.BlockSpec((None, D, D), lambda l, n: (l, 0, n)),
            pl.BlockSpec((None, 1, D), lambda l, n: (l, 0, n)),
        ],
        out_specs=pl.BlockSpec((None, 8, D), lambda l, n: (l, 0, n)),
        compiler_params=_cparams(2, 32),
        name="modulation",
    )(cs, w_ada, b_ada.reshape(DEPTH, 1, 6 * D))


def _halo_specs(width):
    nb = T // HALO
    per = TM // HALO
    return [
        pl.BlockSpec((TM, width), lambda i: (i, 0)),
        pl.BlockSpec((HALO, width), lambda i: (jnp.maximum(i * per - 1, 0), 0)),
        pl.BlockSpec((HALO, width), lambda i: (jnp.minimum((i + 1) * per, nb - 1), 0)),
    ]


def _normed_halo(xt, xp, xn, i, mod_ref, nm_ref):
    is_ctx, j, mrow = _tile_info(i)
    g = nm_ref[...]
    sc = 1.0 + _mod(mod_ref, mrow, 1)
    sh = _mod(mod_ref, mrow, 0)
    pv = jnp.where(jnp.logical_and(jnp.logical_not(is_ctx), j > 0), 1.0, 0.0)
    nv = jnp.where(jnp.logical_and(jnp.logical_not(is_ctx), j < LT - 1), 1.0, 0.0)
    ht = _rms(xt, g) * sc + sh
    hp = (_rms(xp, g) * sc + sh) * pv
    hn = (_rms(xn, g) * sc + sh) * nv
    return ht, hp, hn, is_ctx, j, mrow


def _pool_kernel(split, *refs):
    if split:
        xc_ref, x_ref, xp_ref, xn_ref = refs[:4]
        refs = refs[4:]
    else:
        x_ref, xp_ref, xn_ref = refs[:3]
        refs = refs[3:]
    (mod_ref, nm_ref, band_ref, wp_ref, ps_ref, nf_ref, wrc_ref, wrh_ref, rb_ref,
     x1_ref, h2u_ref, ri_ref, rw_ref) = refs
    i = pl.program_id(0)
    x = x_ref[...]
    if split:
        x = jnp.where(i < NCT, xc_ref[...], x)
    ht, hp, hn, is_ctx, j, mrow = _normed_halo(x, xp_ref[...], xn_ref[...], i, mod_ref, nm_ref)
    hext = jnp.concatenate([ht, hp, hn, jnp.zeros((KB - TM - 2 * HALO, D), F32)], axis=0)
    e_hi = hext.astype(BF16)
    e_lo = (hext - e_hi.astype(F32)).astype(BF16)
    seq_len = jnp.where(is_ctx, CTX_LEN, LAT_LEN)
    t = j * TM + lax.broadcasted_iota(I32, (TM, POOL_CH), 0)
    outs = []
    for g, win in enumerate(POOL_WINDOWS):
        lo = win // 2
        hi = win - 1 - lo
        cols = slice(g * POOL_CH, (g + 1) * POOL_CH)
        band = band_ref[g]
        s = (jnp.dot(band, e_hi[:, cols], preferred_element_type=F32)
             + jnp.dot(band, e_lo[:, cols], preferred_element_type=F32))
        cnt = (jnp.minimum(t + hi, seq_len - 1) - jnp.maximum(t - lo, 0) + 1).astype(F32)
        d = s / cnt - ht[:, cols]
        outs.append(jnp.dot(d.astype(BF16), wp_ref[g], preferred_element_type=F32))
    o = jnp.concatenate(outs, axis=1) * ps_ref[...]
    x1 = x + _mod(mod_ref, mrow, 2) * o
    x1_ref[...] = x1
    _ffn_pre(x1, mrow, mod_ref, nf_ref, wrc_ref, wrh_ref, rb_ref, h2u_ref, ri_ref, rw_ref)


def _pool_band():
    band = np.zeros((len(POOL_WINDOWS), TM, KB), np.float32)
    r = np.arange(TM)[:, None]
    for g, win in enumerate(POOL_WINDOWS):
        lo = win // 2
        hi = win - 1 - lo
        pos = np.concatenate([np.arange(TM), np.arange(-HALO, 0), np.arange(TM, TM + HALO)])[None, :]
        band[g, :, :TM + 2 * HALO] = (pos >= r - lo) & (pos <= r + hi)
    return jnp.asarray(band, BF16)


def _split_halo_specs():
    nb = T_LAT // HALO
    per = TM // HALO
    lat = lambda i: jnp.maximum(i - NCT, 0)
    return [
        pl.BlockSpec((TM, D), lambda i: (jnp.minimum(i, NCT - 1), 0)),
        pl.BlockSpec((TM, D), lambda i: (lat(i), 0)),
        pl.BlockSpec((HALO, D), lambda i: (jnp.maximum(lat(i) * per - 1, 0), 0)),
        pl.BlockSpec((HALO, D), lambda i: (jnp.minimum((lat(i) + 1) * per, nb - 1), 0)),
    ]


def _pool_layer(xs, mod_l, nm, wp, ps, ffn_args):
    const2 = lambda i: (0, 0)
    const3 = lambda i: (0, 0, 0)
    split = isinstance(xs, tuple)
    x_args = (xs[0], xs[1], xs[1], xs[1]) if split else (xs, xs, xs)
    return pl.pallas_call(
        functools.partial(_pool_kernel, split),
        out_shape=_ffn_out_shapes(),
        grid=(NT,),
        in_specs=(_split_halo_specs() if split else _halo_specs(D)) + [
            pl.BlockSpec((8, 6 * D), const2),
            pl.BlockSpec((1, D), const2),
            pl.BlockSpec((len(POOL_WINDOWS), TM, KB), const3),
            pl.BlockSpec((len(POOL_WINDOWS), POOL_CH, POOL_CH), const3),
            pl.BlockSpec((1, D), const2),
        ] + _ffn_in_specs(),
        out_specs=_ffn_out_specs(),
        compiler_params=_cparams(1, 48),
        name="pool_mixer",
    )(*x_args, mod_l, nm, _pool_band(), wp.astype(BF16), ps, *ffn_args)


def _conv_kernel(x_ref, xp_ref, xn_ref, mod_ref, nm_ref, win_ref, cw_ref, wout_ref,
                 nf_ref, wrc_ref, wrh_ref, rb_ref,
                 x1_ref, h2u_ref, ri_ref, rw_ref):
    i = pl.program_id(0)
    x = x_ref[...]
    ht, hp, hn, is_ctx, j, mrow = _normed_halo(x, xp_ref[...], xn_ref[...], i, mod_ref, nm_ref)
    hext = jnp.concatenate([ht, hp, hn], axis=0).astype(BF16)
    bcv = jnp.dot(hext, win_ref[...], preferred_element_type=F32)
    b = bcv[:TM, :D]
    u = bcv[:, D:2 * D] * bcv[:, 2 * D:]
    um = u[:TM]
    u_before = u[TM + HALO - 1:TM + HALO]
    u_after = u[TM + HALO:TM + HALO + 1]
    row = lax.broadcasted_iota(I32, (TM, D), 0)
    up = jnp.where(row == 0, u_before, pltpu.roll(um, 1, 0))
    un = jnp.where(row == TM - 1, u_after, pltpu.roll(um, TM - 1, 0))
    cw = cw_ref[...]
    conv = up * cw[0:1] + um * cw[1:2] + un * cw[2:3]
    o = jnp.dot((b * conv).astype(BF16), wout_ref[...], preferred_element_type=F32)
    x1 = x + _mod(mod_ref, mrow, 2) * o
    x1_ref[...] = x1
    _ffn_pre(x1, mrow, mod_ref, nf_ref, wrc_ref, wrh_ref, rb_ref, h2u_ref, ri_ref, rw_ref)


def _conv_layer(x, mod_l, nm, w_in, cw, w_out, ffn_args):
    const2 = lambda i: (0, 0)
    cw8 = jnp.concatenate([cw, jnp.zeros((8 - cw.shape[0], D), F32)], axis=0)
    return pl.pallas_call(
        _conv_kernel,
        out_shape=_ffn_out_shapes(),
        grid=(NT,),
        in_specs=_halo_specs(D) + [
            pl.BlockSpec((8, 6 * D), const2),
            pl.BlockSpec((1, D), const2),
            pl.BlockSpec((D, 3 * D), const2),
            pl.BlockSpec((8, D), const2),
            pl.BlockSpec((D, D), const2),
        ] + _ffn_in_specs(),
        out_specs=_ffn_out_specs(),
        compiler_params=_cparams(1, 56),
        name="conv_mixer",
    )(x, x, x, mod_l, nm, w_in.astype(BF16), cw8, w_out.astype(BF16), *ffn_args)


W_CAT = Q_RANK + KV_RANK + 2 * QK_ROPE
QH = 2 * QK_NOPE


def _mla_proj_kernel(x_ref, mod_ref, nm_ref, wcat_ref, qn_ref, kvn_ref, wq_ref, wqs_ref,
                     c_ref, s_ref, q_ref, ckv_ref, kpe_ref):
    i = pl.program_id(0)
    _, _, mrow = _tile_info(i)
    h = _rms(x_ref[...], nm_ref[...]) * (1.0 + _mod(mod_ref, mrow, 1)) + _mod(mod_ref, mrow, 0)
    y = jnp.dot(h.astype(BF16), wcat_ref[...], preferred_element_type=F32)
    cqn = _rms(y[:, :Q_RANK], qn_ref[...]).astype(BF16)
    q = jnp.dot(cqn, wq_ref[...], preferred_element_type=F32)
    qs = jnp.dot(cqn, wqs_ref[...], preferred_element_type=F32)
    cos = c_ref[...]
    sin = s_ref[...]
    parts = []
    for hh in range(HEADS):
        parts.append(q[:, hh * QH:hh * QH + QK_NOPE])
        parts.append(q[:, hh * QH + QK_NOPE:(hh + 1) * QH] * cos + qs[:, hh * 128:(hh + 1) * 128] * sin)
    q_ref[...] = (jnp.concatenate(parts, axis=1) * (SM_SCALE * LOG2E)).astype(BF16)
    ckv_ref[...] = _rms(y[:, Q_RANK:Q_RANK + KV_RANK], kvn_ref[...])
    k2 = y[:, Q_RANK + KV_RANK:]
    kr = k2 * cos + pltpu.roll(k2, QK_ROPE, 1) * sin
    lane = lax.broadcasted_iota(I32, (TM, 2 * QK_ROPE), 1)
    kpe_ref[...] = jnp.where(lane < QK_ROPE, kr, 0.0)


def _rope_tables():
    rows_n = LAT_LEN // GRID_W
    rows = jnp.repeat(jnp.arange(rows_n), GRID_W).astype(F32)
    cols = jnp.tile(jnp.arange(GRID_W), rows_n).astype(F32)
    inv = ROPE_BASE ** (-(jnp.arange(AXIS_FREQS, dtype=F32) / AXIS_FREQS))
    ang = jnp.stack([rows[:, None] * inv, cols[:, None] * inv], axis=1)
    cos, sin = jnp.cos(ang), jnp.sin(ang)
    c64 = jnp.concatenate([cos[:, 0], cos[:, 0], cos[:, 1], cos[:, 1]], axis=1)
    s64 = jnp.concatenate([-sin[:, 0], sin[:, 0], -sin[:, 1], sin[:, 1]], axis=1)
    c = jnp.concatenate([c64, jnp.ones((LAT_LEN, QK_ROPE), F32)], axis=1)
    s = jnp.concatenate([s64, jnp.zeros((LAT_LEN, QK_ROPE), F32)], axis=1)
    c = jnp.concatenate([c, jnp.ones((TM, 2 * QK_ROPE), F32)], axis=0)
    s = jnp.concatenate([s, jnp.zeros((TM, 2 * QK_ROPE), F32)], axis=0)
    return c, s


def _swap_rope_cols(w):
    f = AXIS_FREQS
    return jnp.concatenate([w[..., f:2 * f], w[..., :f], w[..., 3 * f:], w[..., 2 * f:3 * f]], axis=-1)


def _mla_proj(x, mod_l, nm, w_dq, q_norm, w_uq, w_dkv, kv_norm):
    const2 = lambda i: (0, 0)
    w_kpe = w_dkv[:, KV_RANK:]
    wcat = jnp.concatenate([w_dq, w_dkv[:, :KV_RANK], w_kpe, _swap_rope_cols(w_kpe)], axis=1).astype(BF16)
    wq = w_uq.reshape(Q_RANK, HEADS, QK_NOPE + QK_ROPE)
    zpad = jnp.zeros((Q_RANK, HEADS, QK_ROPE), F32)
    wq_a = jnp.concatenate([wq, zpad], axis=2).reshape(Q_RANK, HEADS * QH).astype(BF16)
    wq_s = jnp.concatenate([_swap_rope_cols(wq[:, :, QK_NOPE:]), zpad], axis=2)
    wq_s = wq_s.reshape(Q_RANK, HEADS * 128).astype(BF16)
    cos, sin = _rope_tables()
    tab_idx = lambda i: (jnp.where(i < NCT, LT, jnp.maximum(i - NCT, 0) & (LT - 1)), 0)
    return pl.pallas_call(
        _mla_proj_kernel,
        out_shape=[jax.ShapeDtypeStruct((T, HEADS * QH), BF16),
                   jax.ShapeDtypeStruct((T, KV_RANK), F32),
                   jax.ShapeDtypeStruct((T, 2 * QK_ROPE), F32)],
        grid=(NT,),
        in_specs=[
            pl.BlockSpec((TM, D), lambda i: (i, 0)),
            pl.BlockSpec((8, 6 * D), const2),
            pl.BlockSpec((1, D), const2),
            pl.BlockSpec((D, W_CAT), const2),
            pl.BlockSpec((1, Q_RANK), const2),
            pl.BlockSpec((1, KV_RANK), const2),
            pl.BlockSpec((Q_RANK, HEADS * QH), const2),
            pl.BlockSpec((Q_RANK, HEADS * 128), const2),
            pl.BlockSpec((TM, 2 * QK_ROPE), tab_idx),
            pl.BlockSpec((TM, 2 * QK_ROPE), tab_idx),
        ],
        out_specs=[pl.BlockSpec((TM, HEADS * QH), lambda i: (i, 0)),
                   pl.BlockSpec((TM, KV_RANK), lambda i: (i, 0)),
                   pl.BlockSpec((TM, 2 * QK_ROPE), lambda i: (i, 0))],
        compiler_params=_cparams(1, 48),
        name="mla_proj",
    )(x, mod_l, nm, wcat, q_norm, kv_norm, wq_a, wq_s, cos, sin)


def _kv_expand_kernel(ckv_ref, kpe_ref, wk_ref, wv_ref, k_ref, v_ref):
    c = ckv_ref[...].astype(BF16)
    kn = jnp.dot(c, wk_ref[...], preferred_element_type=F32).astype(BF16)
    kp = kpe_ref[...].astype(BF16)
    parts = []
    for hh in range(HEADS):
        parts.append(kn[:, hh * QK_NOPE:(hh + 1) * QK_NOPE])
        parts.append(kp)
    k_ref[...] = jnp.concatenate(parts, axis=1)
    v_ref[...] = jnp.dot(c, wv_ref[...], preferred_element_type=F32).astype(BF16)


def _kv_expand(ckv, kpe, wk, wv, name):
    n = ckv.shape[0]
    const2 = lambda i: (0, 0)
    return pl.pallas_call(
        _kv_expand_kernel,
        out_shape=[jax.ShapeDtypeStruct((n, HEADS * QH), BF16),
                   jax.ShapeDtypeStruct((n, HEADS * V_DIM), BF16)],
        grid=(n // TM,),
        in_specs=[
            pl.BlockSpec((TM, KV_RANK), lambda i: (i, 0)),
            pl.BlockSpec((TM, 2 * QK_ROPE), lambda i: (i, 0)),
            pl.BlockSpec((KV_RANK, HEADS * QK_NOPE), const2),
            pl.BlockSpec((KV_RANK, HEADS * V_DIM), const2),
        ],
        out_specs=[pl.BlockSpec((TM, HEADS * QH), lambda i: (i, 0)),
                   pl.BlockSpec((TM, HEADS * V_DIM), lambda i: (i, 0))],
        compiler_params=_cparams(1, 32),
        name=name,
    )(ckv, kpe, wk, wv)


def _attend(q, k, v):
    s = _dot_nt(q, k)
    p = jnp.exp2(s - jnp.max(s, axis=1, keepdims=True))
    l = jnp.sum(p, axis=1, keepdims=True)
    return jnp.dot(p.astype(BF16), v, preferred_element_type=F32) / l


KC = 512
NKC = NK_LAT // KC


def _attn_lat_kernel(q_ref, k_ref, v_ref, o_ref, s_ref, m_ref, l_ref, acc_ref):
    q = q_ref[...]
    m_ref[...] = jnp.full((TQ, 128), -jnp.inf, F32)

    def scores(c, carry):
        k = k_ref[pl.ds(pl.multiple_of(c * KC, KC), KC), :]
        s = _dot_nt(q, k)
        s_ref[c] = s
        m = m_ref[...]
        for j in range(KC // 128):
            m = jnp.maximum(m, s[:, j * 128:(j + 1) * 128])
        m_ref[...] = m
        return carry
    lax.fori_loop(0, NKC, scores, 0, unroll=True)

    mb = jnp.broadcast_to(jnp.max(m_ref[...], axis=1, keepdims=True), (TQ, 128))
    l_ref[...] = jnp.zeros((TQ, 128), F32)
    acc_ref[...] = jnp.zeros((TQ, V_DIM), F32)

    def weighted(c, carry):
        s = s_ref[c]
        ps = [jnp.exp2(s[:, j * 128:(j + 1) * 128] - mb) for j in range(KC // 128)]
        l = l_ref[...]
        for pj in ps:
            l = l + pj
        l_ref[...] = l
        p = jnp.concatenate(ps, axis=1).astype(BF16)
        v = v_ref[pl.ds(pl.multiple_of(c * KC, KC), KC), :]
        acc_ref[...] += jnp.dot(p, v, preferred_element_type=F32)
        return carry
    lax.fori_loop(0, NKC, weighted, 0, unroll=3)

    o_ref[...] = (acc_ref[...] / jnp.sum(l_ref[...], axis=1, keepdims=True)).astype(BF16)


def _attn_lat(q, k, v):
    nq = LAT_LEN // TQ
    q0 = T_CTX // TQ
    return pl.pallas_call(
        _attn_lat_kernel,
        scratch_shapes=[pltpu.VMEM((NKC, TQ, KC), F32), pltpu.VMEM((TQ, 128), F32),
                        pltpu.VMEM((TQ, 128), F32), pltpu.VMEM((TQ, V_DIM), F32)],
        out_shape=jax.ShapeDtypeStruct((T_LAT, HEADS * V_DIM), BF16),
        grid=(N_LAT_SEQ, HEADS, nq),
        in_specs=[
            pl.BlockSpec((TQ, QH), lambda b, h, t: (q0 + b * nq + t, h)),
            pl.BlockSpec((NK_LAT, QH), lambda b, h, t: (b, h)),
            pl.BlockSpec((NK_LAT, V_DIM), lambda b, h, t: (b, h)),
        ],
        out_specs=pl.BlockSpec((TQ, V_DIM), lambda b, h, t: (b * nq + t, h)),
        compiler_params=_cparams(3, 48),
        name="attn_latent",
    )(q, k, v)


def _attn_ctx_kernel(q_ref, k_ref, v_ref, o_ref):
    outs = []
    for hh in range(HEADS):
        outs.append(_attend(q_ref[:, hh * QH:(hh + 1) * QH], k_ref[:, hh * QH:(hh + 1) * QH],
                            v_ref[:, hh * V_DIM:(hh + 1) * V_DIM]))
    o_ref[...] = jnp.concatenate(outs, axis=1).astype(BF16)


def _attn_ctx(q, k, v):
    return pl.pallas_call(
        _attn_ctx_kernel,
        out_shape=jax.ShapeDtypeStruct((T_CTX, HEADS * V_DIM), BF16),
        grid=(N_CTX_SEQ,),
        in_specs=[
            pl.BlockSpec((CTX_LEN, HEADS * QH), lambda b: (b, 0)),
            pl.BlockSpec((CTX_LEN, HEADS * QH), lambda b: (b, 0)),
            pl.BlockSpec((CTX_LEN, HEADS * V_DIM), lambda b: (b, 0)),
        ],
        out_specs=pl.BlockSpec((CTX_LEN, HEADS * V_DIM), lambda b: (b, 0)),
        compiler_params=_cparams(1, 32),
        name="attn_context",
    )(q, k, v)


def _attn_out_kernel(x_ref, oc_ref, ol_ref, mod_ref, wo_ref,
                     nf_ref, wrc_ref, wrh_ref, rb_ref,
                     x1_ref, h2u_ref, ri_ref, rw_ref):
    i = pl.program_id(0)
    is_ctx, _, mrow = _tile_info(i)
    att = jnp.where(is_ctx, oc_ref[...], ol_ref[...])
    o = jnp.dot(att, wo_ref[...], preferred_element_type=F32)
    x1 = x_ref[...] + _mod(mod_ref, mrow, 2) * o
    x1_ref[...] = x1
    _ffn_pre(x1, mrow, mod_ref, nf_ref, wrc_ref, wrh_ref, rb_ref, h2u_ref, ri_ref, rw_ref)


def _attn_out(x, o_ctx, o_lat, mod_l, w_o, ffn_args):
    const2 = lambda i: (0, 0)
    return pl.pallas_call(
        _attn_out_kernel,
        out_shape=_ffn_out_shapes(),
        grid=(NT,),
        in_specs=[
            pl.BlockSpec((TM, D), lambda i: (i, 0)),
            pl.BlockSpec((TM, D), lambda i: (jnp.minimum(i, NCT - 1), 0)),
            pl.BlockSpec((TM, D), lambda i: (jnp.maximum(i - NCT, 0), 0)),
            pl.BlockSpec((8, 6 * D), const2),
            pl.BlockSpec((D, D), const2),
        ] + _ffn_in_specs(),
        out_specs=_ffn_out_specs(),
        compiler_params=_cparams(1, 48),
        name="attn_out",
    )(x, o_ctx, o_lat, mod_l, w_o.astype(BF16), *ffn_args)


def _mla_layer(x, mod_l, nm, cache_ckv, cache_kpe, w_dq, q_norm, w_uq, w_dkv, kv_norm, w_ukv, w_o, ffn_args):
    q, ckv, kpe = _mla_proj(x, mod_l, nm, w_dq, q_norm, w_uq, w_dkv, kv_norm)
    wkv = w_ukv.reshape(KV_RANK, HEADS, QK_NOPE + V_DIM)
    wk = wkv[:, :, :QK_NOPE].reshape(KV_RANK, HEADS * QK_NOPE).astype(BF16)
    wv = wkv[:, :, QK_NOPE:].reshape(KV_RANK, HEADS * V_DIM).astype(BF16)
    k_c, v_c = _kv_expand(ckv[:T_CTX], kpe[:T_CTX], wk, wv, "kv_expand_context")
    kpe_cache = jnp.concatenate([cache_kpe, jnp.zeros_like(cache_kpe)], axis=-1)
    ckv_l = jnp.concatenate([cache_ckv, ckv[T_CTX:].reshape(N_LAT_SEQ, LAT_LEN, KV_RANK)], axis=1)
    kpe_l = jnp.concatenate([kpe_cache, kpe[T_CTX:].reshape(N_LAT_SEQ, LAT_LEN, 2 * QK_ROPE)], axis=1)
    k_l, v_l = _kv_expand(ckv_l.reshape(N_LAT_SEQ * NK_LAT, KV_RANK),
                          kpe_l.reshape(N_LAT_SEQ * NK_LAT, 2 * QK_ROPE), wk, wv, "kv_expand_latent")
    o_c = _attn_ctx(q, k_c, v_c)
    o_l = _attn_lat(q, k_l, v_l)
    outs = _attn_out(x, o_c, o_l, mod_l, w_o, ffn_args)
    new_ckv = ckv[:T_CTX].reshape(N_CTX_SEQ, 1, CTX_LEN, KV_RANK)
    new_kpe = kpe[:T_CTX, :QK_ROPE].reshape(N_CTX_SEQ, 1, CTX_LEN, QK_ROPE)
    return outs, new_ckv, new_kpe


GROUP = 8


def _slab(ref, row, n=1):
    return ref.at[pl.ds(pl.multiple_of(row * NS, NS), n * NS)]


def _dispatch_kernel(meta_ref, psv_ref, ri_ref, h2_ref, xs_ref, pos_ref, stage, zbuf, hbuf, *rest):
    bufs, (sem, dsem, zsem, hsem) = rest[:-4], rest[-4:]
    i = pl.program_id(0)

    def row_copies(b):
        return [pltpu.make_async_copy(stage.at[b, row], bufs[b * 2 * LC + row], sem.at[b])
                for row in range(2 * LC)]

    def tile_fetch(tile):
        return pltpu.make_async_copy(_slab(h2_ref, tile * TM, TM), hbuf.at[tile % 3], hsem.at[tile % 3])

    def tile_rows_done(b):
        return pltpu.make_async_copy(_slab(h2_ref, 0, 2 * TM), _slab(xs_ref, 0, 2 * TM), dsem.at[b])

    @pl.when(i == 0)
    def _():
        zbuf[...] = jnp.zeros(zbuf.shape, F32)

        def pad_fill(e):
            return pltpu.make_async_copy(zbuf, _slab(xs_ref, meta_ref[2, e] - TS, TS), zsem)

        def start(e, c):
            @pl.when(meta_ref[2, e] > meta_ref[0, e])
            def _():
                pad_fill(e).start()
            return c

        def wait(e, c):
            @pl.when(meta_ref[2, e] > meta_ref[0, e])
            def _():
                pad_fill(e).wait()
            return c
        lax.fori_loop(0, N_EXPERTS, start, 0)
        lax.fori_loop(0, N_EXPERTS, wait, 0)

        def tail_fill(tile):
            return pltpu.make_async_copy(zbuf, _slab(xs_ref, tile * TS, TS), zsem)

        def tail_start(tile, c):
            tail_fill(tile).start()
            return c

        def tail_wait(tile, c):
            tail_fill(tile).wait()
            return c
        first_unused = lax.shift_right_logical(meta_ref[2, N_EXPERTS - 1], jnp.int32(TS.bit_length() - 1))
        lax.fori_loop(first_unused, NST, tail_start, 0)
        lax.fori_loop(first_unused, NST, tail_wait, 0)

    @pl.when(i < NT)
    def _():
        tile_fetch(i).start()
        ri = ri_ref[...]
        rowe = lax.broadcasted_iota(I32, (N_EXPERTS, TM), 0)
        ps = psv_ref[...]
        pos0 = jnp.sum(jnp.where(rowe == ri[0:1], ps, 0), axis=0, keepdims=True) + ri[2:3]
        pos1 = jnp.sum(jnp.where(rowe == ri[1:2], ps, 0), axis=0, keepdims=True) + ri[3:4]
        row8 = lax.broadcasted_iota(I32, (8, 128), 0)
        st = jnp.zeros((8, 128), I32)
        for k, pos in enumerate((pos0, pos1)):
            for h in range(LC):
                st = jnp.where(row8 == k * LC + h, pos[:, h * 128:(h + 1) * 128], st)
        pos_ref[...] = st
        for b in range(2):
            @pl.when(i % 2 == b)
            def _(b=b):
                stage[b] = st
                for c in row_copies(b):
                    c.start()

    j = i - 1
    for b in range(2):
        @pl.when(jnp.logical_and(i >= 1, j % 2 == b))
        def _(b=b):
            for c in row_copies(b):
                c.wait()
            tile_fetch(j).wait()
            rows = hbuf.at[j % 3]
            for h in range(LC):
                def toks(g, c, h=h):
                    t0 = g * GROUP
                    slots = [[bufs[b * 2 * LC + k * LC + h][t0 + u] for k in range(2)] for u in range(GROUP)]
                    for u in range(GROUP):
                        src = _slab(rows, h * 128 + t0 + u)
                        for k in range(2):
                            pltpu.make_async_copy(src, _slab(xs_ref, slots[u][k]), dsem.at[b]).start(priority=k)
                    return c
                lax.fori_loop(0, 128 // GROUP, toks, 0)

            @pl.when(j >= 1)
            def _():
                tile_rows_done(1 - b).wait()

            @pl.when(i == NT)
            def _():
                tile_rows_done(b).wait()


def _rank_kernel(ri_ref, trio_ref, ro_ref, cnt_ref, carry_ref):
    @pl.when(pl.program_id(0) == 0)
    def _():
        carry_ref[...] = jnp.zeros_like(carry_ref)

    rowe = lax.broadcasted_iota(I32, (N_EXPERTS, TM), 0)
    row8 = lax.broadcasted_iota(I32, (8, TM), 0)
    ids = [ri_ref[tt] for tt in range(RG)]
    ohs = [rowe == ids[tt][k:k + 1] for tt in range(RG) for k in range(2)]
    ohb = jnp.concatenate([jnp.where(oh, 1.0, 0.0).astype(BF16) for oh in ohs], axis=0)
    pt = jnp.dot(ohb, trio_ref[...], preferred_element_type=F32)
    carry = carry_ref[...]
    for tt in range(RG):
        ranks = []
        for k in range(2):
            r0 = (tt * 2 + k) * N_EXPERTS
            before = carry + pt[r0:r0 + N_EXPERTS, :TM]
            ranks.append(jnp.sum(jnp.where(ohs[tt * 2 + k], before, 0.0), axis=0, keepdims=True).astype(I32))
            carry = carry + pt[r0:r0 + N_EXPERTS, TM:]
        ro_ref[tt] = jnp.where(row8 == 0, ids[tt][0:1],
                     jnp.where(row8 == 1, ids[tt][1:2],
                     jnp.where(row8 == 2, ranks[0],
                     jnp.where(row8 == 3, ranks[1], 0))))
    carry_ref[...] = carry
    cnt_ref[...] = carry


def _rank(route_i):
    tri = np.triu(np.ones((TM, TM), np.float32), 1)
    trio = jnp.asarray(np.concatenate([tri, np.ones((TM, TM), np.float32)], axis=1), BF16)
    return pl.pallas_call(
        _rank_kernel,
        out_shape=[jax.ShapeDtypeStruct((NT, 8, TM), I32), jax.ShapeDtypeStruct((N_EXPERTS, TM), F32)],
        grid=(NT // RG,),
        in_specs=[pl.BlockSpec((RG, 8, TM), lambda g: (g, 0, 0)),
                  pl.BlockSpec((TM, 2 * TM), lambda g: (0, 0))],
        out_specs=[pl.BlockSpec((RG, 8, TM), lambda g: (g, 0, 0)),
                   pl.BlockSpec((N_EXPERTS, TM), lambda g: (0, 0))],
        scratch_shapes=[pltpu.VMEM((N_EXPERTS, TM), F32)],
        compiler_params=_cparams(1, 16),
        name="expert_rank",
    )(route_i, trio)


def _dispatch(meta, pad_starts, route_i, h2):
    tile = lambda i: (jnp.minimum(i, NT - 1), 0, 0)
    return pl.pallas_call(
        _dispatch_kernel,
        out_shape=[jax.ShapeDtypeStruct((CAP * NS, 128), F32),
                   jax.ShapeDtypeStruct((NT, 8, 128), I32)],
        grid=(NT + 1,),
        in_specs=[pl.BlockSpec(memory_space=pltpu.SMEM),
                  pl.BlockSpec((N_EXPERTS, TM), lambda i: (0, 0)),
                  pl.BlockSpec((None, 8, TM), tile),
                  pl.BlockSpec(memory_space=pl.ANY)],
        out_specs=[pl.BlockSpec(memory_space=pl.ANY), pl.BlockSpec((None, 8, 128), tile)],
        scratch_shapes=([pltpu.VMEM((2, 8, 128), I32), pltpu.VMEM((TS * NS, 128), F32),
                         pltpu.VMEM((3, TM * NS, 128), F32)]
                        + [pltpu.SMEM((128,), I32)] * (4 * LC)
                        + [pltpu.SemaphoreType.DMA((2,)), pltpu.SemaphoreType.DMA((2,)), pltpu.SemaphoreType.DMA,
                           pltpu.SemaphoreType.DMA((3,))]),
        compiler_params=_cparams(1, 16),
        name="moe_dispatch",
    )(meta, jnp.broadcast_to(pad_starts[:, None], (N_EXPERTS, TM)), route_i, h2)


def _moe_kernel(blk_ref, nu_ref, xs_ref, w1_ref, w3_ref, w2_ref, o_ref, wb1, wb3, wb2, pe_ref):
    i = pl.program_id(0)
    n_used = nu_ref[0]

    @pl.when(i == 0)
    def _():
        pe_ref[0] = -1

    @pl.when(i >= n_used)
    def _():
        o_ref[...] = jnp.zeros(o_ref.shape, F32)

    @pl.when(i < n_used)
    def _():
        e = blk_ref[i]

        @pl.when(e != pe_ref[0])
        def _():
            wb1[...] = w1_ref[...].astype(BF16)
            wb3[...] = w3_ref[...].astype(BF16)
            wb2[...] = w2_ref[...].astype(BF16)
            pe_ref[0] = e

        xb = _load_slabs(xs_ref, TS).astype(BF16)
        a = jnp.dot(xb, wb1[...], preferred_element_type=F32)
        b = jnp.dot(xb, wb3[...], preferred_element_type=F32)
        hm = (a * jax.nn.sigmoid(a) * b).astype(BF16)
        _store_slabs(o_ref, jnp.dot(hm, wb2[...], preferred_element_type=F32))


def _moe_experts(l, blk_e, n_used, xs, w1, w3, w2):
    grid_spec = pltpu.PrefetchScalarGridSpec(
        num_scalar_prefetch=2,
        grid=(NST,),
        in_specs=[
            pl.BlockSpec((TS * NS, 128), lambda i, blk, nu: (jnp.minimum(i, nu[0] - 1), 0)),
            pl.BlockSpec((None, None, D, D_EXPERT), lambda i, blk, nu: (l, blk[i], 0, 0)),
            pl.BlockSpec((None, None, D, D_EXPERT), lambda i, blk, nu: (l, blk[i], 0, 0)),
            pl.BlockSpec((None, None, D_EXPERT, D), lambda i, blk, nu: (l, blk[i], 0, 0)),
        ],
        out_specs=pl.BlockSpec((TS * NS, 128), lambda i, blk, nu: (i, 0)),
        scratch_shapes=[
            pltpu.VMEM((D, D_EXPERT), BF16),
            pltpu.VMEM((D, D_EXPERT), BF16),
            pltpu.VMEM((D_EXPERT, D), BF16),
            pltpu.SMEM((1,), I32),
        ],
    )
    return pl.pallas_call(
        _moe_kernel,
        out_shape=jax.ShapeDtypeStruct((CAP * NS, 128), F32),
        grid_spec=grid_spec,
        compiler_params=_cparams(1, 48),
        name="moe_experts",
    )(blk_e, n_used, xs, w1, w3, w2)


def _lane_to_col(row):
    r = lax.broadcasted_iota(I32, (TM, TM), 0)
    c = lax.broadcasted_iota(I32, (TM, TM), 1)
    return jnp.sum(jnp.where(r == c, row, 0.0), axis=1, keepdims=True)


def _combine_kernel(final, x1_ref, pos_ref, os_ref, rw_ref, mod_ref, nfin_ref, *rest):
    n_out = 2 if final else 1
    y_refs = rest[:n_out]
    gbuf = rest[n_out]
    bufs = rest[n_out + 1:-2]
    psem, gsem = rest[-2:]
    i = pl.program_id(0)

    def pos_copies(tile, b):
        return [pltpu.make_async_copy(pos_ref.at[tile, row], bufs[b * 2 * LC + row], psem.at[b])
                for row in range(2 * LC)]

    def start_gathers(b):
        for k in range(2):
            for h in range(LC):
                def tok(t, c, k=k, h=h):
                    src = _slab(os_ref, bufs[b * 2 * LC + k * LC + h][t])
                    pltpu.make_async_copy(src, _slab(gbuf.at[2 * b + k], h * 128 + t), gsem.at[b]).start(priority=k)
                    return c
                lax.fori_loop(0, 128, tok, 0, unroll=8)

    def wait_gathers(b):
        for k in range(2):
            pltpu.make_async_copy(_slab(os_ref, 0, TM), gbuf.at[2 * b + k], gsem.at[b]).wait()

    @pl.when(i == 0)
    def _():
        for c in pos_copies(0, 0):
            c.start()
        for c in pos_copies(0, 0):
            c.wait()
        start_gathers(0)
        if NT > 1:
            for c in pos_copies(1, 1):
                c.start()

    for b in range(2):
        @pl.when(i % 2 == b)
        def _(b=b):
            @pl.when(i + 1 < NT)
            def _():
                for c in pos_copies(i + 1, 1 - b):
                    c.wait()
                start_gathers(1 - b)

            @pl.when(i + 2 < NT)
            def _():
                for c in pos_copies(i + 2, b):
                    c.start()
            wait_gathers(b)

    _, _, mrow = _tile_info(i)
    rw = rw_ref[...]
    w0 = _lane_to_col(rw[0:1])
    w1 = _lane_to_col(rw[1:2])
    g2 = _mod(mod_ref, mrow, 5)
    par = i % 2
    x2 = x1_ref[...] + g2 * (_load_slabs(gbuf.at[2 * par], TM) * w0 + _load_slabs(gbuf.at[2 * par + 1], TM) * w1)
    if final:
        yc_ref, yl_ref = y_refs
        y = _rms(x2, nfin_ref[...])

        @pl.when(i < NCT)
        def _():
            yc_ref[...] = y

        @pl.when(i >= NCT)
        def _():
            yl_ref[...] = y
    else:
        y_refs[0][...] = x2


def _combine(x1, pos, o_sorted, route_w, mod_l, norm_final, final):
    const2 = lambda i: (0, 0)
    if final:
        out_shape = [jax.ShapeDtypeStruct((T_CTX, D), F32), jax.ShapeDtypeStruct((T_LAT, D), F32)]
        out_specs = [pl.BlockSpec((TM, D), lambda i: (jnp.minimum(i, NCT - 1), 0)),
                     pl.BlockSpec((TM, D), lambda i: (jnp.maximum(i - NCT, 0), 0))]
    else:
        out_shape = [jax.ShapeDtypeStruct((T, D), F32)]
        out_specs = [pl.BlockSpec((TM, D), lambda i: (i, 0))]
    return pl.pallas_call(
        functools.partial(_combine_kernel, final),
        out_shape=out_shape,
        grid=(NT,),
        in_specs=[
            pl.BlockSpec((TM, D), lambda i: (i, 0)),
            pl.BlockSpec(memory_space=pl.ANY),
            pl.BlockSpec(memory_space=pl.ANY),
            pl.BlockSpec((None, 8, TM), lambda i: (i, 0, 0)),
            pl.BlockSpec((8, 6 * D), const2),
            pl.BlockSpec((1, D), const2),
        ],
        out_specs=out_specs,
        scratch_shapes=([pltpu.VMEM((4, TM * NS, 128), F32)] + [pltpu.SMEM((128,), I32)] * (4 * LC)
                        + [pltpu.SemaphoreType.DMA((2,)), pltpu.SemaphoreType.DMA((2,))]),
        compiler_params=_cparams(1, 32),
        name="moe_combine",
    )(x1, pos, o_sorted, route_w, mod_l, norm_final)


def _moe_layer(l, x1, h2u, route_ids, route_w, mod_l, w1, w3, w2, norm_final, final):
    route_i, counts = _rank(route_ids)
    sizes = counts[:, 0].astype(I32)
    padded = (sizes + TS - 1) // TS * TS
    pad_ends = jnp.cumsum(padded)
    pad_starts = pad_ends - padded
    n_used = (pad_ends[-1:] // TS).astype(I32)
    tile_start = jnp.arange(NST, dtype=I32) * TS
    blk_e = jnp.minimum(jnp.sum((pad_ends[None, :] <= tile_start[:, None]).astype(I32), axis=1),
                        N_EXPERTS - 1).astype(I32)
    meta = jnp.stack([pad_starts, pad_starts + sizes, pad_ends]).astype(I32)
    xs, pos = _dispatch(meta, pad_starts.astype(I32), route_i, h2u)
    o_sorted = _moe_experts(l, blk_e, n_used, xs, w1, w3, w2)
    return _combine(x1, pos, o_sorted, route_w, mod_l, norm_final, final)


def _routing_params(w_rg, b_rg, w_re, b_re):
    w = jnp.zeros((D, NRL), F32).at[:, :N_GROUPS].set(w_rg).at[:, ER0:ER0 + N_EXPERTS].set(w_re)
    hi = w.astype(BF16)
    lo = (w - hi.astype(F32)).astype(BF16)
    bias = jnp.zeros((1, NRL), F32).at[0, :N_GROUPS].set(b_rg).at[0, ER0:ER0 + N_EXPERTS].set(b_re)
    return jnp.concatenate([hi, lo], axis=1), hi, bias


def kernel(x_prompt, x_sample, cache_ckv, cache_kpe, c, c_ctx, norm_mix, norm_ffn, norm_final, w_ada, b_ada, w_pool, pool_scale, w_conv_in, conv_w, w_conv_out, w_dq, q_norm, w_uq, w_dkv, kv_norm, w_ukv, w_o, w_route_g, b_route_g, w_route_e, b_route_e, w1, w3, w2):
    x = (x_prompt.reshape(T_CTX, D), x_sample.reshape(T_LAT, D))
    cs =jnp.concatenate([c_ctx[None, :], c, jnp.zeros((8 - 1 - N_LAT_SEQ, D), F32)], axis=0)
    mod_all = _modulation(cs, w_ada, b_ada)
    nfin = norm_final[None, :]
    new_ckv = new_kpe = None
    for l in range(DEPTH):
        kind, j = l % 3, l // 3
        mod_l = mod_all[l]
        nm = norm_mix[l][None, :]
        ffn_args = (norm_ffn[l][None, :],) + _routing_params(w_route_g[l], b_route_g[l], w_route_e[l], b_route_e[l])
        if kind == 0:
            outs = _pool_layer(x, mod_l, nm, w_pool[j], pool_scale[j][None, :], ffn_args)
        elif kind == 1:
            outs = _conv_layer(x, mod_l, nm, w_conv_in[j], conv_w[j], w_conv_out[j], ffn_args)
        else:
            outs, new_ckv, new_kpe = _mla_layer(
                x, mod_l, nm, cache_ckv[:, j], cache_kpe[:, j], w_dq[j], q_norm[j][None, :], w_uq[j],
                w_dkv[j], kv_norm[j][None, :], w_ukv[j], w_o[j], ffn_args)
        x1, h2u, route_ids, route_w = outs
        ys = _moe_layer(l, x1, h2u, route_ids, route_w, mod_l, w1, w3, w2, nfin, l == DEPTH - 1)
        x = ys[0]
    y_prompt = ys[0].reshape(N_CTX_SEQ, CTX_LEN, D)
    y_sample = ys[1].reshape(N_LAT_SEQ, LAT_LEN, D)
    return (y_prompt, y_sample, new_ckv, new_kpe)
```

```python
import functools

import numpy as np
import jax
import jax.numpy as jnp
from jax import lax
from jax.experimental import pallas as pl
from jax.experimental.pallas import tpu as pltpu

F32 = jnp.float32
BF16 = jnp.bfloat16
I32 = jnp.int32

D = 1024
N_CTX_SEQ, CTX_LEN = 32, 256
N_LAT_SEQ, LAT_LEN = 4, 4096
PAST = 512
DEPTH = 4
GRID_W = 64
POOL_WINDOWS = (2, 4, 8, 16)
POOL_CH = D // 4
HEADS = 8
QK_NOPE, QK_ROPE, V_DIM = 128, 64, 128
Q_RANK, KV_RANK = 512, 256
AXIS_FREQS = QK_ROPE // 4
ROPE_BASE = 10000.0
SM_SCALE = (QK_NOPE + QK_ROPE) ** -0.5
LOG2E = 1.4426950408889634
N_GROUPS, EXP_PER_GROUP = 4, 8
N_EXPERTS = N_GROUPS * EXP_PER_GROUP
D_EXPERT = 512
EPS = 1e-6

T_CTX = N_CTX_SEQ * CTX_LEN
T_LAT = N_LAT_SEQ * LAT_LEN
T = T_CTX + T_LAT
TM = 256
NT = T // TM
NCT = T_CTX // TM
LT = LAT_LEN // TM
LT_SHIFT = LT.bit_length() - 1
HALO = 8
KB = TM + 128
NRL = 128
ER0 = 16
RG = 4
LC = TM // 128
assert 2 * LC <= 8 and NT % RG == 0
TS = 512
CAP = 2 * T + N_EXPERTS * TS
NST = CAP // TS
NS = D // 128
NK_LAT = PAST + LAT_LEN
TQ = 512
MIB = 1024 * 1024

_HI = lax.Precision.HIGHEST


def _cparams(n_axes, vmem_mib):
    return pltpu.CompilerParams(
        dimension_semantics=("arbitrary",) * n_axes,
        vmem_limit_bytes=vmem_mib * MIB)


def _rms(x, g):
    return x * lax.rsqrt(jnp.mean(x * x, axis=-1, keepdims=True) + EPS) * g


def _tile_info(i):
    is_ctx = i < NCT
    jl = jnp.maximum(i - NCT, 0)
    j = jnp.where(is_ctx, 0, jl & (LT - 1))
    mrow = jnp.where(is_ctx, 0, 1 + lax.shift_right_logical(jl, LT_SHIFT))
    return is_ctx, j, mrow


def _mod(mod_ref, mrow, k):
    return mod_ref[pl.ds(mrow, 1), k * D:(k + 1) * D]


def _store_slabs(ref, x):
    for s in range(NS):
        ref[pl.ds(s, x.shape[0], stride=NS), :] = x[:, s * 128:(s + 1) * 128]


def _load_slabs(ref, rows):
    return jnp.concatenate([ref[pl.ds(s, rows, stride=NS), :] for s in range(NS)], axis=1)


def _dot_nt(a, b):
    return lax.dot_general(a, b, (((1,), (1,)), ((), ())), preferred_element_type=F32)


def _ffn_pre(x1, mrow, mod_ref, nf_ref, wrc_ref, wrh_ref, rb_ref, h2u_ref, ri_ref, rw_ref):
    h2 = _rms(x1, nf_ref[...]) * (1.0 + _mod(mod_ref, mrow, 4)) + _mod(mod_ref, mrow, 3)
    h_hi = h2.astype(BF16)
    h_lo = (h2 - h_hi.astype(F32)).astype(BF16)
    _store_slabs(h2u_ref, h2)

    hw = jnp.dot(h_hi, wrc_ref[...], preferred_element_type=F32)
    lw = jnp.dot(h_lo, wrh_ref[...], preferred_element_type=F32)
    lt = jnp.transpose(hw[:, :NRL] + hw[:, NRL:] + lw + rb_ref[...])
    row8 = lax.broadcasted_iota(I32, (8, TM), 0)
    gl = jnp.where(row8 < N_GROUPS, lt[0:8], -jnp.inf)
    ge = jnp.exp(gl - jnp.max(gl, axis=0, keepdims=True))
    gprob = ge / jnp.sum(ge, axis=0, keepdims=True)
    g_p = jnp.max(gprob, axis=0, keepdims=True)
    g_idx = jnp.min(jnp.where(gprob == g_p, row8, 8), axis=0, keepdims=True)
    e_sel = lt[ER0:ER0 + 8]
    for g in range(1, N_GROUPS):
        e_sel = jnp.where(g_idx == g, lt[ER0 + 8 * g:ER0 + 8 * g + 8], e_sel)
    ee = jnp.exp(e_sel - jnp.max(e_sel, axis=0, keepdims=True))
    eprob = ee / jnp.sum(ee, axis=0, keepdims=True)
    p0 = jnp.max(eprob, axis=0, keepdims=True)
    i0 = jnp.min(jnp.where(eprob == p0, row8, 8), axis=0, keepdims=True)
    rest = jnp.where(row8 == i0, -1.0, eprob)
    p1 = jnp.max(rest, axis=0, keepdims=True)
    i1 = jnp.min(jnp.where(rest == p1, row8, 8), axis=0, keepdims=True)
    psum = p0 + p1
    w0 = g_p * p0 / psum
    w1 = g_p * p1 / psum
    id0 = g_idx * EXP_PER_GROUP + i0
    id1 = g_idx * EXP_PER_GROUP + i1

    ri_ref[...] = jnp.where(row8 == 0, id0, jnp.where(row8 == 1, id1, 0))
    rw_ref[...] = jnp.where(row8 == 0, w0, jnp.where(row8 == 1, w1, 0.0))


def _ffn_in_specs():
    const2 = lambda i: (0, 0)
    return [
        pl.BlockSpec((1, D), const2),
        pl.BlockSpec((D, 2 * NRL), const2),
        pl.BlockSpec((D, NRL), const2),
        pl.BlockSpec((1, NRL), const2),
    ]


def _ffn_out_shapes():
    return [
        jax.ShapeDtypeStruct((T, D), F32),
        jax.ShapeDtypeStruct((T * NS, 128), F32),
        jax.ShapeDtypeStruct((NT, 8, TM), I32),
        jax.ShapeDtypeStruct((NT, 8, TM), F32),
    ]


def _ffn_out_specs():
    return [
        pl.BlockSpec((TM, D), lambda i: (i, 0)),
        pl.BlockSpec((TM * NS, 128), lambda i: (i, 0)),
        pl.BlockSpec((None, 8, TM), lambda i: (i, 0, 0)),
        pl.BlockSpec((None, 8, TM), lambda i: (i, 0, 0)),
    ]


def _mod_kernel(cs_ref, w_ref, b_ref, o_ref):
    s = cs_ref[...]
    a = s * jax.nn.sigmoid(s)
    o_ref[...] = jnp.dot(a, w_ref[...], precision=_HI, preferred_element_type=F32) + b_ref[...]


def _modulation(cs, w_ada, b_ada):
    nb = 6
    return pl.pallas_call(
        _mod_kernel,
        out_shape=jax.ShapeDtypeStruct((DEPTH, 8, 6 * D), F32),
        grid=(DEPTH, nb),
        in_specs=[
            pl.BlockSpec((8, D), lambda l, n: (0, 0)),
            pl.BlockSpec((None, D, D), lambda l, n: (l, 0, n)),
            pl.BlockSpec((None, 1, D), lambda l, n: (l, 0, n)),
        ],
        out_specs=pl.BlockSpec((None, 8, D), lambda l, n: (l, 0, n)),
        compiler_params=_cparams(2, 32),
        name="modulation",
    )(cs, w_ada, b_ada.reshape(DEPTH, 1, 6 * D))


def _halo_specs(width):
    nb = T // HALO
    per = TM // HALO
    return [
        pl.BlockSpec((TM, width), lambda i: (i, 0)),
        pl.BlockSpec((HALO, width), lambda i: (jnp.maximum(i * per - 1, 0), 0)),
        pl.BlockSpec((HALO, width), lambda i: (jnp.minimum((i + 1) * per, nb - 1), 0)),
    ]


def _normed_halo(xt, xp, xn, i, mod_ref, nm_ref):
    is_ctx, j, mrow = _tile_info(i)
    g = nm_ref[...]
    sc = 1.0 + _mod(mod_ref, mrow, 1)
    sh = _mod(mod_ref, mrow, 0)
    pv = jnp.where(jnp.logical_and(jnp.logical_not(is_ctx), j > 0), 1.0, 0.0)
    nv = jnp.where(jnp.logical_and(jnp.logical_not(is_ctx), j < LT - 1), 1.0, 0.0)
    ht = _rms(xt, g) * sc + sh
    hp = (_rms(xp, g) * sc + sh) * pv
    hn = (_rms(xn, g) * sc + sh) * nv
    return ht, hp, hn, is_ctx, j, mrow


def _pool_kernel(split, *refs):
    if split:
        xc_ref, x_ref, xp_ref, xn_ref = refs[:4]
        refs = refs[4:]
    else:
        x_ref, xp_ref, xn_ref = refs[:3]
        refs = refs[3:]
    (mod_ref, nm_ref, band_ref, wp_ref, ps_ref, nf_ref, wrc_ref, wrh_ref, rb_ref,
     x1_ref, h2u_ref, ri_ref, rw_ref) = refs
    i = pl.program_id(0)
    x = x_ref[...]
    if split:
        x = jnp.where(i < NCT, xc_ref[...], x)
    ht, hp, hn, is_ctx, j, mrow = _normed_halo(x, xp_ref[...], xn_ref[...], i, mod_ref, nm_ref)
    hext = jnp.concatenate([ht, hp, hn, jnp.zeros((KB - TM - 2 * HALO, D), F32)], axis=0)
    e_hi = hext.astype(BF16)
    e_lo = (hext - e_hi.astype(F32)).astype(BF16)
    seq_len = jnp.where(is_ctx, CTX_LEN, LAT_LEN)
    t = j * TM + lax.broadcasted_iota(I32, (TM, POOL_CH), 0)
    outs = []
    for g, win in enumerate(POOL_WINDOWS):
        lo = win // 2
        hi = win - 1 - lo
        cols = slice(g * POOL_CH, (g + 1) * POOL_CH)
        band = band_ref[g]
        s = (jnp.dot(band, e_hi[:, cols], preferred_element_type=F32)
             + jnp.dot(band, e_lo[:, cols], preferred_element_type=F32))
        cnt = (jnp.minimum(t + hi, seq_len - 1) - jnp.maximum(t - lo, 0) + 1).astype(F32)
        d = s / cnt - ht[:, cols]
        outs.append(jnp.dot(d.astype(BF16), wp_ref[g], preferred_element_type=F32))
    o = jnp.concatenate(outs, axis=1) * ps_ref[...]
    x1 = x + _mod(mod_ref, mrow, 2) * o
    x1_ref[...] = x1
    _ffn_pre(x1, mrow, mod_ref, nf_ref, wrc_ref, wrh_ref, rb_ref, h2u_ref, ri_ref, rw_ref)


def _pool_band():
    band = np.zeros((len(POOL_WINDOWS), TM, KB), np.float32)
    r = np.arange(TM)[:, None]
    for g, win in enumerate(POOL_WINDOWS):
        lo = win // 2
        hi = win - 1 - lo
        pos = np.concatenate([np.arange(TM), np.arange(-HALO, 0), np.arange(TM, TM + HALO)])[None, :]
        band[g, :, :TM + 2 * HALO] = (pos >= r - lo) & (pos <= r + hi)
    return jnp.asarray(band, BF16)


def _split_halo_specs():
    nb = T_LAT // HALO
    per = TM // HALO
    lat = lambda i: jnp.maximum(i - NCT, 0)
    return [
        pl.BlockSpec((TM, D), lambda i: (jnp.minimum(i, NCT - 1), 0)),
        pl.BlockSpec((TM, D), lambda i: (lat(i), 0)),
        pl.BlockSpec((HALO, D), lambda i: (jnp.maximum(lat(i) * per - 1, 0), 0)),
        pl.BlockSpec((HALO, D), lambda i: (jnp.minimum((lat(i) + 1) * per, nb - 1), 0)),
    ]


def _pool_layer(xs, mod_l, nm, wp, ps, ffn_args):
    const2 = lambda i: (0, 0)
    const3 = lambda i: (0, 0, 0)
    split = isinstance(xs, tuple)
    x_args = (xs[0], xs[1], xs[1], xs[1]) if split else (xs, xs, xs)
    return pl.pallas_call(
        functools.partial(_pool_kernel, split),
        out_shape=_ffn_out_shapes(),
        grid=(NT,),
        in_specs=(_split_halo_specs() if split else _halo_specs(D)) + [
            pl.BlockSpec((8, 6 * D), const2),
            pl.BlockSpec((1, D), const2),
            pl.BlockSpec((len(POOL_WINDOWS), TM, KB), const3),
            pl.BlockSpec((len(POOL_WINDOWS), POOL_CH, POOL_CH), const3),
            pl.BlockSpec((1, D), const2),
        ] + _ffn_in_specs(),
        out_specs=_ffn_out_specs(),
        compiler_params=_cparams(1, 48),
        name="pool_mixer",
    )(*x_args, mod_l, nm, _pool_band(), wp.astype(BF16), ps, *ffn_args)


def _conv_kernel(x_ref, xp_ref, xn_ref, mod_ref, nm_ref, win_ref, cw_ref, wout_ref,
                 nf_ref, wrc_ref, wrh_ref, rb_ref,
                 x1_ref, h2u_ref, ri_ref, rw_ref):
    i = pl.program_id(0)
    x = x_ref[...]
    ht, hp, hn, is_ctx, j, mrow = _normed_halo(x, xp_ref[...], xn_ref[...], i, mod_ref, nm_ref)
    hext = jnp.concatenate([ht, hp, hn], axis=0).astype(BF16)
    bcv = jnp.dot(hext, win_ref[...], preferred_element_type=F32)
    b = bcv[:TM, :D]
    u = bcv[:, D:2 * D] * bcv[:, 2 * D:]
    um = u[:TM]
    u_before = u[TM + HALO - 1:TM + HALO]
    u_after = u[TM + HALO:TM + HALO + 1]
    row = lax.broadcasted_iota(I32, (TM, D), 0)
    up = jnp.where(row == 0, u_before, pltpu.roll(um, 1, 0))
    un = jnp.where(row == TM - 1, u_after, pltpu.roll(um, TM - 1, 0))
    cw = cw_ref[...]
    conv = up * cw[0:1] + um * cw[1:2] + un * cw[2:3]
    o = jnp.dot((b * conv).astype(BF16), wout_ref[...], preferred_element_type=F32)
    x1 = x + _mod(mod_ref, mrow, 2) * o
    x1_ref[...] = x1
    _ffn_pre(x1, mrow, mod_ref, nf_ref, wrc_ref, wrh_ref, rb_ref, h2u_ref, ri_ref, rw_ref)


def _conv_layer(x, mod_l, nm, w_in, cw, w_out, ffn_args):
    const2 = lambda i: (0, 0)
    cw8 = jnp.concatenate([cw, jnp.zeros((8 - cw.shape[0], D), F32)], axis=0)
    return pl.pallas_call(
        _conv_kernel,
        out_shape=_ffn_out_shapes(),
        grid=(NT,),
        in_specs=_halo_specs(D) + [
            pl.BlockSpec((8, 6 * D), const2),
            pl.BlockSpec((1, D), const2),
            pl.BlockSpec((D, 3 * D), const2),
            pl.BlockSpec((8, D), const2),
            pl.BlockSpec((D, D), const2),
        ] + _ffn_in_specs(),
        out_specs=_ffn_out_specs(),
        compiler_params=_cparams(1, 56),
        name="conv_mixer",
    )(x, x, x, mod_l, nm, w_in.astype(BF16), cw8, w_out.astype(BF16), *ffn_args)


W_CAT = Q_RANK + KV_RANK + 2 * QK_ROPE
QH = 2 * QK_NOPE
VH = 2 * V_DIM


def _mla_proj_kernel(x_ref, mod_ref, nm_ref, wcat_ref, qn_ref, kvn_ref, wq_ref, wqs_ref,
                     c_ref, s_ref, q_ref, ckv_ref, kpe_ref):
    i = pl.program_id(0)
    _, _, mrow = _tile_info(i)
    h = _rms(x_ref[...], nm_ref[...]) * (1.0 + _mod(mod_ref, mrow, 1)) + _mod(mod_ref, mrow, 0)
    y = jnp.dot(h.astype(BF16), wcat_ref[...], preferred_element_type=F32)
    cqn = _rms(y[:, :Q_RANK], qn_ref[...]).astype(BF16)
    q = jnp.dot(cqn, wq_ref[...], preferred_element_type=F32)
    qs = jnp.dot(cqn, wqs_ref[...], preferred_element_type=F32)
    cos = c_ref[...]
    sin = s_ref[...]
    parts = []
    for hh in range(HEADS):
        parts.append(q[:, hh * QH:hh * QH + QK_NOPE])
        parts.append(q[:, hh * QH + QK_NOPE:(hh + 1) * QH] * cos + qs[:, hh * 128:(hh + 1) * 128] * sin)
    q_ref[...] = (jnp.concatenate(parts, axis=1) * (SM_SCALE * LOG2E)).astype(BF16)
    ckv_ref[...] = _rms(y[:, Q_RANK:Q_RANK + KV_RANK], kvn_ref[...])
    k2 = y[:, Q_RANK + KV_RANK:]
    kr = k2 * cos + pltpu.roll(k2, QK_ROPE, 1) * sin
    lane = lax.broadcasted_iota(I32, (TM, 2 * QK_ROPE), 1)
    kpe_ref[...] = jnp.where(lane < QK_ROPE, kr, 0.0)


def _rope_tables():
    rows_n = LAT_LEN // GRID_W
    rows = jnp.repeat(jnp.arange(rows_n), GRID_W).astype(F32)
    cols = jnp.tile(jnp.arange(GRID_W), rows_n).astype(F32)
    inv = ROPE_BASE ** (-(jnp.arange(AXIS_FREQS, dtype=F32) / AXIS_FREQS))
    ang = jnp.stack([rows[:, None] * inv, cols[:, None] * inv], axis=1)
    cos, sin = jnp.cos(ang), jnp.sin(ang)
    c64 = jnp.concatenate([cos[:, 0], cos[:, 0], cos[:, 1], cos[:, 1]], axis=1)
    s64 = jnp.concatenate([-sin[:, 0], sin[:, 0], -sin[:, 1], sin[:, 1]], axis=1)
    c = jnp.concatenate([c64, jnp.ones((LAT_LEN, QK_ROPE), F32)], axis=1)
    s = jnp.concatenate([s64, jnp.zeros((LAT_LEN, QK_ROPE), F32)], axis=1)
    c = jnp.concatenate([c, jnp.ones((TM, 2 * QK_ROPE), F32)], axis=0)
    s = jnp.concatenate([s, jnp.zeros((TM, 2 * QK_ROPE), F32)], axis=0)
    return c, s


def _swap_rope_cols(w):
    f = AXIS_FREQS
    return jnp.concatenate([w[..., f:2 * f], w[..., :f], w[..., 3 * f:], w[..., 2 * f:3 * f]], axis=-1)


def _mla_proj(x, mod_l, nm, w_dq, q_norm, w_uq, w_dkv, kv_norm):
    const2 = lambda i: (0, 0)
    w_kpe = w_dkv[:, KV_RANK:]
    wcat = jnp.concatenate([w_dq, w_dkv[:, :KV_RANK], w_kpe, _swap_rope_cols(w_kpe)], axis=1).astype(BF16)
    wq = w_uq.reshape(Q_RANK, HEADS, QK_NOPE + QK_ROPE)
    zpad = jnp.zeros((Q_RANK, HEADS, QK_ROPE), F32)
    wq_a = jnp.concatenate([wq, zpad], axis=2).reshape(Q_RANK, HEADS * QH).astype(BF16)
    wq_s = jnp.concatenate([_swap_rope_cols(wq[:, :, QK_NOPE:]), zpad], axis=2)
    wq_s = wq_s.reshape(Q_RANK, HEADS * 128).astype(BF16)
    cos, sin = _rope_tables()
    tab_idx = lambda i: (jnp.where(i < NCT, LT, jnp.maximum(i - NCT, 0) & (LT - 1)), 0)
    return pl.pallas_call(
        _mla_proj_kernel,
        out_shape=[jax.ShapeDtypeStruct((T, HEADS * QH), BF16),
                   jax.ShapeDtypeStruct((T, KV_RANK), F32),
                   jax.ShapeDtypeStruct((T, 2 * QK_ROPE), F32)],
        grid=(NT,),
        in_specs=[
            pl.BlockSpec((TM, D), lambda i: (i, 0)),
            pl.BlockSpec((8, 6 * D), const2),
            pl.BlockSpec((1, D), const2),
            pl.BlockSpec((D, W_CAT), const2),
            pl.BlockSpec((1, Q_RANK), const2),
            pl.BlockSpec((1, KV_RANK), const2),
            pl.BlockSpec((Q_RANK, HEADS * QH), const2),
            pl.BlockSpec((Q_RANK, HEADS * 128), const2),
            pl.BlockSpec((TM, 2 * QK_ROPE), tab_idx),
            pl.BlockSpec((TM, 2 * QK_ROPE), tab_idx),
        ],
        out_specs=[pl.BlockSpec((TM, HEADS * QH), lambda i: (i, 0)),
                   pl.BlockSpec((TM, KV_RANK), lambda i: (i, 0)),
                   pl.BlockSpec((TM, 2 * QK_ROPE), lambda i: (i, 0))],
        compiler_params=_cparams(1, 48),
        name="mla_proj",
    )(x, mod_l, nm, wcat, q_norm, kv_norm, wq_a, wq_s, cos, sin)


def _kv_expand_kernel(ckv_ref, kpe_ref, wk_ref, wv_ref, k_ref, v_ref):
    c = ckv_ref[...].astype(BF16)
    kn = jnp.dot(c, wk_ref[...], preferred_element_type=F32).astype(BF16)
    kp = kpe_ref[...].astype(BF16)
    parts = []
    for hh in range(HEADS):
        parts.append(kn[:, hh * QK_NOPE:(hh + 1) * QK_NOPE])
        parts.append(kp)
    k_ref[...] = jnp.concatenate(parts, axis=1)
    v = jnp.dot(c, wv_ref[...], preferred_element_type=F32).astype(BF16)
    lane = lax.broadcasted_iota(I32, (TM, VH - V_DIM), 1)
    one = jnp.where(lane == 0, 1.0, 0.0).astype(BF16)
    parts = []
    for hh in range(HEADS):
        parts.append(v[:, hh * V_DIM:(hh + 1) * V_DIM])
        parts.append(one)
    v_ref[...] = jnp.concatenate(parts, axis=1)


def _kv_expand(ckv, kpe, wk, wv, name):
    n = ckv.shape[0]
    const2 = lambda i: (0, 0)
    return pl.pallas_call(
        _kv_expand_kernel,
        out_shape=[jax.ShapeDtypeStruct((n, HEADS * QH), BF16),
                   jax.ShapeDtypeStruct((n, HEADS * VH), BF16)],
        grid=(n // TM,),
        in_specs=[
            pl.BlockSpec((TM, KV_RANK), lambda i: (i, 0)),
            pl.BlockSpec((TM, 2 * QK_ROPE), lambda i: (i, 0)),
            pl.BlockSpec((KV_RANK, HEADS * QK_NOPE), const2),
            pl.BlockSpec((KV_RANK, HEADS * V_DIM), const2),
        ],
        out_specs=[pl.BlockSpec((TM, HEADS * QH), lambda i: (i, 0)),
                   pl.BlockSpec((TM, HEADS * VH), lambda i: (i, 0))],
        compiler_params=_cparams(1, 32),
        name=name,
    )(ckv, kpe, wk, wv)


def _attend(q, k, v):
    s = _dot_nt(q, k)
    p = jnp.exp2(s - jnp.max(s, axis=1, keepdims=True))
    o = jnp.dot(p.astype(BF16), v, preferred_element_type=F32)
    return o[:, :V_DIM] / o[:, V_DIM:V_DIM + 1]


KC = 512
NKC = NK_LAT // KC


def _attn_lat_kernel(q_ref, k_ref, v_ref, o_ref, s_ref, m_ref, acc_ref):
    q = q_ref[...]
    m_ref[...] = jnp.full((TQ, 128), -jnp.inf, F32)

    def scores(c, carry):
        k = k_ref[pl.ds(pl.multiple_of(c * KC, KC), KC), :]
        s = _dot_nt(q, k)
        s_ref[c] = s
        m = m_ref[...]
        for j in range(KC // 128):
            m = jnp.maximum(m, s[:, j * 128:(j + 1) * 128])
        m_ref[...] = m
        return carry
    lax.fori_loop(0, NKC, scores, 0, unroll=True)

    mb = jnp.broadcast_to(jnp.max(m_ref[...], axis=1, keepdims=True), (TQ, 128))
    acc_ref[...] = jnp.zeros((TQ, VH), F32)

    def weighted(c, carry):
        s = s_ref[c]
        p = jnp.concatenate([jnp.exp2(s[:, j * 128:(j + 1) * 128] - mb) for j in range(KC // 128)], axis=1)
        v = v_ref[pl.ds(pl.multiple_of(c * KC, KC), KC), :]
        acc_ref[...] += jnp.dot(p.astype(BF16), v, preferred_element_type=F32)
        return carry
    lax.fori_loop(0, NKC, weighted, 0, unroll=3)

    acc = acc_ref[...]
    o_ref[...] = (acc[:, :V_DIM] / acc[:, V_DIM:V_DIM + 1]).astype(BF16)


def _attn_lat(q, k, v):
    nq = LAT_LEN // TQ
    q0 = T_CTX // TQ
    return pl.pallas_call(
        _attn_lat_kernel,
        scratch_shapes=[pltpu.VMEM((NKC, TQ, KC), F32), pltpu.VMEM((TQ, 128), F32), pltpu.VMEM((TQ, VH), F32)],
        out_shape=jax.ShapeDtypeStruct((T_LAT, HEADS * V_DIM), BF16),
        grid=(N_LAT_SEQ, HEADS, nq),
        in_specs=[
            pl.BlockSpec((TQ, QH), lambda b, h, t: (q0 + b * nq + t, h)),
            pl.BlockSpec((NK_LAT, QH), lambda b, h, t: (b, h)),
            pl.BlockSpec((NK_LAT, VH), lambda b, h, t: (b, h)),
        ],
        out_specs=pl.BlockSpec((TQ, V_DIM), lambda b, h, t: (b * nq + t, h)),
        compiler_params=_cparams(3, 48),
        name="attn_latent",
    )(q, k, v)


def _attn_ctx_kernel(q_ref, k_ref, v_ref, o_ref):
    outs = []
    for hh in range(HEADS):
        outs.append(_attend(q_ref[:, hh * QH:(hh + 1) * QH], k_ref[:, hh * QH:(hh + 1) * QH],
                            v_ref[:, hh * VH:(hh + 1) * VH]))
    o_ref[...] = jnp.concatenate(outs, axis=1).astype(BF16)


def _attn_ctx(q, k, v):
    return pl.pallas_call(
        _attn_ctx_kernel,
        out_shape=jax.ShapeDtypeStruct((T_CTX, HEADS * V_DIM), BF16),
        grid=(N_CTX_SEQ,),
        in_specs=[
            pl.BlockSpec((CTX_LEN, HEADS * QH), lambda b: (b, 0)),
            pl.BlockSpec((CTX_LEN, HEADS * QH), lambda b: (b, 0)),
            pl.BlockSpec((CTX_LEN, HEADS * VH), lambda b: (b, 0)),
        ],
        out_specs=pl.BlockSpec((CTX_LEN, HEADS * V_DIM), lambda b: (b, 0)),
        compiler_params=_cparams(1, 32),
        name="attn_context",
    )(q, k, v)


def _attn_out_kernel(x_ref, oc_ref, ol_ref, mod_ref, wo_ref,
                     nf_ref, wrc_ref, wrh_ref, rb_ref,
                     x1_ref, h2u_ref, ri_ref, rw_ref):
    i = pl.program_id(0)
    is_ctx, _, mrow = _tile_info(i)
    att = jnp.where(is_ctx, oc_ref[...], ol_ref[...])
    o = jnp.dot(att, wo_ref[...], preferred_element_type=F32)
    x1 = x_ref[...] + _mod(mod_ref, mrow, 2) * o
    x1_ref[...] = x1
    _ffn_pre(x1, mrow, mod_ref, nf_ref, wrc_ref, wrh_ref, rb_ref, h2u_ref, ri_ref, rw_ref)


def _attn_out(x, o_ctx, o_lat, mod_l, w_o, ffn_args):
    const2 = lambda i: (0, 0)
    return pl.pallas_call(
        _attn_out_kernel,
        out_shape=_ffn_out_shapes(),
        grid=(NT,),
        in_specs=[
            pl.BlockSpec((TM, D), lambda i: (i, 0)),
            pl.BlockSpec((TM, D), lambda i: (jnp.minimum(i, NCT - 1), 0)),
            pl.BlockSpec((TM, D), lambda i: (jnp.maximum(i - NCT, 0), 0)),
            pl.BlockSpec((8, 6 * D), const2),
            pl.BlockSpec((D, D), const2),
        ] + _ffn_in_specs(),
        out_specs=_ffn_out_specs(),
        compiler_params=_cparams(1, 48),
        name="attn_out",
    )(x, o_ctx, o_lat, mod_l, w_o.astype(BF16), *ffn_args)


def _mla_layer(x, mod_l, nm, cache_ckv, cache_kpe, w_dq, q_norm, w_uq, w_dkv, kv_norm, w_ukv, w_o, ffn_args):
    q, ckv, kpe = _mla_proj(x, mod_l, nm, w_dq, q_norm, w_uq, w_dkv, kv_norm)
    wkv = w_ukv.reshape(KV_RANK, HEADS, QK_NOPE + V_DIM)
    wk = wkv[:, :, :QK_NOPE].reshape(KV_RANK, HEADS * QK_NOPE).astype(BF16)
    wv = wkv[:, :, QK_NOPE:].reshape(KV_RANK, HEADS * V_DIM).astype(BF16)
    k_c, v_c = _kv_expand(ckv[:T_CTX], kpe[:T_CTX], wk, wv, "kv_expand_context")
    kpe_cache = jnp.concatenate([cache_kpe, jnp.zeros_like(cache_kpe)], axis=-1)
    ckv_l = jnp.concatenate([cache_ckv, ckv[T_CTX:].reshape(N_LAT_SEQ, LAT_LEN, KV_RANK)], axis=1)
    kpe_l = jnp.concatenate([kpe_cache, kpe[T_CTX:].reshape(N_LAT_SEQ, LAT_LEN, 2 * QK_ROPE)], axis=1)
    k_l, v_l = _kv_expand(ckv_l.reshape(N_LAT_SEQ * NK_LAT, KV_RANK),
                          kpe_l.reshape(N_LAT_SEQ * NK_LAT, 2 * QK_ROPE), wk, wv, "kv_expand_latent")
    o_c = _attn_ctx(q, k_c, v_c)
    o_l = _attn_lat(q, k_l, v_l)
    outs = _attn_out(x, o_c, o_l, mod_l, w_o, ffn_args)
    new_ckv = ckv[:T_CTX].reshape(N_CTX_SEQ, 1, CTX_LEN, KV_RANK)
    new_kpe = kpe[:T_CTX, :QK_ROPE].reshape(N_CTX_SEQ, 1, CTX_LEN, QK_ROPE)
    return outs, new_ckv, new_kpe


GROUP = 8


def _slab(ref, row, n=1):
    return ref.at[pl.ds(pl.multiple_of(row * NS, NS), n * NS)]


def _dispatch_kernel(meta_ref, psv_ref, ri_ref, h2_ref, xs_ref, pos_ref, stage, zbuf, hbuf, *rest):
    bufs, (sem, dsem, zsem, hsem) = rest[:-4], rest[-4:]
    i = pl.program_id(0)

    def row_copies(b):
        return [pltpu.make_async_copy(stage.at[b, row], bufs[b * 2 * LC + row], sem.at[b])
                for row in range(2 * LC)]

    def tile_fetch(tile):
        return pltpu.make_async_copy(_slab(h2_ref, tile * TM, TM), hbuf.at[tile % 3], hsem.at[tile % 3])

    def tile_rows_done(b):
        return pltpu.make_async_copy(_slab(h2_ref, 0, 2 * TM), _slab(xs_ref, 0, 2 * TM), dsem.at[b])

    @pl.when(i == 0)
    def _():
        zbuf[...] = jnp.zeros(zbuf.shape, F32)

        def pad_fill(e):
            return pltpu.make_async_copy(zbuf, _slab(xs_ref, meta_ref[2, e] - TS, TS), zsem)

        def start(e, c):
            @pl.when(meta_ref[2, e] > meta_ref[0, e])
            def _():
                pad_fill(e).start()
            return c

        def wait(e, c):
            @pl.when(meta_ref[2, e] > meta_ref[0, e])
            def _():
                pad_fill(e).wait()
            return c
        lax.fori_loop(0, N_EXPERTS, start, 0)
        lax.fori_loop(0, N_EXPERTS, wait, 0)

        def tail_fill(tile):
            return pltpu.make_async_copy(zbuf, _slab(xs_ref, tile * TS, TS), zsem)

        def tail_start(tile, c):
            tail_fill(tile).start()
            return c

        def tail_wait(tile, c):
            tail_fill(tile).wait()
            return c
        first_unused = lax.shift_right_logical(meta_ref[2, N_EXPERTS - 1], jnp.int32(TS.bit_length() - 1))
        lax.fori_loop(first_unused, NST, tail_start, 0)
        lax.fori_loop(first_unused, NST, tail_wait, 0)

    @pl.when(i < NT)
    def _():
        tile_fetch(i).start()
        ri = ri_ref[...]
        rowe = lax.broadcasted_iota(I32, (N_EXPERTS, TM), 0)
        ps = psv_ref[...]
        pos0 = jnp.sum(jnp.where(rowe == ri[0:1], ps, 0), axis=0, keepdims=True) + ri[2:3]
        pos1 = jnp.sum(jnp.where(rowe == ri[1:2], ps, 0), axis=0, keepdims=True) + ri[3:4]
        row8 = lax.broadcasted_iota(I32, (8, 128), 0)
        st = jnp.zeros((8, 128), I32)
        for k, pos in enumerate((pos0, pos1)):
            for h in range(LC):
                st = jnp.where(row8 == k * LC + h, pos[:, h * 128:(h + 1) * 128], st)
        pos_ref[...] = st
        for b in range(2):
            @pl.when(i % 2 == b)
            def _(b=b):
                stage[b] = st
                for c in row_copies(b):
                    c.start()

    j = i - 1
    for b in range(2):
        @pl.when(jnp.logical_and(i >= 1, j % 2 == b))
        def _(b=b):
            for c in row_copies(b):
                c.wait()
            tile_fetch(j).wait()
            rows = hbuf.at[j % 3]
            for h in range(LC):
                def toks(g, c, h=h):
                    t0 = g * GROUP
                    slots = [[bufs[b * 2 * LC + k * LC + h][t0 + u] for k in range(2)] for u in range(GROUP)]
                    for u in range(GROUP):
                        src = _slab(rows, h * 128 + t0 + u)
                        for k in range(2):
                            pltpu.make_async_copy(src, _slab(xs_ref, slots[u][k]), dsem.at[b]).start(priority=k)
                    return c
                lax.fori_loop(0, 128 // GROUP, toks, 0)

            @pl.when(j >= 1)
            def _():
                tile_rows_done(1 - b).wait()

            @pl.when(i == NT)
            def _():
                tile_rows_done(b).wait()


def _rank_kernel(ri_ref, trio_ref, ro_ref, cnt_ref, carry_ref):
    @pl.when(pl.program_id(0) == 0)
    def _():
        carry_ref[...] = jnp.zeros_like(carry_ref)

    rowe = lax.broadcasted_iota(I32, (N_EXPERTS, TM), 0)
    row8 = lax.broadcasted_iota(I32, (8, TM), 0)
    ids = [ri_ref[tt] for tt in range(RG)]
    ohs = [rowe == ids[tt][k:k + 1] for tt in range(RG) for k in range(2)]
    ohb = jnp.concatenate([jnp.where(oh, 1.0, 0.0).astype(BF16) for oh in ohs], axis=0)
    pt = jnp.dot(ohb, trio_ref[...], preferred_element_type=F32)
    carry = carry_ref[...]
    for tt in range(RG):
        ranks = []
        for k in range(2):
            r0 = (tt * 2 + k) * N_EXPERTS
            before = carry + pt[r0:r0 + N_EXPERTS, :TM]
            ranks.append(jnp.sum(jnp.where(ohs[tt * 2 + k], before, 0.0), axis=0, keepdims=True).astype(I32))
            carry = carry + pt[r0:r0 + N_EXPERTS, TM:]
        ro_ref[tt] = jnp.where(row8 == 0, ids[tt][0:1],
                     jnp.where(row8 == 1, ids[tt][1:2],
                     jnp.where(row8 == 2, ranks[0],
                     jnp.where(row8 == 3, ranks[1], 0))))
    carry_ref[...] = carry
    cnt_ref[...] = carry


def _rank(route_i):
    tri = np.triu(np.ones((TM, TM), np.float32), 1)
    trio = jnp.asarray(np.concatenate([tri, np.ones((TM, TM), np.float32)], axis=1), BF16)
    return pl.pallas_call(
        _rank_kernel,
        out_shape=[jax.ShapeDtypeStruct((NT, 8, TM), I32), jax.ShapeDtypeStruct((N_EXPERTS, TM), F32)],
        grid=(NT // RG,),
        in_specs=[pl.BlockSpec((RG, 8, TM), lambda g: (g, 0, 0)),
                  pl.BlockSpec((TM, 2 * TM), lambda g: (0, 0))],
        out_specs=[pl.BlockSpec((RG, 8, TM), lambda g: (g, 0, 0)),
                   pl.BlockSpec((N_EXPERTS, TM), lambda g: (0, 0))],
        scratch_shapes=[pltpu.VMEM((N_EXPERTS, TM), F32)],
        compiler_params=_cparams(1, 16),
        name="expert_rank",
    )(route_i, trio)


def _dispatch(meta, pad_starts, route_i, h2):
    tile = lambda i: (jnp.minimum(i, NT - 1), 0, 0)
    return pl.pallas_call(
        _dispatch_kernel,
        out_shape=[jax.ShapeDtypeStruct((CAP * NS, 128), F32),
                   jax.ShapeDtypeStruct((NT, 8, 128), I32)],
        grid=(NT + 1,),
        in_specs=[pl.BlockSpec(memory_space=pltpu.SMEM),
                  pl.BlockSpec((N_EXPERTS, TM), lambda i: (0, 0)),
                  pl.BlockSpec((None, 8, TM), tile),
                  pl.BlockSpec(memory_space=pl.ANY)],
        out_specs=[pl.BlockSpec(memory_space=pl.ANY), pl.BlockSpec((None, 8, 128), tile)],
        scratch_shapes=([pltpu.VMEM((2, 8, 128), I32), pltpu.VMEM((TS * NS, 128), F32),
                         pltpu.VMEM((3, TM * NS, 128), F32)]
                        + [pltpu.SMEM((128,), I32)] * (4 * LC)
                        + [pltpu.SemaphoreType.DMA((2,)), pltpu.SemaphoreType.DMA((2,)), pltpu.SemaphoreType.DMA,
                           pltpu.SemaphoreType.DMA((3,))]),
        compiler_params=_cparams(1, 16),
        name="moe_dispatch",
    )(meta, jnp.broadcast_to(pad_starts[:, None], (N_EXPERTS, TM)), route_i, h2)


def _moe_kernel(blk_ref, nu_ref, xs_ref, w1_ref, w3_ref, w2_ref, o_ref, wb1, wb3, wb2, pe_ref):
    i = pl.program_id(0)
    n_used = nu_ref[0]

    @pl.when(i == 0)
    def _():
        pe_ref[0] = -1

    @pl.when(i >= n_used)
    def _():
        o_ref[...] = jnp.zeros(o_ref.shape, F32)

    @pl.when(i < n_used)
    def _():
        e = blk_ref[i]

        @pl.when(e != pe_ref[0])
        def _():
            wb1[...] = w1_ref[...].astype(BF16)
            wb3[...] = w3_ref[...].astype(BF16)
            wb2[...] = w2_ref[...].astype(BF16)
            pe_ref[0] = e

        xb = _load_slabs(xs_ref, TS).astype(BF16)
        a = jnp.dot(xb, wb1[...], preferred_element_type=F32)
        b = jnp.dot(xb, wb3[...], preferred_element_type=F32)
        hm = (a * jax.nn.sigmoid(a) * b).astype(BF16)
        _store_slabs(o_ref, jnp.dot(hm, wb2[...], preferred_element_type=F32))


def _moe_experts(l, blk_e, n_used, xs, w1, w3, w2):
    grid_spec = pltpu.PrefetchScalarGridSpec(
        num_scalar_prefetch=2,
        grid=(NST,),
        in_specs=[
            pl.BlockSpec((TS * NS, 128), lambda i, blk, nu: (jnp.minimum(i, nu[0] - 1), 0)),
            pl.BlockSpec((None, None, D, D_EXPERT), lambda i, blk, nu: (l, blk[i], 0, 0)),
            pl.BlockSpec((None, None, D, D_EXPERT), lambda i, blk, nu: (l, blk[i], 0, 0)),
            pl.BlockSpec((None, None, D_EXPERT, D), lambda i, blk, nu: (l, blk[i], 0, 0)),
        ],
        out_specs=pl.BlockSpec((TS * NS, 128), lambda i, blk, nu: (i, 0)),
        scratch_shapes=[
            pltpu.VMEM((D, D_EXPERT), BF16),
            pltpu.VMEM((D, D_EXPERT), BF16),
            pltpu.VMEM((D_EXPERT, D), BF16),
            pltpu.SMEM((1,), I32),
        ],
    )
    return pl.pallas_call(
        _moe_kernel,
        out_shape=jax.ShapeDtypeStruct((CAP * NS, 128), F32),
        grid_spec=grid_spec,
        compiler_params=_cparams(1, 48),
        name="moe_experts",
    )(blk_e, n_used, xs, w1, w3, w2)


def _lane_to_col(row):
    r = lax.broadcasted_iota(I32, (TM, TM), 0)
    c = lax.broadcasted_iota(I32, (TM, TM), 1)
    return jnp.sum(jnp.where(r == c, row, 0.0), axis=1, keepdims=True)


def _combine_kernel(final, x1_ref, pos_ref, os_ref, rw_ref, mod_ref, nfin_ref, *rest):
    n_out = 2 if final else 1
    y_refs = rest[:n_out]
    gbuf = rest[n_out]
    bufs = rest[n_out + 1:-2]
    psem, gsem = rest[-2:]
    i = pl.program_id(0)

    def pos_copies(tile, b):
        return [pltpu.make_async_copy(pos_ref.at[tile, row], bufs[b * 2 * LC + row], psem.at[b])
                for row in range(2 * LC)]

    def start_gathers(b):
        for k in range(2):
            for h in range(LC):
                def tok(t, c, k=k, h=h):
                    src = _slab(os_ref, bufs[b * 2 * LC + k * LC + h][t])
                    pltpu.make_async_copy(src, _slab(gbuf.at[2 * b + k], h * 128 + t), gsem.at[b]).start(priority=k)
                    return c
                lax.fori_loop(0, 128, tok, 0, unroll=8)

    def wait_gathers(b):
        for k in range(2):
            pltpu.make_async_copy(_slab(os_ref, 0, TM), gbuf.at[2 * b + k], gsem.at[b]).wait()

    @pl.when(i == 0)
    def _():
        for c in pos_copies(0, 0):
            c.start()
        for c in pos_copies(0, 0):
            c.wait()
        start_gathers(0)
        if NT > 1:
            for c in pos_copies(1, 1):
                c.start()

    for b in range(2):
        @pl.when(i % 2 == b)
        def _(b=b):
            @pl.when(i + 1 < NT)
            def _():
                for c in pos_copies(i + 1, 1 - b):
                    c.wait()
                start_gathers(1 - b)

            @pl.when(i + 2 < NT)
            def _():
                for c in pos_copies(i + 2, b):
                    c.start()
            wait_gathers(b)

    _, _, mrow = _tile_info(i)
    rw = rw_ref[...]
    w0 = _lane_to_col(rw[0:1])
    w1 = _lane_to_col(rw[1:2])
    g2 = _mod(mod_ref, mrow, 5)
    par = i % 2
    x2 = x1_ref[...] + g2 * (_load_slabs(gbuf.at[2 * par], TM) * w0 + _load_slabs(gbuf.at[2 * par + 1], TM) * w1)
    if final:
        yc_ref, yl_ref = y_refs
        y = _rms(x2, nfin_ref[...])

        @pl.when(i < NCT)
        def _():
            yc_ref[...] = y

        @pl.when(i >= NCT)
        def _():
            yl_ref[...] = y
    else:
        y_refs[0][...] = x2


def _combine(x1, pos, o_sorted, route_w, mod_l, norm_final, final):
    const2 = lambda i: (0, 0)
    if final:
        out_shape = [jax.ShapeDtypeStruct((T_CTX, D), F32), jax.ShapeDtypeStruct((T_LAT, D), F32)]
        out_specs = [pl.BlockSpec((TM, D), lambda i: (jnp.minimum(i, NCT - 1), 0)),
                     pl.BlockSpec((TM, D), lambda i: (jnp.maximum(i - NCT, 0), 0))]
    else:
        out_shape = [jax.ShapeDtypeStruct((T, D), F32)]
        out_specs = [pl.BlockSpec((TM, D), lambda i: (i, 0))]
    return pl.pallas_call(
        functools.partial(_combine_kernel, final),
        out_shape=out_shape,
        grid=(NT,),
        in_specs=[
            pl.BlockSpec((TM, D), lambda i: (i, 0)),
            pl.BlockSpec(memory_space=pl.ANY),
            pl.BlockSpec(memory_space=pl.ANY),
            pl.BlockSpec((None, 8, TM), lambda i: (i, 0, 0)),
            pl.BlockSpec((8, 6 * D), const2),
            pl.BlockSpec((1, D), const2),
        ],
        out_specs=out_specs,
        scratch_shapes=([pltpu.VMEM((4, TM * NS, 128), F32)] + [pltpu.SMEM((128,), I32)] * (4 * LC)
                        + [pltpu.SemaphoreType.DMA((2,)), pltpu.SemaphoreType.DMA((2,))]),
        compiler_params=_cparams(1, 32),
        name="moe_combine",
    )(x1, pos, o_sorted, route_w, mod_l, norm_final)


def _moe_layer(l, x1, h2u, route_ids, route_w, mod_l, w1, w3, w2, norm_final, final):
    route_i, counts = _rank(route_ids)
    sizes = counts[:, 0].astype(I32)
    padded = (sizes + TS - 1) // TS * TS
    pad_ends = jnp.cumsum(padded)
    pad_starts = pad_ends - padded
    n_used = (pad_ends[-1:] // TS).astype(I32)
    tile_start = jnp.arange(NST, dtype=I32) * TS
    blk_e = jnp.minimum(jnp.sum((pad_ends[None, :] <= tile_start[:, None]).astype(I32), axis=1),
                        N_EXPERTS - 1).astype(I32)
    meta = jnp.stack([pad_starts, pad_starts + sizes, pad_ends]).astype(I32)
    xs, pos = _dispatch(meta, pad_starts.astype(I32), route_i, h2u)
    o_sorted = _moe_experts(l, blk_e, n_used, xs, w1, w3, w2)
    return _combine(x1, pos, o_sorted, route_w, mod_l, norm_final, final)


def _routing_params(w_rg, b_rg, w_re, b_re):
    w = jnp.zeros((D, NRL), F32).at[:, :N_GROUPS].set(w_rg).at[:, ER0:ER0 + N_EXPERTS].set(w_re)
    hi = w.astype(BF16)
    lo = (w - hi.astype(F32)).astype(BF16)
    bias = jnp.zeros((1, NRL), F32).at[0, :N_GROUPS].set(b_rg).at[0, ER0:ER0 + N_EXPERTS].set(b_re)
    return jnp.concatenate([hi, lo], axis=1), hi, bias


def kernel(x_prompt, x_sample, cache_ckv, cache_kpe, c, c_ctx, norm_mix, norm_ffn, norm_final, w_ada, b_ada, w_pool, pool_scale, w_conv_in, conv_w, w_conv_out, w_dq, q_norm, w_uq, w_dkv, kv_norm, w_ukv, w_o, w_route_g, b_route_g, w_route_e, b_route_e, w1, w3, w2):
    x = (x_prompt.reshape(T_CTX, D), x_sample.reshape(T_LAT, D))
    cs =jnp.concatenate([c_ctx[None, :], c, jnp.zeros((8 - 1 - N_LAT_SEQ, D), F32)], axis=0)
    mod_all = _modulation(cs, w_ada, b_ada)
    nfin = norm_final[None, :]
    new_ckv = new_kpe = None
    for l in range(DEPTH):
        kind, j = l % 3, l // 3
        mod_l = mod_all[l]
        nm = norm_mix[l][None, :]
        ffn_args = (norm_ffn[l][None, :],) + _routing_params(w_route_g[l], b_route_g[l], w_route_e[l], b_route_e[l])
        if kind == 0:
            outs = _pool_layer(x, mod_l, nm, w_pool[j], pool_scale[j][None, :], ffn_args)
        elif kind == 1:
            outs = _conv_layer(x, mod_l, nm, w_conv_in[j], conv_w[j], w_conv_out[j], ffn_args)
        else:
            outs, new_ckv, new_kpe = _mla_layer(
                x, mod_l, nm, cache_ckv[:, j], cache_kpe[:, j], w_dq[j], q_norm[j][None, :], w_uq[j],
                w_dkv[j], kv_norm[j][None, :], w_ukv[j], w_o[j], ffn_args)
        x1, h2u, route_ids, route_w = outs
        ys = _moe_layer(l, x1, h2u, route_ids, route_w, mod_l, w1, w3, w2, nfin, l == DEPTH - 1)
        x = ys[0]
    y_prompt = ys[0].reshape(N_CTX_SEQ, CTX_LEN, D)
    y_sample = ys[1].reshape(N_LAT_SEQ, LAT_LEN, D)
    return (y_prompt, y_sample, new_ckv, new_kpe)
```

```python
import functools

import numpy as np
import jax
import jax.numpy as jnp
from jax import lax
from jax.experimental import pallas as pl
from jax.experimental.pallas import tpu as pltpu

F32 = jnp.float32
BF16 = jnp.bfloat16
I32 = jnp.int32

D = 1024
N_CTX_SEQ, CTX_LEN = 32, 256
N_LAT_SEQ, LAT_LEN = 4, 4096
PAST = 512
DEPTH = 4
GRID_W = 64
POOL_WINDOWS = (2, 4, 8, 16)
POOL_CH = D // 4
HEADS = 8
QK_NOPE, QK_ROPE, V_DIM = 128, 64, 128
Q_RANK, KV_RANK = 512, 256
AXIS_FREQS = QK_ROPE // 4
ROPE_BASE = 10000.0
SM_SCALE = (QK_NOPE + QK_ROPE) ** -0.5
LOG2E = 1.4426950408889634
N_GROUPS, EXP_PER_GROUP = 4, 8
N_EXPERTS = N_GROUPS * EXP_PER_GROUP
D_EXPERT = 512
EPS = 1e-6

T_CTX = N_CTX_SEQ * CTX_LEN
T_LAT = N_LAT_SEQ * LAT_LEN
T = T_CTX + T_LAT
TM = 256
NT = T // TM
NCT = T_CTX // TM
LT = LAT_LEN // TM
LT_SHIFT = LT.bit_length() - 1
HALO = 8
KB = TM + 128
NRL = 128
ER0 = 16
RG = 4
LC = TM // 128
assert 2 * LC <= 8 and NT % RG == 0
TS = 512
CAP = 2 * T + N_EXPERTS * TS
NST = CAP // TS
NS = D // 128
NK_LAT = PAST + LAT_LEN
TQ = 1024
MIB = 1024 * 1024

_HI = lax.Precision.HIGHEST


def _cparams(n_axes, vmem_mib):
    return pltpu.CompilerParams(
        dimension_semantics=("arbitrary",) * n_axes,
        vmem_limit_bytes=vmem_mib * MIB)


def _rms(x, g):
    return x * lax.rsqrt(jnp.mean(x * x, axis=-1, keepdims=True) + EPS) * g


def _tile_info(i):
    is_ctx = i < NCT
    jl = jnp.maximum(i - NCT, 0)
    j = jnp.where(is_ctx, 0, jl & (LT - 1))
    mrow = jnp.where(is_ctx, 0, 1 + lax.shift_right_logical(jl, LT_SHIFT))
    return is_ctx, j, mrow


def _mod(mod_ref, mrow, k):
    return mod_ref[pl.ds(mrow, 1), k * D:(k + 1) * D]


def _store_slabs(ref, x):
    for s in range(NS):
        ref[pl.ds(s, x.shape[0], stride=NS), :] = x[:, s * 128:(s + 1) * 128]


def _load_slabs(ref, rows):
    return jnp.concatenate([ref[pl.ds(s, rows, stride=NS), :] for s in range(NS)], axis=1)


def _dot_nt(a, b):
    return lax.dot_general(a, b, (((1,), (1,)), ((), ())), preferred_element_type=F32)


def _ffn_pre(x1, mrow, mod_ref, nf_ref, wrc_ref, wrh_ref, rb_ref, h2u_ref, ri_ref, rw_ref):
    h2 = _rms(x1, nf_ref[...]) * (1.0 + _mod(mod_ref, mrow, 4)) + _mod(mod_ref, mrow, 3)
    h_hi = h2.astype(BF16)
    h_lo = (h2 - h_hi.astype(F32)).astype(BF16)
    _store_slabs(h2u_ref, h2)

    hw = jnp.dot(h_hi, wrc_ref[...], preferred_element_type=F32)
    lw = jnp.dot(h_lo, wrh_ref[...], preferred_element_type=F32)
    lt = jnp.transpose(hw[:, :NRL] + hw[:, NRL:] + lw + rb_ref[...])
    row8 = lax.broadcasted_iota(I32, (8, TM), 0)
    gl = jnp.where(row8 < N_GROUPS, lt[0:8], -jnp.inf)
    ge = jnp.exp(gl - jnp.max(gl, axis=0, keepdims=True))
    gprob = ge / jnp.sum(ge, axis=0, keepdims=True)
    g_p = jnp.max(gprob, axis=0, keepdims=True)
    g_idx = jnp.min(jnp.where(gprob == g_p, row8, 8), axis=0, keepdims=True)
    e_sel = lt[ER0:ER0 + 8]
    for g in range(1, N_GROUPS):
        e_sel = jnp.where(g_idx == g, lt[ER0 + 8 * g:ER0 + 8 * g + 8], e_sel)
    ee = jnp.exp(e_sel - jnp.max(e_sel, axis=0, keepdims=True))
    eprob = ee / jnp.sum(ee, axis=0, keepdims=True)
    p0 = jnp.max(eprob, axis=0, keepdims=True)
    i0 = jnp.min(jnp.where(eprob == p0, row8, 8), axis=0, keepdims=True)
    rest = jnp.where(row8 == i0, -1.0, eprob)
    p1 = jnp.max(rest, axis=0, keepdims=True)
    i1 = jnp.min(jnp.where(rest == p1, row8, 8), axis=0, keepdims=True)
    psum = p0 + p1
    w0 = g_p * p0 / psum
    w1 = g_p * p1 / psum
    id0 = g_idx * EXP_PER_GROUP + i0
    id1 = g_idx * EXP_PER_GROUP + i1

    ri_ref[...] = jnp.where(row8 == 0, id0, jnp.where(row8 == 1, id1, 0))
    rw_ref[...] = jnp.where(row8 == 0, w0, jnp.where(row8 == 1, w1, 0.0))


def _ffn_in_specs():
    const2 = lambda i: (0, 0)
    return [
        pl.BlockSpec((1, D), const2),
        pl.BlockSpec((D, 2 * NRL), const2),
        pl.BlockSpec((D, NRL), const2),
        pl.BlockSpec((1, NRL), const2),
    ]


def _ffn_out_shapes():
    return [
        jax.ShapeDtypeStruct((T, D), F32),
        jax.ShapeDtypeStruct((T * NS, 128), F32),
        jax.ShapeDtypeStruct((NT, 8, TM), I32),
        jax.ShapeDtypeStruct((NT, 8, TM), F32),
    ]


def _ffn_out_specs():
    return [
        pl.BlockSpec((TM, D), lambda i: (i, 0)),
        pl.BlockSpec((TM * NS, 128), lambda i: (i, 0)),
        pl.BlockSpec((None, 8, TM), lambda i: (i, 0, 0)),
        pl.BlockSpec((None, 8, TM), lambda i: (i, 0, 0)),
    ]


def _mod_kernel(cs_ref, w_ref, b_ref, o_ref):
    s = cs_ref[...]
    a = s * jax.nn.sigmoid(s)
    o_ref[...] = jnp.dot(a, w_ref[...], precision=_HI, preferred_element_type=F32) + b_ref[...]


def _modulation(cs, w_ada, b_ada):
    nb = 6
    return pl.pallas_call(
        _mod_kernel,
        out_shape=jax.ShapeDtypeStruct((DEPTH, 8, 6 * D), F32),
        grid=(DEPTH, nb),
        in_specs=[
            pl.BlockSpec((8, D), lambda l, n: (0, 0)),
            pl.BlockSpec((None, D, D), lambda l, n: (l, 0, n)),
            pl.BlockSpec((None, 1, D), lambda l, n: (l, 0, n)),
        ],
        out_specs=pl.BlockSpec((None, 8, D), lambda l, n: (l, 0, n)),
        compiler_params=_cparams(2, 32),
        name="modulation",
    )(cs, w_ada, b_ada.reshape(DEPTH, 1, 6 * D))


def _halo_specs(width):
    nb = T // HALO
    per = TM // HALO
    return [
        pl.BlockSpec((TM, width), lambda i: (i, 0)),
        pl.BlockSpec((HALO, width), lambda i: (jnp.maximum(i * per - 1, 0), 0)),
        pl.BlockSpec((HALO, width), lambda i: (jnp.minimum((i + 1) * per, nb - 1), 0)),
    ]


def _normed_halo(xt, xp, xn, i, mod_ref, nm_ref):
    is_ctx, j, mrow = _tile_info(i)
    g = nm_ref[...]
    sc = 1.0 + _mod(mod_ref, mrow, 1)
    sh = _mod(mod_ref, mrow, 0)
    pv = jnp.where(jnp.logical_and(jnp.logical_not(is_ctx), j > 0), 1.0, 0.0)
    nv = jnp.where(jnp.logical_and(jnp.logical_not(is_ctx), j < LT - 1), 1.0, 0.0)
    ht = _rms(xt, g) * sc + sh
    hp = (_rms(xp, g) * sc + sh) * pv
    hn = (_rms(xn, g) * sc + sh) * nv
    return ht, hp, hn, is_ctx, j, mrow


def _pool_kernel(split, *refs):
    if split:
        xc_ref, x_ref, xp_ref, xn_ref = refs[:4]
        refs = refs[4:]
    else:
        x_ref, xp_ref, xn_ref = refs[:3]
        refs = refs[3:]
    (mod_ref, nm_ref, band_ref, wp_ref, ps_ref, nf_ref, wrc_ref, wrh_ref, rb_ref,
     x1_ref, h2u_ref, ri_ref, rw_ref) = refs
    i = pl.program_id(0)
    x = x_ref[...]
    if split:
        x = jnp.where(i < NCT, xc_ref[...], x)
    ht, hp, hn, is_ctx, j, mrow = _normed_halo(x, xp_ref[...], xn_ref[...], i, mod_ref, nm_ref)
    hext = jnp.concatenate([ht, hp, hn, jnp.zeros((KB - TM - 2 * HALO, D), F32)], axis=0)
    e_hi = hext.astype(BF16)
    e_lo = (hext - e_hi.astype(F32)).astype(BF16)
    seq_len = jnp.where(is_ctx, CTX_LEN, LAT_LEN)
    t = j * TM + lax.broadcasted_iota(I32, (TM, POOL_CH), 0)
    outs = []
    for g, win in enumerate(POOL_WINDOWS):
        lo = win // 2
        hi = win - 1 - lo
        cols = slice(g * POOL_CH, (g + 1) * POOL_CH)
        band = band_ref[g]
        s = (jnp.dot(band, e_hi[:, cols], preferred_element_type=F32)
             + jnp.dot(band, e_lo[:, cols], preferred_element_type=F32))
        cnt = (jnp.minimum(t + hi, seq_len - 1) - jnp.maximum(t - lo, 0) + 1).astype(F32)
        d = s / cnt - ht[:, cols]
        outs.append(jnp.dot(d.astype(BF16), wp_ref[g], preferred_element_type=F32))
    o = jnp.concatenate(outs, axis=1) * ps_ref[...]
    x1 = x + _mod(mod_ref, mrow, 2) * o
    x1_ref[...] = x1
    _ffn_pre(x1, mrow, mod_ref, nf_ref, wrc_ref, wrh_ref, rb_ref, h2u_ref, ri_ref, rw_ref)


def _pool_band():
    band = np.zeros((len(POOL_WINDOWS), TM, KB), np.float32)
    r = np.arange(TM)[:, None]
    for g, win in enumerate(POOL_WINDOWS):
        lo = win // 2
        hi = win - 1 - lo
        pos = np.concatenate([np.arange(TM), np.arange(-HALO, 0), np.arange(TM, TM + HALO)])[None, :]
        band[g, :, :TM + 2 * HALO] = (pos >= r - lo) & (pos <= r + hi)
    return jnp.asarray(band, BF16)


def _split_halo_specs():
    nb = T_LAT // HALO
    per = TM // HALO
    lat = lambda i: jnp.maximum(i - NCT, 0)
    return [
        pl.BlockSpec((TM, D), lambda i: (jnp.minimum(i, NCT - 1), 0)),
        pl.BlockSpec((TM, D), lambda i: (lat(i), 0)),
        pl.BlockSpec((HALO, D), lambda i: (jnp.maximum(lat(i) * per - 1, 0), 0)),
        pl.BlockSpec((HALO, D), lambda i: (jnp.minimum((lat(i) + 1) * per, nb - 1), 0)),
    ]


def _pool_layer(xs, mod_l, nm, wp, ps, ffn_args):
    const2 = lambda i: (0, 0)
    const3 = lambda i: (0, 0, 0)
    split = isinstance(xs, tuple)
    x_args = (xs[0], xs[1], xs[1], xs[1]) if split else (xs, xs, xs)
    return pl.pallas_call(
        functools.partial(_pool_kernel, split),
        out_shape=_ffn_out_shapes(),
        grid=(NT,),
        in_specs=(_split_halo_specs() if split else _halo_specs(D)) + [
            pl.BlockSpec((8, 6 * D), const2),
            pl.BlockSpec((1, D), const2),
            pl.BlockSpec((len(POOL_WINDOWS), TM, KB), const3),
            pl.BlockSpec((len(POOL_WINDOWS), POOL_CH, POOL_CH), const3),
            pl.BlockSpec((1, D), const2),
        ] + _ffn_in_specs(),
        out_specs=_ffn_out_specs(),
        compiler_params=_cparams(1, 48),
        name="pool_mixer",
    )(*x_args, mod_l, nm, _pool_band(), wp.astype(BF16), ps, *ffn_args)


def _conv_kernel(x_ref, xp_ref, xn_ref, mod_ref, nm_ref, win_ref, cw_ref, wout_ref,
                 nf_ref, wrc_ref, wrh_ref, rb_ref,
                 x1_ref, h2u_ref, ri_ref, rw_ref):
    i = pl.program_id(0)
    x = x_ref[...]
    ht, hp, hn, is_ctx, j, mrow = _normed_halo(x, xp_ref[...], xn_ref[...], i, mod_ref, nm_ref)
    hext = jnp.concatenate([ht, hp, hn], axis=0).astype(BF16)
    bcv = jnp.dot(hext, win_ref[...], preferred_element_type=F32)
    b = bcv[:TM, :D]
    u = bcv[:, D:2 * D] * bcv[:, 2 * D:]
    um = u[:TM]
    u_before = u[TM + HALO - 1:TM + HALO]
    u_after = u[TM + HALO:TM + HALO + 1]
    row = lax.broadcasted_iota(I32, (TM, D), 0)
    up = jnp.where(row == 0, u_before, pltpu.roll(um, 1, 0))
    un = jnp.where(row == TM - 1, u_after, pltpu.roll(um, TM - 1, 0))
    cw = cw_ref[...]
    conv = up * cw[0:1] + um * cw[1:2] + un * cw[2:3]
    o = jnp.dot((b * conv).astype(BF16), wout_ref[...], preferred_element_type=F32)
    x1 = x + _mod(mod_ref, mrow, 2) * o
    x1_ref[...] = x1
    _ffn_pre(x1, mrow, mod_ref, nf_ref, wrc_ref, wrh_ref, rb_ref, h2u_ref, ri_ref, rw_ref)


def _conv_layer(x, mod_l, nm, w_in, cw, w_out, ffn_args):
    const2 = lambda i: (0, 0)
    cw8 = jnp.concatenate([cw, jnp.zeros((8 - cw.shape[0], D), F32)], axis=0)
    return pl.pallas_call(
        _conv_kernel,
        out_shape=_ffn_out_shapes(),
        grid=(NT,),
        in_specs=_halo_specs(D) + [
            pl.BlockSpec((8, 6 * D), const2),
            pl.BlockSpec((1, D), const2),
            pl.BlockSpec((D, 3 * D), const2),
            pl.BlockSpec((8, D), const2),
            pl.BlockSpec((D, D), const2),
        ] + _ffn_in_specs(),
        out_specs=_ffn_out_specs(),
        compiler_params=_cparams(1, 56),
        name="conv_mixer",
    )(x, x, x, mod_l, nm, w_in.astype(BF16), cw8, w_out.astype(BF16), *ffn_args)


W_CAT = Q_RANK + KV_RANK + 2 * QK_ROPE
QH = 2 * QK_NOPE
VH = 2 * V_DIM


def _mla_proj_kernel(x_ref, mod_ref, nm_ref, wcat_ref, qn_ref, kvn_ref, wq_ref, wqs_ref,
                     c_ref, s_ref, q_ref, ckv_ref, kpe_ref):
    i = pl.program_id(0)
    _, _, mrow = _tile_info(i)
    h = _rms(x_ref[...], nm_ref[...]) * (1.0 + _mod(mod_ref, mrow, 1)) + _mod(mod_ref, mrow, 0)
    y = jnp.dot(h.astype(BF16), wcat_ref[...], preferred_element_type=F32)
    cqn = _rms(y[:, :Q_RANK], qn_ref[...]).astype(BF16)
    q = jnp.dot(cqn, wq_ref[...], preferred_element_type=F32)
    qs = jnp.dot(cqn, wqs_ref[...], preferred_element_type=F32)
    cos = c_ref[...]
    sin = s_ref[...]
    parts = []
    for hh in range(HEADS):
        parts.append(q[:, hh * QH:hh * QH + QK_NOPE])
        parts.append(q[:, hh * QH + QK_NOPE:(hh + 1) * QH] * cos + qs[:, hh * 128:(hh + 1) * 128] * sin)
    q_ref[...] = (jnp.concatenate(parts, axis=1) * (SM_SCALE * LOG2E)).astype(BF16)
    ckv_ref[...] = _rms(y[:, Q_RANK:Q_RANK + KV_RANK], kvn_ref[...])
    k2 = y[:, Q_RANK + KV_RANK:]
    kr = k2 * cos + pltpu.roll(k2, QK_ROPE, 1) * sin
    lane = lax.broadcasted_iota(I32, (TM, 2 * QK_ROPE), 1)
    kpe_ref[...] = jnp.where(lane < QK_ROPE, kr, 0.0)


def _rope_tables():
    rows_n = LAT_LEN // GRID_W
    rows = jnp.repeat(jnp.arange(rows_n), GRID_W).astype(F32)
    cols = jnp.tile(jnp.arange(GRID_W), rows_n).astype(F32)
    inv = ROPE_BASE ** (-(jnp.arange(AXIS_FREQS, dtype=F32) / AXIS_FREQS))
    ang = jnp.stack([rows[:, None] * inv, cols[:, None] * inv], axis=1)
    cos, sin = jnp.cos(ang), jnp.sin(ang)
    c64 = jnp.concatenate([cos[:, 0], cos[:, 0], cos[:, 1], cos[:, 1]], axis=1)
    s64 = jnp.concatenate([-sin[:, 0], sin[:, 0], -sin[:, 1], sin[:, 1]], axis=1)
    c = jnp.concatenate([c64, jnp.ones((LAT_LEN, QK_ROPE), F32)], axis=1)
    s = jnp.concatenate([s64, jnp.zeros((LAT_LEN, QK_ROPE), F32)], axis=1)
    c = jnp.concatenate([c, jnp.ones((TM, 2 * QK_ROPE), F32)], axis=0)
    s = jnp.concatenate([s, jnp.zeros((TM, 2 * QK_ROPE), F32)], axis=0)
    return c, s


def _swap_rope_cols(w):
    f = AXIS_FREQS
    return jnp.concatenate([w[..., f:2 * f], w[..., :f], w[..., 3 * f:], w[..., 2 * f:3 * f]], axis=-1)


def _mla_proj(x, mod_l, nm, w_dq, q_norm, w_uq, w_dkv, kv_norm):
    const2 = lambda i: (0, 0)
    w_kpe = w_dkv[:, KV_RANK:]
    wcat = jnp.concatenate([w_dq, w_dkv[:, :KV_RANK], w_kpe, _swap_rope_cols(w_kpe)], axis=1).astype(BF16)
    wq = w_uq.reshape(Q_RANK, HEADS, QK_NOPE + QK_ROPE)
    zpad = jnp.zeros((Q_RANK, HEADS, QK_ROPE), F32)
    wq_a = jnp.concatenate([wq, zpad], axis=2).reshape(Q_RANK, HEADS * QH).astype(BF16)
    wq_s = jnp.concatenate([_swap_rope_cols(wq[:, :, QK_NOPE:]), zpad], axis=2)
    wq_s = wq_s.reshape(Q_RANK, HEADS * 128).astype(BF16)
    cos, sin = _rope_tables()
    tab_idx = lambda i: (jnp.where(i < NCT, LT, jnp.maximum(i - NCT, 0) & (LT - 1)), 0)
    return pl.pallas_call(
        _mla_proj_kernel,
        out_shape=[jax.ShapeDtypeStruct((T, HEADS * QH), BF16),
                   jax.ShapeDtypeStruct((T, KV_RANK), F32),
                   jax.ShapeDtypeStruct((T, 2 * QK_ROPE), F32)],
        grid=(NT,),
        in_specs=[
            pl.BlockSpec((TM, D), lambda i: (i, 0)),
            pl.BlockSpec((8, 6 * D), const2),
            pl.BlockSpec((1, D), const2),
            pl.BlockSpec((D, W_CAT), const2),
            pl.BlockSpec((1, Q_RANK), const2),
            pl.BlockSpec((1, KV_RANK), const2),
            pl.BlockSpec((Q_RANK, HEADS * QH), const2),
            pl.BlockSpec((Q_RANK, HEADS * 128), const2),
            pl.BlockSpec((TM, 2 * QK_ROPE), tab_idx),
            pl.BlockSpec((TM, 2 * QK_ROPE), tab_idx),
        ],
        out_specs=[pl.BlockSpec((TM, HEADS * QH), lambda i: (i, 0)),
                   pl.BlockSpec((TM, KV_RANK), lambda i: (i, 0)),
                   pl.BlockSpec((TM, 2 * QK_ROPE), lambda i: (i, 0))],
        compiler_params=_cparams(1, 48),
        name="mla_proj",
    )(x, mod_l, nm, wcat, q_norm, kv_norm, wq_a, wq_s, cos, sin)


def _kv_expand_kernel(ckv_ref, kpe_ref, wk_ref, wv_ref, k_ref, v_ref):
    c = ckv_ref[...].astype(BF16)
    kn = jnp.dot(c, wk_ref[...], preferred_element_type=F32).astype(BF16)
    kp = kpe_ref[...].astype(BF16)
    parts = []
    for hh in range(HEADS):
        parts.append(kn[:, hh * QK_NOPE:(hh + 1) * QK_NOPE])
        parts.append(kp)
    k_ref[...] = jnp.concatenate(parts, axis=1)
    v = jnp.dot(c, wv_ref[...], preferred_element_type=F32).astype(BF16)
    lane = lax.broadcasted_iota(I32, (TM, VH - V_DIM), 1)
    one = jnp.where(lane == 0, 1.0, 0.0).astype(BF16)
    parts = []
    for hh in range(HEADS):
        parts.append(v[:, hh * V_DIM:(hh + 1) * V_DIM])
        parts.append(one)
    v_ref[...] = jnp.concatenate(parts, axis=1)


def _kv_expand(ckv, kpe, wk, wv, name):
    n = ckv.shape[0]
    const2 = lambda i: (0, 0)
    return pl.pallas_call(
        _kv_expand_kernel,
        out_shape=[jax.ShapeDtypeStruct((n, HEADS * QH), BF16),
                   jax.ShapeDtypeStruct((n, HEADS * VH), BF16)],
        grid=(n // TM,),
        in_specs=[
            pl.BlockSpec((TM, KV_RANK), lambda i: (i, 0)),
            pl.BlockSpec((TM, 2 * QK_ROPE), lambda i: (i, 0)),
            pl.BlockSpec((KV_RANK, HEADS * QK_NOPE), const2),
            pl.BlockSpec((KV_RANK, HEADS * V_DIM), const2),
        ],
        out_specs=[pl.BlockSpec((TM, HEADS * QH), lambda i: (i, 0)),
                   pl.BlockSpec((TM, HEADS * VH), lambda i: (i, 0))],
        compiler_params=_cparams(1, 32),
        name=name,
    )(ckv, kpe, wk, wv)


def _attend(q, k, v):
    s = _dot_nt(q, k)
    p = jnp.exp2(s - jnp.max(s, axis=1, keepdims=True))
    o = jnp.dot(p.astype(BF16), v, preferred_element_type=F32)
    return o[:, :V_DIM] / o[:, V_DIM:V_DIM + 1]


KC = 512
NKC = NK_LAT // KC


def _attn_lat_kernel(q_ref, k_ref, v_ref, o_ref, s_ref, m_ref, acc_ref):
    q = q_ref[...]
    m_ref[...] = jnp.full((TQ, 128), -jnp.inf, F32)

    def scores(c, carry):
        k = k_ref[pl.ds(pl.multiple_of(c * KC, KC), KC), :]
        s = _dot_nt(q, k)
        s_ref[c] = s
        m = m_ref[...]
        for j in range(KC // 128):
            m = jnp.maximum(m, s[:, j * 128:(j + 1) * 128])
        m_ref[...] = m
        return carry
    lax.fori_loop(0, NKC, scores, 0, unroll=True)

    mb = jnp.broadcast_to(jnp.max(m_ref[...], axis=1, keepdims=True), (TQ, 128))
    acc_ref[...] = jnp.zeros((TQ, VH), F32)

    def weighted(c, carry):
        s = s_ref[c]
        p = jnp.concatenate([jnp.exp2(s[:, j * 128:(j + 1) * 128] - mb) for j in range(KC // 128)], axis=1)
        v = v_ref[pl.ds(pl.multiple_of(c * KC, KC), KC), :]
        acc_ref[...] += jnp.dot(p.astype(BF16), v, preferred_element_type=F32)
        return carry
    lax.fori_loop(0, NKC, weighted, 0, unroll=3)

    acc = acc_ref[...]
    o_ref[...] = (acc[:, :V_DIM] / acc[:, V_DIM:V_DIM + 1]).astype(BF16)


def _attn_lat(q, k, v):
    nq = LAT_LEN // TQ
    q0 = T_CTX // TQ
    return pl.pallas_call(
        _attn_lat_kernel,
        scratch_shapes=[pltpu.VMEM((NKC, TQ, KC), F32), pltpu.VMEM((TQ, 128), F32), pltpu.VMEM((TQ, VH), F32)],
        out_shape=jax.ShapeDtypeStruct((T_LAT, HEADS * V_DIM), BF16),
        grid=(N_LAT_SEQ, HEADS, nq),
        in_specs=[
            pl.BlockSpec((TQ, QH), lambda b, h, t: (q0 + b * nq + t, h)),
            pl.BlockSpec((NK_LAT, QH), lambda b, h, t: (b, h)),
            pl.BlockSpec((NK_LAT, VH), lambda b, h, t: (b, h)),
        ],
        out_specs=pl.BlockSpec((TQ, V_DIM), lambda b, h, t: (b * nq + t, h)),
        compiler_params=_cparams(3, 48),
        name="attn_latent",
    )(q, k, v)


def _attn_ctx_kernel(q_ref, k_ref, v_ref, o_ref):
    outs = []
    for hh in range(HEADS):
        outs.append(_attend(q_ref[:, hh * QH:(hh + 1) * QH], k_ref[:, hh * QH:(hh + 1) * QH],
                            v_ref[:, hh * VH:(hh + 1) * VH]))
    o_ref[...] = jnp.concatenate(outs, axis=1).astype(BF16)


def _attn_ctx(q, k, v):
    return pl.pallas_call(
        _attn_ctx_kernel,
        out_shape=jax.ShapeDtypeStruct((T_CTX, HEADS * V_DIM), BF16),
        grid=(N_CTX_SEQ,),
        in_specs=[
            pl.BlockSpec((CTX_LEN, HEADS * QH), lambda b: (b, 0)),
            pl.BlockSpec((CTX_LEN, HEADS * QH), lambda b: (b, 0)),
            pl.BlockSpec((CTX_LEN, HEADS * VH), lambda b: (b, 0)),
        ],
        out_specs=pl.BlockSpec((CTX_LEN, HEADS * V_DIM), lambda b: (b, 0)),
        compiler_params=_cparams(1, 32),
        name="attn_context",
    )(q, k, v)


def _attn_out_kernel(x_ref, oc_ref, ol_ref, mod_ref, wo_ref,
                     nf_ref, wrc_ref, wrh_ref, rb_ref,
                     x1_ref, h2u_ref, ri_ref, rw_ref):
    i = pl.program_id(0)
    is_ctx, _, mrow = _tile_info(i)
    att = jnp.where(is_ctx, oc_ref[...], ol_ref[...])
    o = jnp.dot(att, wo_ref[...], preferred_element_type=F32)
    x1 = x_ref[...] + _mod(mod_ref, mrow, 2) * o
    x1_ref[...] = x1
    _ffn_pre(x1, mrow, mod_ref, nf_ref, wrc_ref, wrh_ref, rb_ref, h2u_ref, ri_ref, rw_ref)


def _attn_out(x, o_ctx, o_lat, mod_l, w_o, ffn_args):
    const2 = lambda i: (0, 0)
    return pl.pallas_call(
        _attn_out_kernel,
        out_shape=_ffn_out_shapes(),
        grid=(NT,),
        in_specs=[
            pl.BlockSpec((TM, D), lambda i: (i, 0)),
            pl.BlockSpec((TM, D), lambda i: (jnp.minimum(i, NCT - 1), 0)),
            pl.BlockSpec((TM, D), lambda i: (jnp.maximum(i - NCT, 0), 0)),
            pl.BlockSpec((8, 6 * D), const2),
            pl.BlockSpec((D, D), const2),
        ] + _ffn_in_specs(),
        out_specs=_ffn_out_specs(),
        compiler_params=_cparams(1, 48),
        name="attn_out",
    )(x, o_ctx, o_lat, mod_l, w_o.astype(BF16), *ffn_args)


def _mla_layer(x, mod_l, nm, cache_ckv, cache_kpe, w_dq, q_norm, w_uq, w_dkv, kv_norm, w_ukv, w_o, ffn_args):
    q, ckv, kpe = _mla_proj(x, mod_l, nm, w_dq, q_norm, w_uq, w_dkv, kv_norm)
    wkv = w_ukv.reshape(KV_RANK, HEADS, QK_NOPE + V_DIM)
    wk = wkv[:, :, :QK_NOPE].reshape(KV_RANK, HEADS * QK_NOPE).astype(BF16)
    wv = wkv[:, :, QK_NOPE:].reshape(KV_RANK, HEADS * V_DIM).astype(BF16)
    k_c, v_c = _kv_expand(ckv[:T_CTX], kpe[:T_CTX], wk, wv, "kv_expand_context")
    kpe_cache = jnp.concatenate([cache_kpe, jnp.zeros_like(cache_kpe)], axis=-1)
    ckv_l = jnp.concatenate([cache_ckv, ckv[T_CTX:].reshape(N_LAT_SEQ, LAT_LEN, KV_RANK)], axis=1)
    kpe_l = jnp.concatenate([kpe_cache, kpe[T_CTX:].reshape(N_LAT_SEQ, LAT_LEN, 2 * QK_ROPE)], axis=1)
    k_l, v_l = _kv_expand(ckv_l.reshape(N_LAT_SEQ * NK_LAT, KV_RANK),
                          kpe_l.reshape(N_LAT_SEQ * NK_LAT, 2 * QK_ROPE), wk, wv, "kv_expand_latent")
    o_c = _attn_ctx(q, k_c, v_c)
    o_l = _attn_lat(q, k_l, v_l)
    outs = _attn_out(x, o_c, o_l, mod_l, w_o, ffn_args)
    new_ckv = ckv[:T_CTX].reshape(N_CTX_SEQ, 1, CTX_LEN, KV_RANK)
    new_kpe = kpe[:T_CTX, :QK_ROPE].reshape(N_CTX_SEQ, 1, CTX_LEN, QK_ROPE)
    return outs, new_ckv, new_kpe


GROUP = 8


def _slab(ref, row, n=1):
    return ref.at[pl.ds(pl.multiple_of(row * NS, NS), n * NS)]


def _dispatch_kernel(meta_ref, psv_ref, ri_ref, h2_ref, xs_ref, pos_ref, stage, zbuf, hbuf, *rest):
    bufs, (sem, dsem, zsem, hsem) = rest[:-4], rest[-4:]
    i = pl.program_id(0)

    def row_copies(b):
        return [pltpu.make_async_copy(stage.at[b, row], bufs[b * 2 * LC + row], sem.at[b])
                for row in range(2 * LC)]

    def tile_fetch(tile):
        return pltpu.make_async_copy(_slab(h2_ref, tile * TM, TM), hbuf.at[tile % 3], hsem.at[tile % 3])

    def tile_rows_done(b):
        return pltpu.make_async_copy(_slab(h2_ref, 0, 2 * TM), _slab(xs_ref, 0, 2 * TM), dsem.at[b])

    @pl.when(i == 0)
    def _():
        zbuf[...] = jnp.zeros(zbuf.shape, F32)

        def pad_fill(e):
            return pltpu.make_async_copy(zbuf, _slab(xs_ref, meta_ref[2, e] - TS, TS), zsem)

        def start(e, c):
            @pl.when(meta_ref[2, e] > meta_ref[0, e])
            def _():
                pad_fill(e).start()
            return c

        def wait(e, c):
            @pl.when(meta_ref[2, e] > meta_ref[0, e])
            def _():
                pad_fill(e).wait()
            return c
        lax.fori_loop(0, N_EXPERTS, start, 0)
        lax.fori_loop(0, N_EXPERTS, wait, 0)

        def tail_fill(tile):
            return pltpu.make_async_copy(zbuf, _slab(xs_ref, tile * TS, TS), zsem)

        def tail_start(tile, c):
            tail_fill(tile).start()
            return c

        def tail_wait(tile, c):
            tail_fill(tile).wait()
            return c
        first_unused = lax.shift_right_logical(meta_ref[2, N_EXPERTS - 1], jnp.int32(TS.bit_length() - 1))
        lax.fori_loop(first_unused, NST, tail_start, 0)
        lax.fori_loop(first_unused, NST, tail_wait, 0)

    @pl.when(i < NT)
    def _():
        tile_fetch(i).start()
        ri = ri_ref[...]
        rowe = lax.broadcasted_iota(I32, (N_EXPERTS, TM), 0)
        ps = psv_ref[...]
        pos0 = jnp.sum(jnp.where(rowe == ri[0:1], ps, 0), axis=0, keepdims=True) + ri[2:3]
        pos1 = jnp.sum(jnp.where(rowe == ri[1:2], ps, 0), axis=0, keepdims=True) + ri[3:4]
        row8 = lax.broadcasted_iota(I32, (8, 128), 0)
        st = jnp.zeros((8, 128), I32)
        for k, pos in enumerate((pos0, pos1)):
            for h in range(LC):
                st = jnp.where(row8 == k * LC + h, pos[:, h * 128:(h + 1) * 128], st)
        pos_ref[...] = st
        for b in range(2):
            @pl.when(i % 2 == b)
            def _(b=b):
                stage[b] = st
                for c in row_copies(b):
                    c.start()

    j = i - 1
    for b in range(2):
        @pl.when(jnp.logical_and(i >= 1, j % 2 == b))
        def _(b=b):
            for c in row_copies(b):
                c.wait()
            tile_fetch(j).wait()
            rows = hbuf.at[j % 3]
            for h in range(LC):
                def toks(g, c, h=h):
                    t0 = g * GROUP
                    slots = [[bufs[b * 2 * LC + k * LC + h][t0 + u] for k in range(2)] for u in range(GROUP)]
                    for u in range(GROUP):
                        src = _slab(rows, h * 128 + t0 + u)
                        for k in range(2):
                            pltpu.make_async_copy(src, _slab(xs_ref, slots[u][k]), dsem.at[b]).start(priority=k)
                    return c
                lax.fori_loop(0, 128 // GROUP, toks, 0)

            @pl.when(j >= 1)
            def _():
                tile_rows_done(1 - b).wait()

            @pl.when(i == NT)
            def _():
                tile_rows_done(b).wait()


def _rank_kernel(ri_ref, trio_ref, ro_ref, cnt_ref, carry_ref):
    @pl.when(pl.program_id(0) == 0)
    def _():
        carry_ref[...] = jnp.zeros_like(carry_ref)

    rowe = lax.broadcasted_iota(I32, (N_EXPERTS, TM), 0)
    row8 = lax.broadcasted_iota(I32, (8, TM), 0)
    ids = [ri_ref[tt] for tt in range(RG)]
    ohs = [rowe == ids[tt][k:k + 1] for tt in range(RG) for k in range(2)]
    ohb = jnp.concatenate([jnp.where(oh, 1.0, 0.0).astype(BF16) for oh in ohs], axis=0)
    pt = jnp.dot(ohb, trio_ref[...], preferred_element_type=F32)
    carry = carry_ref[...]
    for tt in range(RG):
        ranks = []
        for k in range(2):
            r0 = (tt * 2 + k) * N_EXPERTS
            before = carry + pt[r0:r0 + N_EXPERTS, :TM]
            ranks.append(jnp.sum(jnp.where(ohs[tt * 2 + k], before, 0.0), axis=0, keepdims=True).astype(I32))
            carry = carry + pt[r0:r0 + N_EXPERTS, TM:]
        ro_ref[tt] = jnp.where(row8 == 0, ids[tt][0:1],
                     jnp.where(row8 == 1, ids[tt][1:2],
                     jnp.where(row8 == 2, ranks[0],
                     jnp.where(row8 == 3, ranks[1], 0))))
    carry_ref[...] = carry
    cnt_ref[...] = carry


def _rank(route_i):
    tri = np.triu(np.ones((TM, TM), np.float32), 1)
    trio = jnp.asarray(np.concatenate([tri, np.ones((TM, TM), np.float32)], axis=1), BF16)
    return pl.pallas_call(
        _rank_kernel,
        out_shape=[jax.ShapeDtypeStruct((NT, 8, TM), I32), jax.ShapeDtypeStruct((N_EXPERTS, TM), F32)],
        grid=(NT // RG,),
        in_specs=[pl.BlockSpec((RG, 8, TM), lambda g: (g, 0, 0)),
                  pl.BlockSpec((TM, 2 * TM), lambda g: (0, 0))],
        out_specs=[pl.BlockSpec((RG, 8, TM), lambda g: (g, 0, 0)),
                   pl.BlockSpec((N_EXPERTS, TM), lambda g: (0, 0))],
        scratch_shapes=[pltpu.VMEM((N_EXPERTS, TM), F32)],
        compiler_params=_cparams(1, 16),
        name="expert_rank",
    )(route_i, trio)


def _dispatch(meta, pad_starts, route_i, h2):
    tile = lambda i: (jnp.minimum(i, NT - 1), 0, 0)
    return pl.pallas_call(
        _dispatch_kernel,
        out_shape=[jax.ShapeDtypeStruct((CAP * NS, 128), F32),
                   jax.ShapeDtypeStruct((NT, 8, 128), I32)],
        grid=(NT + 1,),
        in_specs=[pl.BlockSpec(memory_space=pltpu.SMEM),
                  pl.BlockSpec((N_EXPERTS, TM), lambda i: (0, 0)),
                  pl.BlockSpec((None, 8, TM), tile),
                  pl.BlockSpec(memory_space=pl.ANY)],
        out_specs=[pl.BlockSpec(memory_space=pl.ANY), pl.BlockSpec((None, 8, 128), tile)],
        scratch_shapes=([pltpu.VMEM((2, 8, 128), I32), pltpu.VMEM((TS * NS, 128), F32),
                         pltpu.VMEM((3, TM * NS, 128), F32)]
                        + [pltpu.SMEM((128,), I32)] * (4 * LC)
                        + [pltpu.SemaphoreType.DMA((2,)), pltpu.SemaphoreType.DMA((2,)), pltpu.SemaphoreType.DMA,
                           pltpu.SemaphoreType.DMA((3,))]),
        compiler_params=_cparams(1, 16),
        name="moe_dispatch",
    )(meta, jnp.broadcast_to(pad_starts[:, None], (N_EXPERTS, TM)), route_i, h2)


def _moe_kernel(blk_ref, nu_ref, xs_ref, w1_ref, w3_ref, w2_ref, o_ref, wb1, wb3, wb2, pe_ref):
    i = pl.program_id(0)
    n_used = nu_ref[0]

    @pl.when(i == 0)
    def _():
        pe_ref[0] = -1

    @pl.when(i >= n_used)
    def _():
        o_ref[...] = jnp.zeros(o_ref.shape, F32)

    @pl.when(i < n_used)
    def _():
        e = blk_ref[i]

        @pl.when(e != pe_ref[0])
        def _():
            wb1[...] = w1_ref[...].astype(BF16)
            wb3[...] = w3_ref[...].astype(BF16)
            wb2[...] = w2_ref[...].astype(BF16)
            pe_ref[0] = e

        xb = _load_slabs(xs_ref, TS).astype(BF16)
        a = jnp.dot(xb, wb1[...], preferred_element_type=F32)
        b = jnp.dot(xb, wb3[...], preferred_element_type=F32)
        hm = (a * jax.nn.sigmoid(a) * b).astype(BF16)
        _store_slabs(o_ref, jnp.dot(hm, wb2[...], preferred_element_type=F32))


def _moe_experts(l, blk_e, n_used, xs, w1, w3, w2):
    grid_spec = pltpu.PrefetchScalarGridSpec(
        num_scalar_prefetch=2,
        grid=(NST,),
        in_specs=[
            pl.BlockSpec((TS * NS, 128), lambda i, blk, nu: (jnp.minimum(i, nu[0] - 1), 0)),
            pl.BlockSpec((None, None, D, D_EXPERT), lambda i, blk, nu: (l, blk[i], 0, 0)),
            pl.BlockSpec((None, None, D, D_EXPERT), lambda i, blk, nu: (l, blk[i], 0, 0)),
            pl.BlockSpec((None, None, D_EXPERT, D), lambda i, blk, nu: (l, blk[i], 0, 0)),
        ],
        out_specs=pl.BlockSpec((TS * NS, 128), lambda i, blk, nu: (i, 0)),
        scratch_shapes=[
            pltpu.VMEM((D, D_EXPERT), BF16),
            pltpu.VMEM((D, D_EXPERT), BF16),
            pltpu.VMEM((D_EXPERT, D), BF16),
            pltpu.SMEM((1,), I32),
        ],
    )
    return pl.pallas_call(
        _moe_kernel,
        out_shape=jax.ShapeDtypeStruct((CAP * NS, 128), F32),
        grid_spec=grid_spec,
        compiler_params=_cparams(1, 48),
        name="moe_experts",
    )(blk_e, n_used, xs, w1, w3, w2)


def _lane_to_col(row):
    r = lax.broadcasted_iota(I32, (TM, TM), 0)
    c = lax.broadcasted_iota(I32, (TM, TM), 1)
    return jnp.sum(jnp.where(r == c, row, 0.0), axis=1, keepdims=True)


def _combine_kernel(final, x1_ref, pos_ref, os_ref, rw_ref, mod_ref, nfin_ref, *rest):
    n_out = 2 if final else 1
    y_refs = rest[:n_out]
    gbuf = rest[n_out]
    bufs = rest[n_out + 1:-2]
    psem, gsem = rest[-2:]
    i = pl.program_id(0)

    def pos_copies(tile, b):
        return [pltpu.make_async_copy(pos_ref.at[tile, row], bufs[b * 2 * LC + row], psem.at[b])
                for row in range(2 * LC)]

    def start_gathers(b):
        for k in range(2):
            for h in range(LC):
                def tok(t, c, k=k, h=h):
                    src = _slab(os_ref, bufs[b * 2 * LC + k * LC + h][t])
                    pltpu.make_async_copy(src, _slab(gbuf.at[2 * b + k], h * 128 + t), gsem.at[b]).start(priority=k)
                    return c
                lax.fori_loop(0, 128, tok, 0, unroll=8)

    def wait_gathers(b):
        for k in range(2):
            pltpu.make_async_copy(_slab(os_ref, 0, TM), gbuf.at[2 * b + k], gsem.at[b]).wait()

    @pl.when(i == 0)
    def _():
        for c in pos_copies(0, 0):
            c.start()
        for c in pos_copies(0, 0):
            c.wait()
        start_gathers(0)
        if NT > 1:
            for c in pos_copies(1, 1):
                c.start()

    for b in range(2):
        @pl.when(i % 2 == b)
        def _(b=b):
            @pl.when(i + 1 < NT)
            def _():
                for c in pos_copies(i + 1, 1 - b):
                    c.wait()
                start_gathers(1 - b)

            @pl.when(i + 2 < NT)
            def _():
                for c in pos_copies(i + 2, b):
                    c.start()
            wait_gathers(b)

    _, _, mrow = _tile_info(i)
    rw = rw_ref[...]
    w0 = _lane_to_col(rw[0:1])
    w1 = _lane_to_col(rw[1:2])
    g2 = _mod(mod_ref, mrow, 5)
    par = i % 2
    x2 = x1_ref[...] + g2 * (_load_slabs(gbuf.at[2 * par], TM) * w0 + _load_slabs(gbuf.at[2 * par + 1], TM) * w1)
    if final:
        yc_ref, yl_ref = y_refs
        y = _rms(x2, nfin_ref[...])

        @pl.when(i < NCT)
        def _():
            yc_ref[...] = y

        @pl.when(i >= NCT)
        def _():
            yl_ref[...] = y
    else:
        y_refs[0][...] = x2


def _combine(x1, pos, o_sorted, route_w, mod_l, norm_final, final):
    const2 = lambda i: (0, 0)
    if final:
        out_shape = [jax.ShapeDtypeStruct((T_CTX, D), F32), jax.ShapeDtypeStruct((T_LAT, D), F32)]
        out_specs = [pl.BlockSpec((TM, D), lambda i: (jnp.minimum(i, NCT - 1), 0)),
                     pl.BlockSpec((TM, D), lambda i: (jnp.maximum(i - NCT, 0), 0))]
    else:
        out_shape = [jax.ShapeDtypeStruct((T, D), F32)]
        out_specs = [pl.BlockSpec((TM, D), lambda i: (i, 0))]
    return pl.pallas_call(
        functools.partial(_combine_kernel, final),
        out_shape=out_shape,
        grid=(NT,),
        in_specs=[
            pl.BlockSpec((TM, D), lambda i: (i, 0)),
            pl.BlockSpec(memory_space=pl.ANY),
            pl.BlockSpec(memory_space=pl.ANY),
            pl.BlockSpec((None, 8, TM), lambda i: (i, 0, 0)),
            pl.BlockSpec((8, 6 * D), const2),
            pl.BlockSpec((1, D), const2),
        ],
        out_specs=out_specs,
        scratch_shapes=([pltpu.VMEM((4, TM * NS, 128), F32)] + [pltpu.SMEM((128,), I32)] * (4 * LC)
                        + [pltpu.SemaphoreType.DMA((2,)), pltpu.SemaphoreType.DMA((2,))]),
        compiler_params=_cparams(1, 32),
        name="moe_combine",
    )(x1, pos, o_sorted, route_w, mod_l, norm_final)


def _moe_layer(l, x1, h2u, route_ids, route_w, mod_l, w1, w3, w2, norm_final, final):
    route_i, counts = _rank(route_ids)
    sizes = counts[:, 0].astype(I32)
    padded = (sizes + TS - 1) // TS * TS
    pad_ends = jnp.cumsum(padded)
    pad_starts = pad_ends - padded
    n_used = (pad_ends[-1:] // TS).astype(I32)
    tile_start = jnp.arange(NST, dtype=I32) * TS
    blk_e = jnp.minimum(jnp.sum((pad_ends[None, :] <= tile_start[:, None]).astype(I32), axis=1),
                        N_EXPERTS - 1).astype(I32)
    meta = jnp.stack([pad_starts, pad_starts + sizes, pad_ends]).astype(I32)
    xs, pos = _dispatch(meta, pad_starts.astype(I32), route_i, h2u)
    o_sorted = _moe_experts(l, blk_e, n_used, xs, w1, w3, w2)
    return _combine(x1, pos, o_sorted, route_w, mod_l, norm_final, final)


def _routing_params(w_rg, b_rg, w_re, b_re):
    w = jnp.zeros((D, NRL), F32).at[:, :N_GROUPS].set(w_rg).at[:, ER0:ER0 + N_EXPERTS].set(w_re)
    hi = w.astype(BF16)
    lo = (w - hi.astype(F32)).astype(BF16)
    bias = jnp.zeros((1, NRL), F32).at[0, :N_GROUPS].set(b_rg).at[0, ER0:ER0 + N_EXPERTS].set(b_re)
    return jnp.concatenate([hi, lo], axis=1), hi, bias


def kernel(x_prompt, x_sample, cache_ckv, cache_kpe, c, c_ctx, norm_mix, norm_ffn, norm_final, w_ada, b_ada, w_pool, pool_scale, w_conv_in, conv_w, w_conv_out, w_dq, q_norm, w_uq, w_dkv, kv_norm, w_ukv, w_o, w_route_g, b_route_g, w_route_e, b_route_e, w1, w3, w2):
    x = (x_prompt.reshape(T_CTX, D), x_sample.reshape(T_LAT, D))
    cs =jnp.concatenate([c_ctx[None, :], c, jnp.zeros((8 - 1 - N_LAT_SEQ, D), F32)], axis=0)
    mod_all = _modulation(cs, w_ada, b_ada)
    nfin = norm_final[None, :]
    new_ckv = new_kpe = None
    for l in range(DEPTH):
        kind, j = l % 3, l // 3
        mod_l = mod_all[l]
        nm = norm_mix[l][None, :]
        ffn_args = (norm_ffn[l][None, :],) + _routing_params(w_route_g[l], b_route_g[l], w_route_e[l], b_route_e[l])
        if kind == 0:
            outs = _pool_layer(x, mod_l, nm, w_pool[j], pool_scale[j][None, :], ffn_args)
        elif kind == 1:
            outs = _conv_layer(x, mod_l, nm, w_conv_in[j], conv_w[j], w_conv_out[j], ffn_args)
        else:
            outs, new_ckv, new_kpe = _mla_layer(
                x, mod_l, nm, cache_ckv[:, j], cache_kpe[:, j], w_dq[j], q_norm[j][None, :], w_uq[j],
                w_dkv[j], kv_norm[j][None, :], w_ukv[j], w_o[j], ffn_args)
        x1, h2u, route_ids, route_w = outs
        ys = _moe_layer(l, x1, h2u, route_ids, route_w, mod_l, w1, w3, w2, nfin, l == DEPTH - 1)
        x = ys[0]
    y_prompt = ys[0].reshape(N_CTX_SEQ, CTX_LEN, D)
    y_sample = ys[1].reshape(N_LAT_SEQ, LAT_LEN, D)
    return (y_prompt, y_sample, new_ckv, new_kpe)
```

```python
import functools

import numpy as np
import jax
import jax.numpy as jnp
from jax import lax
from jax.experimental import pallas as pl
from jax.experimental.pallas import tpu as pltpu

F32 = jnp.float32
BF16 = jnp.bfloat16
I32 = jnp.int32

D = 1024
N_CTX_SEQ, CTX_LEN = 32, 256
N_LAT_SEQ, LAT_LEN = 4, 4096
PAST = 512
DEPTH = 4
GRID_W = 64
POOL_WINDOWS = (2, 4, 8, 16)
POOL_CH = D // 4
HEADS = 8
QK_NOPE, QK_ROPE, V_DIM = 128, 64, 128
Q_RANK, KV_RANK = 512, 256
AXIS_FREQS = QK_ROPE // 4
ROPE_BASE = 10000.0
SM_SCALE = (QK_NOPE + QK_ROPE) ** -0.5
LOG2E = 1.4426950408889634
N_GROUPS, EXP_PER_GROUP = 4, 8
N_EXPERTS = N_GROUPS * EXP_PER_GROUP
D_EXPERT = 512
EPS = 1e-6

T_CTX = N_CTX_SEQ * CTX_LEN
T_LAT = N_LAT_SEQ * LAT_LEN
T = T_CTX + T_LAT
TM = 256
NT = T // TM
NCT = T_CTX // TM
LT = LAT_LEN // TM
LT_SHIFT = LT.bit_length() - 1
HALO = 8
KB = TM + 128
NRL = 128
ER0 = 16
RG = 4
LC = TM // 128
assert 2 * LC <= 8 and NT % RG == 0
TS = 512
CAP = 2 * T + N_EXPERTS * TS
NST = CAP // TS
NS = D // 128
NK_LAT = PAST + LAT_LEN
TQ = 1024
MIB = 1024 * 1024

_HI = lax.Precision.HIGHEST


def _cparams(n_axes, vmem_mib):
    return pltpu.CompilerParams(
        dimension_semantics=("arbitrary",) * n_axes,
        vmem_limit_bytes=vmem_mib * MIB)


def _rms(x, g):
    return x * lax.rsqrt(jnp.mean(x * x, axis=-1, keepdims=True) + EPS) * g


def _tile_info(i):
    is_ctx = i < NCT
    jl = jnp.maximum(i - NCT, 0)
    j = jnp.where(is_ctx, 0, jl & (LT - 1))
    mrow = jnp.where(is_ctx, 0, 1 + lax.shift_right_logical(jl, LT_SHIFT))
    return is_ctx, j, mrow


def _mod(mod_ref, mrow, k):
    return mod_ref[pl.ds(mrow, 1), k * D:(k + 1) * D]


def _store_slabs(ref, x):
    for s in range(NS):
        ref[pl.ds(s, x.shape[0], stride=NS), :] = x[:, s * 128:(s + 1) * 128]


def _load_slabs(ref, rows):
    return jnp.concatenate([ref[pl.ds(s, rows, stride=NS), :] for s in range(NS)], axis=1)


def _dot_nt(a, b):
    return lax.dot_general(a, b, (((1,), (1,)), ((), ())), preferred_element_type=F32)


def _ffn_pre(x1, mrow, mod_ref, nf_ref, wrc_ref, wrh_ref, rb_ref, h2u_ref, ri_ref, rw_ref):
    h2 = _rms(x1, nf_ref[...]) * (1.0 + _mod(mod_ref, mrow, 4)) + _mod(mod_ref, mrow, 3)
    h_hi = h2.astype(BF16)
    h_lo = (h2 - h_hi.astype(F32)).astype(BF16)
    _store_slabs(h2u_ref, h2)

    hw = jnp.dot(h_hi, wrc_ref[...], preferred_element_type=F32)
    lw = jnp.dot(h_lo, wrh_ref[...], preferred_element_type=F32)
    lt = jnp.transpose(hw[:, :NRL] + hw[:, NRL:] + lw + rb_ref[...])
    row8 = lax.broadcasted_iota(I32, (8, TM), 0)
    gl = jnp.where(row8 < N_GROUPS, lt[0:8], -jnp.inf)
    ge = jnp.exp(gl - jnp.max(gl, axis=0, keepdims=True))
    gprob = ge / jnp.sum(ge, axis=0, keepdims=True)
    g_p = jnp.max(gprob, axis=0, keepdims=True)
    g_idx = jnp.min(jnp.where(gprob == g_p, row8, 8), axis=0, keepdims=True)
    e_sel = lt[ER0:ER0 + 8]
    for g in range(1, N_GROUPS):
        e_sel = jnp.where(g_idx == g, lt[ER0 + 8 * g:ER0 + 8 * g + 8], e_sel)
    ee = jnp.exp(e_sel - jnp.max(e_sel, axis=0, keepdims=True))
    eprob = ee / jnp.sum(ee, axis=0, keepdims=True)
    p0 = jnp.max(eprob, axis=0, keepdims=True)
    i0 = jnp.min(jnp.where(eprob == p0, row8, 8), axis=0, keepdims=True)
    rest = jnp.where(row8 == i0, -1.0, eprob)
    p1 = jnp.max(rest, axis=0, keepdims=True)
    i1 = jnp.min(jnp.where(rest == p1, row8, 8), axis=0, keepdims=True)
    psum = p0 + p1
    w0 = g_p * p0 / psum
    w1 = g_p * p1 / psum
    id0 = g_idx * EXP_PER_GROUP + i0
    id1 = g_idx * EXP_PER_GROUP + i1

    ri_ref[...] = jnp.where(row8 == 0, id0, jnp.where(row8 == 1, id1, 0))
    rw_ref[...] = jnp.where(row8 == 0, w0, jnp.where(row8 == 1, w1, 0.0))


def _ffn_in_specs():
    const2 = lambda i: (0, 0)
    return [
        pl.BlockSpec((1, D), const2),
        pl.BlockSpec((D, 2 * NRL), const2),
        pl.BlockSpec((D, NRL), const2),
        pl.BlockSpec((1, NRL), const2),
    ]


def _ffn_out_shapes():
    return [
        jax.ShapeDtypeStruct((T, D), F32),
        jax.ShapeDtypeStruct((T * NS, 128), F32),
        jax.ShapeDtypeStruct((NT, 8, TM), I32),
        jax.ShapeDtypeStruct((NT, 8, TM), F32),
    ]


def _ffn_out_specs():
    return [
        pl.BlockSpec((TM, D), lambda i: (i, 0)),
        pl.BlockSpec((TM * NS, 128), lambda i: (i, 0)),
        pl.BlockSpec((None, 8, TM), lambda i: (i, 0, 0)),
        pl.BlockSpec((None, 8, TM), lambda i: (i, 0, 0)),
    ]


def _mod_kernel(cs_ref, w_ref, b_ref, o_ref):
    s = cs_ref[...]
    a = s * jax.nn.sigmoid(s)
    o_ref[...] = jnp.dot(a, w_ref[...], precision=_HI, preferred_element_type=F32) + b_ref[...]


def _modulation(cs, w_ada, b_ada):
    nb = 6
    return pl.pallas_call(
        _mod_kernel,
        out_shape=jax.ShapeDtypeStruct((DEPTH, 8, 6 * D), F32),
        grid=(DEPTH, nb),
        in_specs=[
            pl.BlockSpec((8, D), lambda l, n: (0, 0)),
            pl.BlockSpec((None, D, D), lambda l, n: (l, 0, n)),
            pl.BlockSpec((None, 1, D), lambda l, n: (l, 0, n)),
        ],
        out_specs=pl.BlockSpec((None, 8, D), lambda l, n: (l, 0, n)),
        compiler_params=_cparams(2, 32),
        name="modulation",
    )(cs, w_ada, b_ada.reshape(DEPTH, 1, 6 * D))


def _halo_specs(width):
    nb = T // HALO
    per = TM // HALO
    return [
        pl.BlockSpec((TM, width), lambda i: (i, 0)),
        pl.BlockSpec((HALO, width), lambda i: (jnp.maximum(i * per - 1, 0), 0)),
        pl.BlockSpec((HALO, width), lambda i: (jnp.minimum((i + 1) * per, nb - 1), 0)),
    ]


def _normed_halo(xt, xp, xn, i, mod_ref, nm_ref):
    is_ctx, j, mrow = _tile_info(i)
    g = nm_ref[...]
    sc = 1.0 + _mod(mod_ref, mrow, 1)
    sh = _mod(mod_ref, mrow, 0)
    pv = jnp.where(jnp.logical_and(jnp.logical_not(is_ctx), j > 0), 1.0, 0.0)
    nv = jnp.where(jnp.logical_and(jnp.logical_not(is_ctx), j < LT - 1), 1.0, 0.0)
    ht = _rms(xt, g) * sc + sh
    hp = (_rms(xp, g) * sc + sh) * pv
    hn = (_rms(xn, g) * sc + sh) * nv
    return ht, hp, hn, is_ctx, j, mrow


def _pool_kernel(split, *refs):
    if split:
        xc_ref, x_ref, xp_ref, xn_ref = refs[:4]
        refs = refs[4:]
    else:
        x_ref, xp_ref, xn_ref = refs[:3]
        refs = refs[3:]
    (mod_ref, nm_ref, band_ref, wp_ref, ps_ref, nf_ref, wrc_ref, wrh_ref, rb_ref,
     x1_ref, h2u_ref, ri_ref, rw_ref) = refs
    i = pl.program_id(0)
    x = x_ref[...]
    if split:
        x = jnp.where(i < NCT, xc_ref[...], x)
    ht, hp, hn, is_ctx, j, mrow = _normed_halo(x, xp_ref[...], xn_ref[...], i, mod_ref, nm_ref)
    hext = jnp.concatenate([ht, hp, hn, jnp.zeros((KB - TM - 2 * HALO, D), F32)], axis=0)
    e_hi = hext.astype(BF16)
    e_lo = (hext - e_hi.astype(F32)).astype(BF16)
    seq_len = jnp.where(is_ctx, CTX_LEN, LAT_LEN)
    t = j * TM + lax.broadcasted_iota(I32, (TM, POOL_CH), 0)
    outs = []
    for g, win in enumerate(POOL_WINDOWS):
        lo = win // 2
        hi = win - 1 - lo
        cols = slice(g * POOL_CH, (g + 1) * POOL_CH)
        band = band_ref[g]
        s = (jnp.dot(band, e_hi[:, cols], preferred_element_type=F32)
             + jnp.dot(band, e_lo[:, cols], preferred_element_type=F32))
        cnt = (jnp.minimum(t + hi, seq_len - 1) - jnp.maximum(t - lo, 0) + 1).astype(F32)
        d = s / cnt - ht[:, cols]
        outs.append(jnp.dot(d.astype(BF16), wp_ref[g], preferred_element_type=F32))
    o = jnp.concatenate(outs, axis=1) * ps_ref[...]
    x1 = x + _mod(mod_ref, mrow, 2) * o
    x1_ref[...] = x1
    _ffn_pre(x1, mrow, mod_ref, nf_ref, wrc_ref, wrh_ref, rb_ref, h2u_ref, ri_ref, rw_ref)


def _pool_band():
    band = np.zeros((len(POOL_WINDOWS), TM, KB), np.float32)
    r = np.arange(TM)[:, None]
    for g, win in enumerate(POOL_WINDOWS):
        lo = win // 2
        hi = win - 1 - lo
        pos = np.concatenate([np.arange(TM), np.arange(-HALO, 0), np.arange(TM, TM + HALO)])[None, :]
        band[g, :, :TM + 2 * HALO] = (pos >= r - lo) & (pos <= r + hi)
    return jnp.asarray(band, BF16)


def _split_halo_specs():
    nb = T_LAT // HALO
    per = TM // HALO
    lat = lambda i: jnp.maximum(i - NCT, 0)
    return [
        pl.BlockSpec((TM, D), lambda i: (jnp.minimum(i, NCT - 1), 0)),
        pl.BlockSpec((TM, D), lambda i: (lat(i), 0)),
        pl.BlockSpec((HALO, D), lambda i: (jnp.maximum(lat(i) * per - 1, 0), 0)),
        pl.BlockSpec((HALO, D), lambda i: (jnp.minimum((lat(i) + 1) * per, nb - 1), 0)),
    ]


def _pool_layer(xs, mod_l, nm, wp, ps, ffn_args):
    const2 = lambda i: (0, 0)
    const3 = lambda i: (0, 0, 0)
    split = isinstance(xs, tuple)
    x_args = (xs[0], xs[1], xs[1], xs[1]) if split else (xs, xs, xs)
    return pl.pallas_call(
        functools.partial(_pool_kernel, split),
        out_shape=_ffn_out_shapes(),
        grid=(NT,),
        in_specs=(_split_halo_specs() if split else _halo_specs(D)) + [
            pl.BlockSpec((8, 6 * D), const2),
            pl.BlockSpec((1, D), const2),
            pl.BlockSpec((len(POOL_WINDOWS), TM, KB), const3),
            pl.BlockSpec((len(POOL_WINDOWS), POOL_CH, POOL_CH), const3),
            pl.BlockSpec((1, D), const2),
        ] + _ffn_in_specs(),
        out_specs=_ffn_out_specs(),
        compiler_params=_cparams(1, 48),
        name="pool_mixer",
    )(*x_args, mod_l, nm, _pool_band(), wp.astype(BF16), ps, *ffn_args)


def _conv_kernel(x_ref, xp_ref, xn_ref, mod_ref, nm_ref, win_ref, cw_ref, wout_ref,
                 nf_ref, wrc_ref, wrh_ref, rb_ref,
                 x1_ref, h2u_ref, ri_ref, rw_ref):
    i = pl.program_id(0)
    x = x_ref[...]
    ht, hp, hn, is_ctx, j, mrow = _normed_halo(x, xp_ref[...], xn_ref[...], i, mod_ref, nm_ref)
    hext = jnp.concatenate([ht, hp, hn], axis=0).astype(BF16)
    bcv = jnp.dot(hext, win_ref[...], preferred_element_type=F32)
    b = bcv[:TM, :D]
    u = bcv[:, D:2 * D] * bcv[:, 2 * D:]
    um = u[:TM]
    u_before = u[TM + HALO - 1:TM + HALO]
    u_after = u[TM + HALO:TM + HALO + 1]
    row = lax.broadcasted_iota(I32, (TM, D), 0)
    up = jnp.where(row == 0, u_before, pltpu.roll(um, 1, 0))
    un = jnp.where(row == TM - 1, u_after, pltpu.roll(um, TM - 1, 0))
    cw = cw_ref[...]
    conv = up * cw[0:1] + um * cw[1:2] + un * cw[2:3]
    o = jnp.dot((b * conv).astype(BF16), wout_ref[...], preferred_element_type=F32)
    x1 = x + _mod(mod_ref, mrow, 2) * o
    x1_ref[...] = x1
    _ffn_pre(x1, mrow, mod_ref, nf_ref, wrc_ref, wrh_ref, rb_ref, h2u_ref, ri_ref, rw_ref)


def _conv_layer(x, mod_l, nm, w_in, cw, w_out, ffn_args):
    const2 = lambda i: (0, 0)
    cw8 = jnp.concatenate([cw, jnp.zeros((8 - cw.shape[0], D), F32)], axis=0)
    return pl.pallas_call(
        _conv_kernel,
        out_shape=_ffn_out_shapes(),
        grid=(NT,),
        in_specs=_halo_specs(D) + [
            pl.BlockSpec((8, 6 * D), const2),
            pl.BlockSpec((1, D), const2),
            pl.BlockSpec((D, 3 * D), const2),
            pl.BlockSpec((8, D), const2),
            pl.BlockSpec((D, D), const2),
        ] + _ffn_in_specs(),
        out_specs=_ffn_out_specs(),
        compiler_params=_cparams(1, 56),
        name="conv_mixer",
    )(x, x, x, mod_l, nm, w_in.astype(BF16), cw8, w_out.astype(BF16), *ffn_args)


W_CAT = Q_RANK + KV_RANK + 2 * QK_ROPE
QH = 2 * QK_NOPE
VH = 2 * V_DIM


def _mla_proj_kernel(x_ref, mod_ref, nm_ref, wcat_ref, qn_ref, kvn_ref, wq_ref, wqs_ref,
                     c_ref, s_ref, q_ref, ckv_ref, kpe_ref):
    i = pl.program_id(0)
    _, _, mrow = _tile_info(i)
    h = _rms(x_ref[...], nm_ref[...]) * (1.0 + _mod(mod_ref, mrow, 1)) + _mod(mod_ref, mrow, 0)
    y = jnp.dot(h.astype(BF16), wcat_ref[...], preferred_element_type=F32)
    cqn = _rms(y[:, :Q_RANK], qn_ref[...]).astype(BF16)
    q = jnp.dot(cqn, wq_ref[...], preferred_element_type=F32)
    qs = jnp.dot(cqn, wqs_ref[...], preferred_element_type=F32)
    cos = c_ref[...]
    sin = s_ref[...]
    parts = []
    for hh in range(HEADS):
        parts.append(q[:, hh * QH:hh * QH + QK_NOPE])
        parts.append(q[:, hh * QH + QK_NOPE:(hh + 1) * QH] * cos + qs[:, hh * 128:(hh + 1) * 128] * sin)
    q_ref[...] = (jnp.concatenate(parts, axis=1) * (SM_SCALE * LOG2E)).astype(BF16)
    ckv_ref[...] = _rms(y[:, Q_RANK:Q_RANK + KV_RANK], kvn_ref[...])
    k2 = y[:, Q_RANK + KV_RANK:]
    kr = k2 * cos + pltpu.roll(k2, QK_ROPE, 1) * sin
    lane = lax.broadcasted_iota(I32, (TM, 2 * QK_ROPE), 1)
    kpe_ref[...] = jnp.where(lane < QK_ROPE, kr, 0.0)


def _rope_tables():
    rows_n = LAT_LEN // GRID_W
    rows = jnp.repeat(jnp.arange(rows_n), GRID_W).astype(F32)
    cols = jnp.tile(jnp.arange(GRID_W), rows_n).astype(F32)
    inv = ROPE_BASE ** (-(jnp.arange(AXIS_FREQS, dtype=F32) / AXIS_FREQS))
    ang = jnp.stack([rows[:, None] * inv, cols[:, None] * inv], axis=1)
    cos, sin = jnp.cos(ang), jnp.sin(ang)
    c64 = jnp.concatenate([cos[:, 0], cos[:, 0], cos[:, 1], cos[:, 1]], axis=1)
    s64 = jnp.concatenate([-sin[:, 0], sin[:, 0], -sin[:, 1], sin[:, 1]], axis=1)
    c = jnp.concatenate([c64, jnp.ones((LAT_LEN, QK_ROPE), F32)], axis=1)
    s = jnp.concatenate([s64, jnp.zeros((LAT_LEN, QK_ROPE), F32)], axis=1)
    c = jnp.concatenate([c, jnp.ones((TM, 2 * QK_ROPE), F32)], axis=0)
    s = jnp.concatenate([s, jnp.zeros((TM, 2 * QK_ROPE), F32)], axis=0)
    return c, s


def _swap_rope_cols(w):
    f = AXIS_FREQS
    return jnp.concatenate([w[..., f:2 * f], w[..., :f], w[..., 3 * f:], w[..., 2 * f:3 * f]], axis=-1)


def _mla_proj(x, mod_l, nm, w_dq, q_norm, w_uq, w_dkv, kv_norm):
    const2 = lambda i: (0, 0)
    w_kpe = w_dkv[:, KV_RANK:]
    wcat = jnp.concatenate([w_dq, w_dkv[:, :KV_RANK], w_kpe, _swap_rope_cols(w_kpe)], axis=1).astype(BF16)
    wq = w_uq.reshape(Q_RANK, HEADS, QK_NOPE + QK_ROPE)
    zpad = jnp.zeros((Q_RANK, HEADS, QK_ROPE), F32)
    wq_a = jnp.concatenate([wq, zpad], axis=2).reshape(Q_RANK, HEADS * QH).astype(BF16)
    wq_s = jnp.concatenate([_swap_rope_cols(wq[:, :, QK_NOPE:]), zpad], axis=2)
    wq_s = wq_s.reshape(Q_RANK, HEADS * 128).astype(BF16)
    cos, sin = _rope_tables()
    tab_idx = lambda i: (jnp.where(i < NCT, LT, jnp.maximum(i - NCT, 0) & (LT - 1)), 0)
    return pl.pallas_call(
        _mla_proj_kernel,
        out_shape=[jax.ShapeDtypeStruct((T, HEADS * QH), BF16),
                   jax.ShapeDtypeStruct((T, KV_RANK), F32),
                   jax.ShapeDtypeStruct((T, 2 * QK_ROPE), F32)],
        grid=(NT,),
        in_specs=[
            pl.BlockSpec((TM, D), lambda i: (i, 0)),
            pl.BlockSpec((8, 6 * D), const2),
            pl.BlockSpec((1, D), const2),
            pl.BlockSpec((D, W_CAT), const2),
            pl.BlockSpec((1, Q_RANK), const2),
            pl.BlockSpec((1, KV_RANK), const2),
            pl.BlockSpec((Q_RANK, HEADS * QH), const2),
            pl.BlockSpec((Q_RANK, HEADS * 128), const2),
            pl.BlockSpec((TM, 2 * QK_ROPE), tab_idx),
            pl.BlockSpec((TM, 2 * QK_ROPE), tab_idx),
        ],
        out_specs=[pl.BlockSpec((TM, HEADS * QH), lambda i: (i, 0)),
                   pl.BlockSpec((TM, KV_RANK), lambda i: (i, 0)),
                   pl.BlockSpec((TM, 2 * QK_ROPE), lambda i: (i, 0))],
        compiler_params=_cparams(1, 48),
        name="mla_proj",
    )(x, mod_l, nm, wcat, q_norm, kv_norm, wq_a, wq_s, cos, sin)


def _kv_expand_kernel(ckv_ref, kpe_ref, wk_ref, wv_ref, k_ref, v_ref):
    c = ckv_ref[...].astype(BF16)
    kn = jnp.dot(c, wk_ref[...], preferred_element_type=F32).astype(BF16)
    kp = kpe_ref[...].astype(BF16)
    parts = []
    for hh in range(HEADS):
        parts.append(kn[:, hh * QK_NOPE:(hh + 1) * QK_NOPE])
        parts.append(kp)
    k_ref[...] = jnp.concatenate(parts, axis=1)
    v = jnp.dot(c, wv_ref[...], preferred_element_type=F32).astype(BF16)
    lane = lax.broadcasted_iota(I32, (TM, VH - V_DIM), 1)
    one = jnp.where(lane == 0, 1.0, 0.0).astype(BF16)
    parts = []
    for hh in range(HEADS):
        parts.append(v[:, hh * V_DIM:(hh + 1) * V_DIM])
        parts.append(one)
    v_ref[...] = jnp.concatenate(parts, axis=1)


def _kv_expand(ckv, kpe, wk, wv, name):
    n = ckv.shape[0]
    const2 = lambda i: (0, 0)
    return pl.pallas_call(
        _kv_expand_kernel,
        out_shape=[jax.ShapeDtypeStruct((n, HEADS * QH), BF16),
                   jax.ShapeDtypeStruct((n, HEADS * VH), BF16)],
        grid=(n // TM,),
        in_specs=[
            pl.BlockSpec((TM, KV_RANK), lambda i: (i, 0)),
            pl.BlockSpec((TM, 2 * QK_ROPE), lambda i: (i, 0)),
            pl.BlockSpec((KV_RANK, HEADS * QK_NOPE), const2),
            pl.BlockSpec((KV_RANK, HEADS * V_DIM), const2),
        ],
        out_specs=[pl.BlockSpec((TM, HEADS * QH), lambda i: (i, 0)),
                   pl.BlockSpec((TM, HEADS * VH), lambda i: (i, 0))],
        compiler_params=_cparams(1, 32),
        name=name,
    )(ckv, kpe, wk, wv)


def _attend(q, k, v):
    s = _dot_nt(q, k)
    p = jnp.exp2(s - jnp.max(s, axis=1, keepdims=True))
    o = jnp.dot(p.astype(BF16), v, preferred_element_type=F32)
    return o[:, :V_DIM] / o[:, V_DIM:V_DIM + 1]


KC = 768
NKC = NK_LAT // KC


def _attn_lat_kernel(q_ref, k_ref, v_ref, o_ref, s_ref, m_ref, acc_ref):
    q = q_ref[...]
    m_ref[...] = jnp.full((TQ, 128), -jnp.inf, F32)

    def scores(c, carry):
        k = k_ref[pl.ds(pl.multiple_of(c * KC, KC), KC), :]
        s = _dot_nt(q, k)
        s_ref[c] = s
        m = m_ref[...]
        for j in range(KC // 128):
            m = jnp.maximum(m, s[:, j * 128:(j + 1) * 128])
        m_ref[...] = m
        return carry
    lax.fori_loop(0, NKC, scores, 0, unroll=True)

    mb = jnp.broadcast_to(jnp.max(m_ref[...], axis=1, keepdims=True), (TQ, 128))
    acc_ref[...] = jnp.zeros((TQ, VH), F32)

    def weighted(c, carry):
        s = s_ref[c]
        p = jnp.concatenate([jnp.exp2(s[:, j * 128:(j + 1) * 128] - mb) for j in range(KC // 128)], axis=1)
        v = v_ref[pl.ds(pl.multiple_of(c * KC, KC), KC), :]
        acc_ref[...] += jnp.dot(p.astype(BF16), v, preferred_element_type=F32)
        return carry
    lax.fori_loop(0, NKC, weighted, 0, unroll=3)

    acc = acc_ref[...]
    o_ref[...] = (acc[:, :V_DIM] / acc[:, V_DIM:V_DIM + 1]).astype(BF16)


def _attn_lat(q, k, v):
    nq = LAT_LEN // TQ
    q0 = T_CTX // TQ
    return pl.pallas_call(
        _attn_lat_kernel,
        scratch_shapes=[pltpu.VMEM((NKC, TQ, KC), F32), pltpu.VMEM((TQ, 128), F32), pltpu.VMEM((TQ, VH), F32)],
        out_shape=jax.ShapeDtypeStruct((T_LAT, HEADS * V_DIM), BF16),
        grid=(N_LAT_SEQ, HEADS, nq),
        in_specs=[
            pl.BlockSpec((TQ, QH), lambda b, h, t: (q0 + b * nq + t, h)),
            pl.BlockSpec((NK_LAT, QH), lambda b, h, t: (b, h)),
            pl.BlockSpec((NK_LAT, VH), lambda b, h, t: (b, h)),
        ],
        out_specs=pl.BlockSpec((TQ, V_DIM), lambda b, h, t: (b * nq + t, h)),
        compiler_params=_cparams(3, 48),
        name="attn_latent",
    )(q, k, v)


def _attn_ctx_kernel(q_ref, k_ref, v_ref, o_ref):
    outs = []
    for hh in range(HEADS):
        outs.append(_attend(q_ref[:, hh * QH:(hh + 1) * QH], k_ref[:, hh * QH:(hh + 1) * QH],
                            v_ref[:, hh * VH:(hh + 1) * VH]))
    o_ref[...] = jnp.concatenate(outs, axis=1).astype(BF16)


def _attn_ctx(q, k, v):
    return pl.pallas_call(
        _attn_ctx_kernel,
        out_shape=jax.ShapeDtypeStruct((T_CTX, HEADS * V_DIM), BF16),
        grid=(N_CTX_SEQ,),
        in_specs=[
            pl.BlockSpec((CTX_LEN, HEADS * QH), lambda b: (b, 0)),
            pl.BlockSpec((CTX_LEN, HEADS * QH), lambda b: (b, 0)),
            pl.BlockSpec((CTX_LEN, HEADS * VH), lambda b: (b, 0)),
        ],
        out_specs=pl.BlockSpec((CTX_LEN, HEADS * V_DIM), lambda b: (b, 0)),
        compiler_params=_cparams(1, 32),
        name="attn_context",
    )(q, k, v)


def _attn_out_kernel(x_ref, oc_ref, ol_ref, mod_ref, wo_ref,
                     nf_ref, wrc_ref, wrh_ref, rb_ref,
                     x1_ref, h2u_ref, ri_ref, rw_ref):
    i = pl.program_id(0)
    is_ctx, _, mrow = _tile_info(i)
    att = jnp.where(is_ctx, oc_ref[...], ol_ref[...])
    o = jnp.dot(att, wo_ref[...], preferred_element_type=F32)
    x1 = x_ref[...] + _mod(mod_ref, mrow, 2) * o
    x1_ref[...] = x1
    _ffn_pre(x1, mrow, mod_ref, nf_ref, wrc_ref, wrh_ref, rb_ref, h2u_ref, ri_ref, rw_ref)


def _attn_out(x, o_ctx, o_lat, mod_l, w_o, ffn_args):
    const2 = lambda i: (0, 0)
    return pl.pallas_call(
        _attn_out_kernel,
        out_shape=_ffn_out_shapes(),
        grid=(NT,),
        in_specs=[
            pl.BlockSpec((TM, D), lambda i: (i, 0)),
            pl.BlockSpec((TM, D), lambda i: (jnp.minimum(i, NCT - 1), 0)),
            pl.BlockSpec((TM, D), lambda i: (jnp.maximum(i - NCT, 0), 0)),
            pl.BlockSpec((8, 6 * D), const2),
            pl.BlockSpec((D, D), const2),
        ] + _ffn_in_specs(),
        out_specs=_ffn_out_specs(),
        compiler_params=_cparams(1, 48),
        name="attn_out",
    )(x, o_ctx, o_lat, mod_l, w_o.astype(BF16), *ffn_args)


def _mla_layer(x, mod_l, nm, cache_ckv, cache_kpe, w_dq, q_norm, w_uq, w_dkv, kv_norm, w_ukv, w_o, ffn_args):
    q, ckv, kpe = _mla_proj(x, mod_l, nm, w_dq, q_norm, w_uq, w_dkv, kv_norm)
    wkv = w_ukv.reshape(KV_RANK, HEADS, QK_NOPE + V_DIM)
    wk = wkv[:, :, :QK_NOPE].reshape(KV_RANK, HEADS * QK_NOPE).astype(BF16)
    wv = wkv[:, :, QK_NOPE:].reshape(KV_RANK, HEADS * V_DIM).astype(BF16)
    k_c, v_c = _kv_expand(ckv[:T_CTX], kpe[:T_CTX], wk, wv, "kv_expand_context")
    kpe_cache = jnp.concatenate([cache_kpe, jnp.zeros_like(cache_kpe)], axis=-1)
    ckv_l = jnp.concatenate([cache_ckv, ckv[T_CTX:].reshape(N_LAT_SEQ, LAT_LEN, KV_RANK)], axis=1)
    kpe_l = jnp.concatenate([kpe_cache, kpe[T_CTX:].reshape(N_LAT_SEQ, LAT_LEN, 2 * QK_ROPE)], axis=1)
    k_l, v_l = _kv_expand(ckv_l.reshape(N_LAT_SEQ * NK_LAT, KV_RANK),
                          kpe_l.reshape(N_LAT_SEQ * NK_LAT, 2 * QK_ROPE), wk, wv, "kv_expand_latent")
    o_c = _attn_ctx(q, k_c, v_c)
    o_l = _attn_lat(q, k_l, v_l)
    outs = _attn_out(x, o_c, o_l, mod_l, w_o, ffn_args)
    new_ckv = ckv[:T_CTX].reshape(N_CTX_SEQ, 1, CTX_LEN, KV_RANK)
    new_kpe = kpe[:T_CTX, :QK_ROPE].reshape(N_CTX_SEQ, 1, CTX_LEN, QK_ROPE)
    return outs, new_ckv, new_kpe


GROUP = 8


def _slab(ref, row, n=1):
    return ref.at[pl.ds(pl.multiple_of(row * NS, NS), n * NS)]


def _dispatch_kernel(meta_ref, psv_ref, ri_ref, h2_ref, xs_ref, pos_ref, stage, zbuf, hbuf, *rest):
    bufs, (sem, dsem, zsem, hsem) = rest[:-4], rest[-4:]
    i = pl.program_id(0)

    def row_copies(b):
        return [pltpu.make_async_copy(stage.at[b, row], bufs[b * 2 * LC + row], sem.at[b])
                for row in range(2 * LC)]

    def tile_fetch(tile):
        return pltpu.make_async_copy(_slab(h2_ref, tile * TM, TM), hbuf.at[tile % 3], hsem.at[tile % 3])

    def tile_rows_done(b):
        return pltpu.make_async_copy(_slab(h2_ref, 0, 2 * TM), _slab(xs_ref, 0, 2 * TM), dsem.at[b])

    @pl.when(i == 0)
    def _():
        zbuf[...] = jnp.zeros(zbuf.shape, F32)

        def pad_fill(e):
            return pltpu.make_async_copy(zbuf, _slab(xs_ref, meta_ref[2, e] - TS, TS), zsem)

        def start(e, c):
            @pl.when(meta_ref[2, e] > meta_ref[0, e])
            def _():
                pad_fill(e).start()
            return c

        def wait(e, c):
            @pl.when(meta_ref[2, e] > meta_ref[0, e])
            def _():
                pad_fill(e).wait()
            return c
        lax.fori_loop(0, N_EXPERTS, start, 0)
        lax.fori_loop(0, N_EXPERTS, wait, 0)

        def tail_fill(tile):
            return pltpu.make_async_copy(zbuf, _slab(xs_ref, tile * TS, TS), zsem)

        def tail_start(tile, c):
            tail_fill(tile).start()
            return c

        def tail_wait(tile, c):
            tail_fill(tile).wait()
            return c
        first_unused = lax.shift_right_logical(meta_ref[2, N_EXPERTS - 1], jnp.int32(TS.bit_length() - 1))
        lax.fori_loop(first_unused, NST, tail_start, 0)
        lax.fori_loop(first_unused, NST, tail_wait, 0)

    @pl.when(i < NT)
    def _():
        tile_fetch(i).start()
        ri = ri_ref[...]
        rowe = lax.broadcasted_iota(I32, (N_EXPERTS, TM), 0)
        ps = psv_ref[...]
        pos0 = jnp.sum(jnp.where(rowe == ri[0:1], ps, 0), axis=0, keepdims=True) + ri[2:3]
        pos1 = jnp.sum(jnp.where(rowe == ri[1:2], ps, 0), axis=0, keepdims=True) + ri[3:4]
        row8 = lax.broadcasted_iota(I32, (8, 128), 0)
        st = jnp.zeros((8, 128), I32)
        for k, pos in enumerate((pos0, pos1)):
            for h in range(LC):
                st = jnp.where(row8 == k * LC + h, pos[:, h * 128:(h + 1) * 128], st)
        pos_ref[...] = st
        for b in range(2):
            @pl.when(i % 2 == b)
            def _(b=b):
                stage[b] = st
                for c in row_copies(b):
                    c.start()

    j = i - 1
    for b in range(2):
        @pl.when(jnp.logical_and(i >= 1, j % 2 == b))
        def _(b=b):
            for c in row_copies(b):
                c.wait()
            tile_fetch(j).wait()
            rows = hbuf.at[j % 3]
            for h in range(LC):
                def toks(g, c, h=h):
                    t0 = g * GROUP
                    slots = [[bufs[b * 2 * LC + k * LC + h][t0 + u] for k in range(2)] for u in range(GROUP)]
                    for u in range(GROUP):
                        src = _slab(rows, h * 128 + t0 + u)
                        for k in range(2):
                            pltpu.make_async_copy(src, _slab(xs_ref, slots[u][k]), dsem.at[b]).start(priority=k)
                    return c
                lax.fori_loop(0, 128 // GROUP, toks, 0)

            @pl.when(j >= 1)
            def _():
                tile_rows_done(1 - b).wait()

            @pl.when(i == NT)
            def _():
                tile_rows_done(b).wait()


def _rank_kernel(ri_ref, trio_ref, ro_ref, cnt_ref, carry_ref):
    @pl.when(pl.program_id(0) == 0)
    def _():
        carry_ref[...] = jnp.zeros_like(carry_ref)

    rowe = lax.broadcasted_iota(I32, (N_EXPERTS, TM), 0)
    row8 = lax.broadcasted_iota(I32, (8, TM), 0)
    ids = [ri_ref[tt] for tt in range(RG)]
    ohs = [rowe == ids[tt][k:k + 1] for tt in range(RG) for k in range(2)]
    ohb = jnp.concatenate([jnp.where(oh, 1.0, 0.0).astype(BF16) for oh in ohs], axis=0)
    pt = jnp.dot(ohb, trio_ref[...], preferred_element_type=F32)
    carry = carry_ref[...]
    for tt in range(RG):
        ranks = []
        for k in range(2):
            r0 = (tt * 2 + k) * N_EXPERTS
            before = carry + pt[r0:r0 + N_EXPERTS, :TM]
            ranks.append(jnp.sum(jnp.where(ohs[tt * 2 + k], before, 0.0), axis=0, keepdims=True).astype(I32))
            carry = carry + pt[r0:r0 + N_EXPERTS, TM:]
        ro_ref[tt] = jnp.where(row8 == 0, ids[tt][0:1],
                     jnp.where(row8 == 1, ids[tt][1:2],
                     jnp.where(row8 == 2, ranks[0],
                     jnp.where(row8 == 3, ranks[1], 0))))
    carry_ref[...] = carry
    cnt_ref[...] = carry


def _rank(route_i):
    tri = np.triu(np.ones((TM, TM), np.float32), 1)
    trio = jnp.asarray(np.concatenate([tri, np.ones((TM, TM), np.float32)], axis=1), BF16)
    return pl.pallas_call(
        _rank_kernel,
        out_shape=[jax.ShapeDtypeStruct((NT, 8, TM), I32), jax.ShapeDtypeStruct((N_EXPERTS, TM), F32)],
        grid=(NT // RG,),
        in_specs=[pl.BlockSpec((RG, 8, TM), lambda g: (g, 0, 0)),
                  pl.BlockSpec((TM, 2 * TM), lambda g: (0, 0))],
        out_specs=[pl.BlockSpec((RG, 8, TM), lambda g: (g, 0, 0)),
                   pl.BlockSpec((N_EXPERTS, TM), lambda g: (0, 0))],
        scratch_shapes=[pltpu.VMEM((N_EXPERTS, TM), F32)],
        compiler_params=_cparams(1, 16),
        name="expert_rank",
    )(route_i, trio)


def _dispatch(meta, pad_starts, route_i, h2):
    tile = lambda i: (jnp.minimum(i, NT - 1), 0, 0)
    return pl.pallas_call(
        _dispatch_kernel,
        out_shape=[jax.ShapeDtypeStruct((CAP * NS, 128), F32),
                   jax.ShapeDtypeStruct((NT, 8, 128), I32)],
        grid=(NT + 1,),
        in_specs=[pl.BlockSpec(memory_space=pltpu.SMEM),
                  pl.BlockSpec((N_EXPERTS, TM), lambda i: (0, 0)),
                  pl.BlockSpec((None, 8, TM), tile),
                  pl.BlockSpec(memory_space=pl.ANY)],
        out_specs=[pl.BlockSpec(memory_space=pl.ANY), pl.BlockSpec((None, 8, 128), tile)],
        scratch_shapes=([pltpu.VMEM((2, 8, 128), I32), pltpu.VMEM((TS * NS, 128), F32),
                         pltpu.VMEM((3, TM * NS, 128), F32)]
                        + [pltpu.SMEM((128,), I32)] * (4 * LC)
                        + [pltpu.SemaphoreType.DMA((2,)), pltpu.SemaphoreType.DMA((2,)), pltpu.SemaphoreType.DMA,
                           pltpu.SemaphoreType.DMA((3,))]),
        compiler_params=_cparams(1, 16),
        name="moe_dispatch",
    )(meta, jnp.broadcast_to(pad_starts[:, None], (N_EXPERTS, TM)), route_i, h2)


def _moe_kernel(blk_ref, nu_ref, xs_ref, w1_ref, w3_ref, w2_ref, o_ref, wb1, wb3, wb2, pe_ref):
    i = pl.program_id(0)
    n_used = nu_ref[0]

    @pl.when(i == 0)
    def _():
        pe_ref[0] = -1

    @pl.when(i >= n_used)
    def _():
        o_ref[...] = jnp.zeros(o_ref.shape, F32)

    @pl.when(i < n_used)
    def _():
        e = blk_ref[i]

        @pl.when(e != pe_ref[0])
        def _():
            wb1[...] = w1_ref[...].astype(BF16)
            wb3[...] = w3_ref[...].astype(BF16)
            wb2[...] = w2_ref[...].astype(BF16)
            pe_ref[0] = e

        xb = _load_slabs(xs_ref, TS).astype(BF16)
        a = jnp.dot(xb, wb1[...], preferred_element_type=F32)
        b = jnp.dot(xb, wb3[...], preferred_element_type=F32)
        hm = (a * jax.nn.sigmoid(a) * b).astype(BF16)
        _store_slabs(o_ref, jnp.dot(hm, wb2[...], preferred_element_type=F32))


def _moe_experts(l, blk_e, n_used, xs, w1, w3, w2):
    grid_spec = pltpu.PrefetchScalarGridSpec(
        num_scalar_prefetch=2,
        grid=(NST,),
        in_specs=[
            pl.BlockSpec((TS * NS, 128), lambda i, blk, nu: (jnp.minimum(i, nu[0] - 1), 0)),
            pl.BlockSpec((None, None, D, D_EXPERT), lambda i, blk, nu: (l, blk[i], 0, 0)),
            pl.BlockSpec((None, None, D, D_EXPERT), lambda i, blk, nu: (l, blk[i], 0, 0)),
            pl.BlockSpec((None, None, D_EXPERT, D), lambda i, blk, nu: (l, blk[i], 0, 0)),
        ],
        out_specs=pl.BlockSpec((TS * NS, 128), lambda i, blk, nu: (i, 0)),
        scratch_shapes=[
            pltpu.VMEM((D, D_EXPERT), BF16),
            pltpu.VMEM((D, D_EXPERT), BF16),
            pltpu.VMEM((D_EXPERT, D), BF16),
            pltpu.SMEM((1,), I32),
        ],
    )
    return pl.pallas_call(
        _moe_kernel,
        out_shape=jax.ShapeDtypeStruct((CAP * NS, 128), F32),
        grid_spec=grid_spec,
        compiler_params=_cparams(1, 48),
        name="moe_experts",
    )(blk_e, n_used, xs, w1, w3, w2)


def _lane_to_col(row):
    r = lax.broadcasted_iota(I32, (TM, TM), 0)
    c = lax.broadcasted_iota(I32, (TM, TM), 1)
    return jnp.sum(jnp.where(r == c, row, 0.0), axis=1, keepdims=True)


def _combine_kernel(final, x1_ref, pos_ref, os_ref, rw_ref, mod_ref, nfin_ref, *rest):
    n_out = 2 if final else 1
    y_refs = rest[:n_out]
    gbuf = rest[n_out]
    bufs = rest[n_out + 1:-2]
    psem, gsem = rest[-2:]
    i = pl.program_id(0)

    def pos_copies(tile, b):
        return [pltpu.make_async_copy(pos_ref.at[tile, row], bufs[b * 2 * LC + row], psem.at[b])
                for row in range(2 * LC)]

    def start_gathers(b):
        for k in range(2):
            for h in range(LC):
                def tok(t, c, k=k, h=h):
                    src = _slab(os_ref, bufs[b * 2 * LC + k * LC + h][t])
                    pltpu.make_async_copy(src, _slab(gbuf.at[2 * b + k], h * 128 + t), gsem.at[b]).start(priority=k)
                    return c
                lax.fori_loop(0, 128, tok, 0, unroll=8)

    def wait_gathers(b):
        for k in range(2):
            pltpu.make_async_copy(_slab(os_ref, 0, TM), gbuf.at[2 * b + k], gsem.at[b]).wait()

    @pl.when(i == 0)
    def _():
        for c in pos_copies(0, 0):
            c.start()
        for c in pos_copies(0, 0):
            c.wait()
        start_gathers(0)
        if NT > 1:
            for c in pos_copies(1, 1):
                c.start()

    for b in range(2):
        @pl.when(i % 2 == b)
        def _(b=b):
            @pl.when(i + 1 < NT)
            def _():
                for c in pos_copies(i + 1, 1 - b):
                    c.wait()
                start_gathers(1 - b)

            @pl.when(i + 2 < NT)
            def _():
                for c in pos_copies(i + 2, b):
                    c.start()
            wait_gathers(b)

    _, _, mrow = _tile_info(i)
    rw = rw_ref[...]
    w0 = _lane_to_col(rw[0:1])
    w1 = _lane_to_col(rw[1:2])
    g2 = _mod(mod_ref, mrow, 5)
    par = i % 2
    x2 = x1_ref[...] + g2 * (_load_slabs(gbuf.at[2 * par], TM) * w0 + _load_slabs(gbuf.at[2 * par + 1], TM) * w1)
    if final:
        yc_ref, yl_ref = y_refs
        y = _rms(x2, nfin_ref[...])

        @pl.when(i < NCT)
        def _():
            yc_ref[...] = y

        @pl.when(i >= NCT)
        def _():
            yl_ref[...] = y
    else:
        y_refs[0][...] = x2


def _combine(x1, pos, o_sorted, route_w, mod_l, norm_final, final):
    const2 = lambda i: (0, 0)
    if final:
        out_shape = [jax.ShapeDtypeStruct((T_CTX, D), F32), jax.ShapeDtypeStruct((T_LAT, D), F32)]
        out_specs = [pl.BlockSpec((TM, D), lambda i: (jnp.minimum(i, NCT - 1), 0)),
                     pl.BlockSpec((TM, D), lambda i: (jnp.maximum(i - NCT, 0), 0))]
    else:
        out_shape = [jax.ShapeDtypeStruct((T, D), F32)]
        out_specs = [pl.BlockSpec((TM, D), lambda i: (i, 0))]
    return pl.pallas_call(
        functools.partial(_combine_kernel, final),
        out_shape=out_shape,
        grid=(NT,),
        in_specs=[
            pl.BlockSpec((TM, D), lambda i: (i, 0)),
            pl.BlockSpec(memory_space=pl.ANY),
            pl.BlockSpec(memory_space=pl.ANY),
            pl.BlockSpec((None, 8, TM), lambda i: (i, 0, 0)),
            pl.BlockSpec((8, 6 * D), const2),
            pl.BlockSpec((1, D), const2),
        ],
        out_specs=out_specs,
        scratch_shapes=([pltpu.VMEM((4, TM * NS, 128), F32)] + [pltpu.SMEM((128,), I32)] * (4 * LC)
                        + [pltpu.SemaphoreType.DMA((2,)), pltpu.SemaphoreType.DMA((2,))]),
        compiler_params=_cparams(1, 32),
        name="moe_combine",
    )(x1, pos, o_sorted, route_w, mod_l, norm_final)


def _moe_layer(l, x1, h2u, route_ids, route_w, mod_l, w1, w3, w2, norm_final, final):
    route_i, counts = _rank(route_ids)
    sizes = counts[:, 0].astype(I32)
    padded = (sizes + TS - 1) // TS * TS
    pad_ends = jnp.cumsum(padded)
    pad_starts = pad_ends - padded
    n_used = (pad_ends[-1:] // TS).astype(I32)
    tile_start = jnp.arange(NST, dtype=I32) * TS
    blk_e = jnp.minimum(jnp.sum((pad_ends[None, :] <= tile_start[:, None]).astype(I32), axis=1),
                        N_EXPERTS - 1).astype(I32)
    meta = jnp.stack([pad_starts, pad_starts + sizes, pad_ends]).astype(I32)
    xs, pos = _dispatch(meta, pad_starts.astype(I32), route_i, h2u)
    o_sorted = _moe_experts(l, blk_e, n_used, xs, w1, w3, w2)
    return _combine(x1, pos, o_sorted, route_w, mod_l, norm_final, final)


def _routing_params(w_rg, b_rg, w_re, b_re):
    w = jnp.zeros((D, NRL), F32).at[:, :N_GROUPS].set(w_rg).at[:, ER0:ER0 + N_EXPERTS].set(w_re)
    hi = w.astype(BF16)
    lo = (w - hi.astype(F32)).astype(BF16)
    bias = jnp.zeros((1, NRL), F32).at[0, :N_GROUPS].set(b_rg).at[0, ER0:ER0 + N_EXPERTS].set(b_re)
    return jnp.concatenate([hi, lo], axis=1), hi, bias


def kernel(x_prompt, x_sample, cache_ckv, cache_kpe, c, c_ctx, norm_mix, norm_ffn, norm_final, w_ada, b_ada, w_pool, pool_scale, w_conv_in, conv_w, w_conv_out, w_dq, q_norm, w_uq, w_dkv, kv_norm, w_ukv, w_o, w_route_g, b_route_g, w_route_e, b_route_e, w1, w3, w2):
    x = (x_prompt.reshape(T_CTX, D), x_sample.reshape(T_LAT, D))
    cs =jnp.concatenate([c_ctx[None, :], c, jnp.zeros((8 - 1 - N_LAT_SEQ, D), F32)], axis=0)
    mod_all = _modulation(cs, w_ada, b_ada)
    nfin = norm_final[None, :]
    new_ckv = new_kpe = None
    for l in range(DEPTH):
        kind, j = l % 3, l // 3
        mod_l = mod_all[l]
        nm = norm_mix[l][None, :]
        ffn_args = (norm_ffn[l][None, :],) + _routing_params(w_route_g[l], b_route_g[l], w_route_e[l], b_route_e[l])
        if kind == 0:
            outs = _pool_layer(x, mod_l, nm, w_pool[j], pool_scale[j][None, :], ffn_args)
        elif kind == 1:
            outs = _conv_layer(x, mod_l, nm, w_conv_in[j], conv_w[j], w_conv_out[j], ffn_args)
        else:
            outs, new_ckv, new_kpe = _mla_layer(
                x, mod_l, nm, cache_ckv[:, j], cache_kpe[:, j], w_dq[j], q_norm[j][None, :], w_uq[j],
                w_dkv[j], kv_norm[j][None, :], w_ukv[j], w_o[j], ffn_args)
        x1, h2u, route_ids, route_w = outs
        ys = _moe_layer(l, x1, h2u, route_ids, route_w, mod_l, w1, w3, w2, nfin, l == DEPTH - 1)
        x = ys[0]
    y_prompt = ys[0].reshape(N_CTX_SEQ, CTX_LEN, D)
    y_sample = ys[1].reshape(N_LAT_SEQ, LAT_LEN, D)
    return (y_prompt, y_sample, new_ckv, new_kpe)
```
